```python
import jax, jax.numpy as jnp
from jax import lax
import numpy as np

D_MODEL = 2048
BATCH = 4
SEQ = 2048
DEPTH = 2
DEC_BATCH = 128
DEC_SEQ = 8
PAST_LEN = 16384
PAGE_SIZE = 128

N_META = 16
N_AB = (DEPTH + 1) // 2
N_C = DEPTH // 2
MIX_A = D_MODEL // 2
NH_A = 4
HD_A = MIX_A // NH_A
CHUNK_A = 128
RG_W = D_MODEL // 2
RG_BLOCKS = 8
RG_BW = RG_W // RG_BLOCKS
CONV_W = 4
RG_C = 8.0
PROJ_AB = 4 * MIX_A + 2 * NH_A + 2 * RG_W
HD_C = 64
NH_C = D_MODEL // HD_C
DECAY_LORA = 96
AAA_LORA = 96
GATE_LORA = 256
D_FF = 4 * D_MODEL
LN_EPS = 1e-5
GN_EPS = 64e-5
ALPHA = (2.0 * DEPTH) ** 0.25
BETA = (8.0 * DEPTH) ** -0.25

kernel_name = 'hybrid_mlstm_rglru_rwkv7_step'


def layer_norm(x, g, b, eps=LN_EPS):
    xf = x.astype(jnp.float32)
    mu = jnp.mean(xf, axis=-1, keepdims=True)
    var = jnp.mean(jnp.square(xf - mu), axis=-1, keepdims=True)
    return ((xf - mu) * lax.rsqrt(var + eps) * g + b).astype(x.dtype)


def sq_relu_mlp(x, w_up, w_down):
    return jnp.square(jax.nn.relu(x @ w_up)) @ w_down


def mlstm_chunk(carry, inp):
    C, n, m = carry
    q, k, v, ig, lf = inp
    L = q.shape[1]
    Fh = jnp.swapaxes(jnp.cumsum(lf, axis=1), 1, 2)
    igh = jnp.swapaxes(ig, 1, 2)
    logD = Fh[..., :, None] - Fh[..., None, :] + igh[..., None, :]
    logD = jnp.where(jnp.tril(jnp.ones((L, L), bool)), logD, -jnp.inf)
    b = Fh + m[..., None]
    m_row = jnp.maximum(b, jnp.max(logD, axis=-1))
    inter = jnp.exp(b - m_row)
    s = jnp.einsum('blhd,bmhd->bhlm', q, k) * jnp.exp(logD - m_row[..., None])
    num = jnp.einsum('bhlm,bmhd->bhld', s, v) + inter[..., None] * jnp.einsum('blhk,bhkv->bhlv', q, C)
    den = jnp.sum(s, axis=-1) + inter * jnp.einsum('blhk,bhk->bhl', q, n)
    h = num / jnp.maximum(jnp.abs(den), jnp.exp(-m_row))[..., None]
    FL = Fh[..., -1]
    wj = FL[..., None] - Fh + igh
    m_new = jnp.maximum(FL + m, jnp.max(wj, axis=-1))
    wexp = jnp.exp(wj - m_new[..., None])
    dec = jnp.exp(FL + m - m_new)
    C_new = dec[..., None, None] * C + jnp.einsum('bhl,blhk,blhv->bhkv', wexp, k, v)
    n_new = dec[..., None] * n + jnp.einsum('bhl,blhk->bhk', wexp, k)
    return (C_new, n_new, m_new), jnp.swapaxes(h, 1, 2)


def mlstm_seq(state, q, k, v, ig, lf, lead):
    B, T = q.shape[:2]
    ins = (q, k, v, ig, lf)
    if lead > 0:
        state, h_lead = mlstm_chunk(state, tuple(a[:, :lead] for a in ins))
        ins = tuple(a[:, lead:] for a in ins)
    rest = T - lead
    if rest % CHUNK_A == 0 and rest >= CHUNK_A:
        nc = rest // CHUNK_A
        split = lambda a: jnp.swapaxes(a.reshape((B, nc, CHUNK_A) + a.shape[2:]), 0, 1)
        state, h = lax.scan(mlstm_chunk, state, tuple(split(a) for a in ins))
        h = jnp.swapaxes(h, 0, 1)
        h = h.reshape((B, rest) + h.shape[3:])
    else:
        state, h = mlstm_chunk(state, ins)
    if lead > 0:
        h = jnp.concatenate([h_lead, h], axis=1)
    return state, h


def causal_conv(xb, buf, w, b):
    T = xb.shape[1]
    xp = jnp.concatenate([buf.astype(xb.dtype), xb], axis=1)
    out = b + sum(xp[:, j:j + T] * w[j] for j in range(CONV_W))
    return out, xp[:, T:]


def block_diag(x, w, b):
    xb = x.reshape(x.shape[:-1] + (RG_BLOCKS, RG_BW))
    return jnp.einsum('btni,nij->btnj', xb, w).reshape(x.shape) + b


def _lin_combine(e1, e2):
    return (e1[0] * e2[0], e2[0] * e1[1] + e2[1])


def rglru(x, h0, wa, ba, wx, bx, lam):
    r = jax.nn.sigmoid(block_diag(x, wa, ba))
    gi = jax.nn.sigmoid(block_diag(x, wx, bx))
    log_a = -RG_C * r * jax.nn.softplus(-lam.astype(jnp.float32))
    a = jnp.exp(log_a)
    u = jnp.sqrt(-jnp.expm1(2.0 * log_a)) * (gi * x)
    u = u.at[:, 0].add(a[:, 0] * h0)
    _, h = lax.associative_scan(_lin_combine, (a, u), axis=1)
    return h, h[:, -1]


def ab_mixer(x, C0, n0, m0, h0, conv0, p, i, lead):
    B, T, _ = x.shape
    z = x @ p['w_in_ab'][i]
    cuts = [MIX_A, 2 * MIX_A, 3 * MIX_A, 4 * MIX_A, 4 * MIX_A + 2 * NH_A, 4 * MIX_A + 2 * NH_A + RG_W]
    q, k, v, o, gates, xr, gr = jnp.split(z, cuts, axis=-1)
    heads = lambda t: t.astype(jnp.float32).reshape(B, T, NH_A, HD_A)
    gates = gates.astype(jnp.float32) + p['b_if_ab'][i].astype(jnp.float32)
    ig, lf = gates[..., :NH_A], jax.nn.log_sigmoid(gates[..., NH_A:])
    st0 = (C0.astype(jnp.float32), n0.astype(jnp.float32), m0.astype(jnp.float32))
    (C, n, m), hm = mlstm_seq(st0, heads(q), heads(k) * HD_A ** -0.5, heads(v), ig, lf, lead)
    mu = jnp.mean(hm, axis=-1, keepdims=True)
    var = jnp.mean(jnp.square(hm - mu), axis=-1, keepdims=True)
    hm = (hm - mu) * lax.rsqrt(var + LN_EPS) * p['mlstm_norm_g'][i].reshape(NH_A, HD_A)
    hm = hm.reshape(B, T, MIX_A) * jax.nn.sigmoid(o.astype(jnp.float32))
    xc, conv_new = causal_conv(xr, conv0, p['rg_conv_w'][i], p['rg_conv_b'][i])
    hr, h_last = rglru(xc.astype(jnp.float32), h0.astype(jnp.float32), p['rg_wa'][i], p['rg_ba'][i],
                       p['rg_wx'][i], p['rg_bx'][i], p['rg_lambda'][i])
    yr = hr * jax.nn.gelu(gr.astype(jnp.float32))
    y = jnp.concatenate([hm, yr], axis=-1).astype(x.dtype) @ p['w_out_ab'][i]
    return y.astype(x.dtype), C, n, m, h_last, conv_new


def rwkv_step(S, inp):
    r, w, k, v, a, b = inp
    sa = jnp.einsum('bhvk,bhk->bhv', S, a)
    S = S * w[:, :, None, :] + sa[..., None] * b[:, :, None, :] + v[..., None] * k[:, :, None, :]
    return S, jnp.einsum('bhvk,bhk->bhv', S, r)


def rwkv_mixer(x, S0, x_last, p, i):
    B, T, D = x.shape
    xf = x.astype(jnp.float32)
    x_prev = jnp.concatenate([x_last.astype(jnp.float32)[:, None], xf[:, :-1]], axis=1)
    xs = xf[:, :, None] + (x_prev - xf)[:, :, None] * p['rw_mu'][i]
    xr, xw, xk, xv, xa, xg = (xs[:, :, j] for j in range(6))
    r = xr @ p['rw_wr'][i]
    k = xk @ p['rw_wk'][i]
    v = xv @ p['rw_wv'][i]
    w = -jax.nn.softplus(-(p['rw_w0'][i] + jnp.tanh(xw @ p['rw_w1'][i]) @ p['rw_w2'][i])) - 0.5
    a = jax.nn.sigmoid(p['rw_a0'][i] + (xa @ p['rw_a1'][i]) @ p['rw_a2'][i])
    g = jax.nn.sigmoid(xg @ p['rw_g1'][i]) @ p['rw_g2'][i]
    heads = lambda t: t.astype(jnp.float32).reshape(B, T, NH_C, HD_C)
    kk = heads(k * p['rw_kk'][i])
    kk = kk / jnp.maximum(jnp.sqrt(jnp.sum(jnp.square(kk), axis=-1, keepdims=True)), 1e-12)
    k = k * (1.0 + (a - 1.0) * p['rw_ka'][i])
    rh, kh, vh = heads(r), heads(k), heads(v)
    seq = (rh, heads(jnp.exp(-jnp.exp(w))), kh, vh, -kk, kk * heads(a))
    S, y = lax.scan(rwkv_step, S0.astype(jnp.float32), tuple(jnp.moveaxis(t, 1, 0) for t in seq))
    y = jnp.moveaxis(y, 0, 1)
    mu = jnp.mean(y, axis=-1, keepdims=True)
    var = jnp.mean(jnp.square(y - mu), axis=-1, keepdims=True)
    yn = ((y - mu) * lax.rsqrt(var + GN_EPS)).reshape(B, T, D) * p['rw_lnx_g'][i] + p['rw_lnx_b'][i]
    bonus = jnp.sum(rh * kh * p['rw_rk'][i], axis=-1, keepdims=True) * vh
    out = ((yn + bonus.reshape(B, T, D)) * g).astype(x.dtype) @ p['rw_wo'][i]
    return out.astype(x.dtype), S, x[:, -1]


def trunk(h, states, p, lead):
    mC, mn, mm, rgh, rgc, rwS, rwx = states
    out = [[] for _ in range(7)]
    for layer in range(DEPTH):
        i = layer // 2
        if layer % 2 == 0:
            y, *new = ab_mixer(h, mC[i], mn[i], mm[i], rgh[i], rgc[i], p, i, lead)
            for lst, s in zip(out[:5], new):
                lst.append(s)
        else:
            y, *new = rwkv_mixer(h, rwS[i], rwx[i], p, i)
            for lst, s in zip(out[5:], new):
                lst.append(s)
        h = layer_norm(ALPHA * h + y, p['ln1_g'][layer], p['ln1_b'][layer])
        h = layer_norm(ALPHA * h + sq_relu_mlp(h, p['w_up'][layer], p['w_down'][layer]),
                       p['ln2_g'][layer], p['ln2_b'][layer])
    return h, [jnp.stack(lst) for lst in out]


def setup_inputs(seed: int = 0) -> dict:
    key = jax.random.key(seed)
    keys = list(jax.random.split(key, 64))
    nrm = lambda shape, scale: scale * jax.random.normal(keys.pop(), shape, jnp.float32)
    unif = lambda shape, lo, hi: jax.random.uniform(keys.pop(), shape, jnp.float32, lo, hi)
    D = D_MODEL
    s = unif((N_AB, RG_W), 0.9, 0.999) ** (1.0 / RG_C)
    return {
        'x_prompt': nrm((BATCH, SEQ, D), 1.0),
        'x_sample': nrm((DEC_BATCH, DEC_SEQ, D), 1.0),
        'state_mlstm_C': nrm((N_AB, DEC_BATCH, NH_A, HD_A, HD_A), 0.05),
        'state_mlstm_n': nrm((N_AB, DEC_BATCH, NH_A, HD_A), 0.05),
        'state_mlstm_m': nrm((N_AB, DEC_BATCH, NH_A), 1.0),
        'state_rglru_h': nrm((N_AB, DEC_BATCH, RG_W), 0.5),
        'state_rglru_conv': nrm((N_AB, DEC_BATCH, CONV_W - 1, RG_W), 1.0),
        'state_rwkv_S': nrm((N_C, DEC_BATCH, NH_C, HD_C, HD_C), 0.1),
        'state_rwkv_shift': nrm((N_C, DEC_BATCH, D), 1.0),
        'meta_tokens': nrm((N_META, D), 1.0),
        'w_in_ab': nrm((N_AB, D, PROJ_AB), D ** -0.5),
        'b_if_ab': jnp.concatenate([nrm((N_AB, NH_A), 0.1), 3.0 + unif((N_AB, NH_A), 0.0, 3.0)], axis=-1),
        'mlstm_norm_g': 1.0 + nrm((N_AB, MIX_A), 0.02),
        'rg_conv_w': nrm((N_AB, CONV_W, RG_W), CONV_W ** -0.5),
        'rg_conv_b': nrm((N_AB, RG_W), 0.02),
        'rg_wa': nrm((N_AB, RG_BLOCKS, RG_BW, RG_BW), RG_BW ** -0.5),
        'rg_ba': nrm((N_AB, RG_W), 0.02),
        'rg_wx': nrm((N_AB, RG_BLOCKS, RG_BW, RG_BW), RG_BW ** -0.5),
        'rg_bx': nrm((N_AB, RG_W), 0.02),
        'rg_lambda': jnp.log(s) - jnp.log1p(-s),
        'w_out_ab': nrm((N_AB, MIX_A + RG_W, D), (MIX_A + RG_W) ** -0.5 * BETA),
        'rw_mu': unif((N_C, 6, D), 0.0, 1.0),
        'rw_wr': nrm((N_C, D, D), D ** -0.5),
        'rw_wk': nrm((N_C, D, D), D ** -0.5),
        'rw_wv': nrm((N_C, D, D), D ** -0.5),
        'rw_wo': nrm((N_C, D, D), D ** -0.5 * BETA),
        'rw_w0': unif((N_C, D), -6.0, 0.5),
        'rw_w1': nrm((N_C, D, DECAY_LORA), D ** -0.5),
        'rw_w2': nrm((N_C, DECAY_LORA, D), 0.1 * DECAY_LORA ** -0.5),
        'rw_a0': nrm((N_C, D), 0.1),
        'rw_a1': nrm((N_C, D, AAA_LORA), D ** -0.5),
        'rw_a2': nrm((N_C, AAA_LORA, D), 0.1 * AAA_LORA ** -0.5),
        'rw_g1': nrm((N_C, D, GATE_LORA), D ** -0.5),
        'rw_g2': nrm((N_C, GATE_LORA, D), GATE_LORA ** -0.5),
        'rw_kk': 0.85 + nrm((N_C, D), 0.02),
        'rw_ka': 1.0 + nrm((N_C, D), 0.02),
        'rw_rk': nrm((N_C, NH_C, HD_C), 0.1),
        'rw_lnx_g': 1.0 + nrm((N_C, D), 0.02),
        'rw_lnx_b': nrm((N_C, D), 0.02),
        'ln1_g': 1.0 + nrm((DEPTH, D), 0.02),
        'ln1_b': nrm((DEPTH, D), 0.02),
        'ln2_g': 1.0 + nrm((DEPTH, D), 0.02),
        'ln2_b': nrm((DEPTH, D), 0.02),
        'w_up': nrm((DEPTH, D, D_FF), D ** -0.5),
        'w_down': nrm((DEPTH, D_FF, D), D_FF ** -0.5 * BETA),
    }


def reference(x_prompt, x_sample, state_mlstm_C, state_mlstm_n, state_mlstm_m, state_rglru_h,
              state_rglru_conv, state_rwkv_S, state_rwkv_shift, meta_tokens, w_in_ab, b_if_ab,
              mlstm_norm_g, rg_conv_w, rg_conv_b, rg_wa, rg_ba, rg_wx, rg_bx, rg_lambda, w_out_ab,
              rw_mu, rw_wr, rw_wk, rw_wv, rw_wo, rw_w0, rw_w1, rw_w2, rw_a0, rw_a1, rw_a2, rw_g1,
              rw_g2, rw_kk, rw_ka, rw_rk, rw_lnx_g, rw_lnx_b, ln1_g, ln1_b, ln2_g, ln2_b, w_up, w_down):
    p = dict(w_in_ab=w_in_ab, b_if_ab=b_if_ab, mlstm_norm_g=mlstm_norm_g, rg_conv_w=rg_conv_w,
             rg_conv_b=rg_conv_b, rg_wa=rg_wa, rg_ba=rg_ba, rg_wx=rg_wx, rg_bx=rg_bx,
             rg_lambda=rg_lambda, w_out_ab=w_out_ab, rw_mu=rw_mu, rw_wr=rw_wr, rw_wk=rw_wk,
             rw_wv=rw_wv, rw_wo=rw_wo, rw_w0=rw_w0, rw_w1=rw_w1, rw_w2=rw_w2, rw_a0=rw_a0,
             rw_a1=rw_a1, rw_a2=rw_a2, rw_g1=rw_g1, rw_g2=rw_g2, rw_kk=rw_kk, rw_ka=rw_ka,
             rw_rk=rw_rk, rw_lnx_g=rw_lnx_g, rw_lnx_b=rw_lnx_b, ln1_g=ln1_g, ln1_b=ln1_b,
             ln2_g=ln2_g, ln2_b=ln2_b, w_up=w_up, w_down=w_down)
    B = x_prompt.shape[0]
    dt = x_prompt.dtype
    zero_states = (
        jnp.zeros((N_AB, B, NH_A, HD_A, HD_A), jnp.float32),
        jnp.zeros((N_AB, B, NH_A, HD_A), jnp.float32),
        jnp.zeros((N_AB, B, NH_A), jnp.float32),
        jnp.zeros((N_AB, B, RG_W), jnp.float32),
        jnp.zeros((N_AB, B, CONV_W - 1, RG_W), dt),
        jnp.zeros((N_C, B, NH_C, HD_C, HD_C), jnp.float32),
        jnp.zeros((N_C, B, D_MODEL), dt),
    )
    meta = jnp.broadcast_to(meta_tokens.astype(dt)[None], (B, N_META, D_MODEL))
    hp, ps = trunk(jnp.concatenate([meta, x_prompt], axis=1), zero_states, p, N_META)
    sample_states = (state_mlstm_C, state_mlstm_n, state_mlstm_m, state_rglru_h, state_rglru_conv,
                     state_rwkv_S, state_rwkv_shift)
    hs, ss = trunk(x_sample, sample_states, p, 0)
    return (hp[:, N_META:], hs, ps[0], ps[1], ps[2], ps[3], ps[4], ps[5], ps[6],
            ss[0], ss[1], ss[2], ss[3], ss[4], ss[5], ss[6])
```

```python
import functools

import jax
import jax.numpy as jnp
from jax import lax
from jax.experimental import pallas as pl
from jax.experimental.pallas import tpu as pltpu

f32 = jnp.float32
bf16 = jnp.bfloat16
HI = lax.Precision.HIGHEST

D_MODEL = 2048
N_META = 16
MIX_A = 1024
NH_A = 4
HD_A = 256
RG_W = 1024
RG_BLOCKS = 8
RG_BW = 128
CONV_W = 4
RG_C = 8.0
HD_C = 64
NH_C = 32
D_FF = 8192
LN_EPS = 1e-5
GN_EPS = 64e-5
DEPTH = 2
ALPHA = (2.0 * DEPTH) ** 0.25

LANES = 128
SUBLANES = 8
VMEM_LIMIT_BYTES = 56 * 1024 * 1024

MLSTM_CHUNK = 128
RWKV_CHUNK = 64
LORA_PAD = 128
NEG_BIG = -1e30

NT_DIMS = (((1,), (1,)), ((), ()))
TN_DIMS = (((0,), (0,)), ((), ()))


def _cparams(*sem):
    return pltpu.CompilerParams(dimension_semantics=sem, vmem_limit_bytes=VMEM_LIMIT_BYTES)


def _pick_tile(n, cap):
    best = None
    for d in range(SUBLANES, min(n, cap) + 1, SUBLANES):
        if n % d == 0:
            best = d
    assert best is not None, (n, cap)
    return best


def _dot_bf16(a, b):
    return jnp.dot(a.astype(bf16), b.astype(bf16), preferred_element_type=f32)


def _dot_hi(a, b, dims=None):
    if dims is None:
        return jnp.dot(a, b, precision=HI, preferred_element_type=f32)
    return lax.dot_general(a, b, dims, precision=HI, preferred_element_type=f32)


def _layer_norm_rows(t, g, b):
    mu = jnp.mean(t, axis=-1, keepdims=True)
    d = t - mu
    var = jnp.mean(d * d, axis=-1, keepdims=True)
    return d * lax.rsqrt(var + LN_EPS) * g + b


def _linear_kernel(x_ref, w_ref, o_ref, xb_ref):
    @pl.when(pl.program_id(1) == 0)
    def _():
        xb_ref[...] = x_ref[...].astype(bf16)

    o_ref[...] = jnp.dot(xb_ref[...], w_ref[...], preferred_element_type=f32)


def _linear(x, w, n_out, tm_cap=512, tn=1024):
    M, K = x.shape
    tm = _pick_tile(M, tm_cap)
    return pl.pallas_call(
        _linear_kernel,
        out_shape=jax.ShapeDtypeStruct((M, n_out), f32),
        grid=(M // tm, n_out // tn),
        in_specs=[pl.BlockSpec((tm, K), lambda i, j: (i, 0)),
                  pl.BlockSpec((K, tn), lambda i, j: (0, j))],
        out_specs=pl.BlockSpec((tm, tn), lambda i, j: (i, j)),
        scratch_shapes=[pltpu.VMEM((tm, K), bf16)],
        compiler_params=_cparams("parallel", "arbitrary"),
        name="linear",
    )(x, w)


def _gates_kernel(x_ref, w_ref, b_ref, o_ref):
    pre = _dot_hi(x_ref[...], w_ref[...]) + b_ref[...]
    lane = lax.broadcasted_iota(jnp.int32, pre.shape, 1)
    o_ref[...] = jnp.where(lane < NH_A, pre,
                           jnp.where(lane < 2 * NH_A, jax.nn.log_sigmoid(pre), 0.0))


def _gates(x, w_g, b_g, tm_cap=512):
    M, K = x.shape
    tm = _pick_tile(M, tm_cap)
    return pl.pallas_call(
        _gates_kernel,
        out_shape=jax.ShapeDtypeStruct((M, LANES), f32),
        grid=(M // tm,),
        in_specs=[pl.BlockSpec((tm, K), lambda i: (i, 0)),
                  pl.BlockSpec((K, LANES), lambda i: (0, 0)),
                  pl.BlockSpec((1, LANES), lambda i: (0, 0))],
        out_specs=pl.BlockSpec((tm, LANES), lambda i: (i, 0)),
        compiler_params=_cparams("parallel"),
        name="gates",
    )(x, w_g, b_g)


def _proj_ln_kernel(*refs, n_x):
    x_refs = refs[:n_x]
    w_ref, res_ref, g_ref, b_ref, o_ref = refs[n_x:]
    y = None
    k0 = 0
    for x_ref in x_refs:
        kc = x_ref.shape[1]
        part = _dot_bf16(x_ref[...], w_ref[k0:k0 + kc, :])
        y = part if y is None else y + part
        k0 += kc
    t = ALPHA * res_ref[...] + y
    o_ref[...] = _layer_norm_rows(t, g_ref[...], b_ref[...])


def _proj_ln(xs, w, res, g, b, tm_cap=256):
    M, N = res.shape
    tm = _pick_tile(M, tm_cap)
    in_specs = []
    args = []
    for arr, width, blk in xs:
        in_specs.append(pl.BlockSpec((tm, width), functools.partial(lambda i, blk: (i, blk), blk=blk)))
        args.append(arr)
    K = w.shape[0]
    in_specs += [pl.BlockSpec((K, N), lambda i: (0, 0)),
                 pl.BlockSpec((tm, N), lambda i: (i, 0)),
                 pl.BlockSpec((1, N), lambda i: (0, 0)),
                 pl.BlockSpec((1, N), lambda i: (0, 0))]
    return pl.pallas_call(
        functools.partial(_proj_ln_kernel, n_x=len(xs)),
        out_shape=jax.ShapeDtypeStruct((M, N), f32),
        grid=(M // tm,),
        in_specs=in_specs,
        out_specs=pl.BlockSpec((tm, N), lambda i: (i, 0)),
        compiler_params=_cparams("parallel"),
        name="proj_ln",
    )(*args, w, res, g, b)


def _mlp_ln_kernel(x_ref, wu_ref, wd_ref, g_ref, b_ref, o_ref, xb_ref, acc_ref):
    f = pl.program_id(1)

    @pl.when(f == 0)
    def _():
        xb_ref[...] = x_ref[...].astype(bf16)
        acc_ref[...] = jnp.zeros_like(acc_ref)

    a = jnp.dot(xb_ref[...], wu_ref[...], preferred_element_type=f32)
    a = jnp.square(jnp.maximum(a, 0.0))
    acc_ref[...] += jnp.dot(a.astype(bf16), wd_ref[...], preferred_element_type=f32)

    @pl.when(f == pl.num_programs(1) - 1)
    def _():
        t = ALPHA * x_ref[...] + acc_ref[...]
        o_ref[...] = _layer_norm_rows(t, g_ref[...], b_ref[...])


def _mlp_ln(x, w_up, w_down, g, b, tm_cap=512, tf=512):
    M, K = x.shape
    F = w_up.shape[1]
    tm = _pick_tile(M, tm_cap)
    return pl.pallas_call(
        _mlp_ln_kernel,
        out_shape=jax.ShapeDtypeStruct((M, K), f32),
        grid=(M // tm, F // tf),
        in_specs=[pl.BlockSpec((tm, K), lambda i, j: (i, 0)),
                  pl.BlockSpec((K, tf), lambda i, j: (0, j)),
                  pl.BlockSpec((tf, K), lambda i, j: (j, 0)),
                  pl.BlockSpec((1, K), lambda i, j: (0, 0)),
                  pl.BlockSpec((1, K), lambda i, j: (0, 0))],
        out_specs=pl.BlockSpec((tm, K), lambda i, j: (i, 0)),
        scratch_shapes=[pltpu.VMEM((tm, K), bf16), pltpu.VMEM((tm, K), f32)],
        compiler_params=_cparams("parallel", "arbitrary"),
        name="mlp_ln",
    )(x, w_up, w_down, g, b)


def _pad_rows(x, rows):
    if x.shape[0] == rows:
        return x
    return jnp.concatenate([x, jnp.zeros((rows - x.shape[0],) + x.shape[1:], x.dtype)], axis=0)


def _mlstm_kernel(q_ref, k_ref, v_ref, o_ref, g_ref, c0_ref, n0_ref, m0_ref, ng_ref,
                  h_ref, c_ref, n_ref, m_ref, cs, ns, ms, *, t_real):
    L = MLSTM_CHUNK
    lb = q_ref.shape[0]
    head = pl.program_id(1)
    c = pl.program_id(2)

    @pl.when(c == 0)
    def _():
        cs[...] = c0_ref[...]
        ns[...] = n0_ref[...]
        ms[...] = m0_ref[...]

    nvalid = jnp.minimum(t_real - c * lb, lb)
    row1 = lax.broadcasted_iota(jnp.int32, (L, 1), 0)
    col1 = lax.broadcasted_iota(jnp.int32, (1, L), 1)
    vrow = row1 < nvalid
    vcol = col1 < nvalid

    q = jnp.where(vrow, _pad_rows(q_ref[...], L), 0.0)
    k = jnp.where(vrow, _pad_rows(k_ref[...], L), 0.0) * (HD_A ** -0.5)
    v = jnp.where(vrow, _pad_rows(v_ref[...], L), 0.0)
    G = jnp.where(vrow, _pad_rows(g_ref[...], L), 0.0)

    lane = lax.broadcasted_iota(jnp.int32, (1, LANES), 1)
    ig_sel = lane == head
    lf_sel = lane == NH_A + head
    rowL = lax.broadcasted_iota(jnp.int32, (L, L), 0)
    colL = lax.broadcasted_iota(jnp.int32, (L, L), 1)
    causal = rowL >= colL

    fcum = _dot_hi(causal.astype(f32), G)
    fc = jnp.sum(jnp.where(lf_sel, fcum, 0.0), axis=1, keepdims=True)
    igc = jnp.where(vrow, jnp.sum(jnp.where(ig_sel, G, 0.0), axis=1, keepdims=True), NEG_BIG)
    e_lf = jnp.broadcast_to(lf_sel.astype(f32), (SUBLANES, LANES))
    e_ig = jnp.broadcast_to(ig_sel.astype(f32), (SUBLANES, LANES))
    fr = _dot_hi(e_lf, fcum, NT_DIMS)[0:1, :]
    igr = jnp.where(vcol, _dot_hi(e_ig, G, NT_DIMS)[0:1, :], NEG_BIG)

    m_prev = ms[0:1, 0:1]
    logd = jnp.where(causal, fc - fr + igr, NEG_BIG)
    b_in = fc + m_prev
    m_row = jnp.maximum(b_in, jnp.max(logd, axis=1, keepdims=True))
    inter = jnp.exp(b_in - m_row)
    dm = jnp.exp(logd - m_row)

    qb = q.astype(bf16)
    kb = k.astype(bf16)
    vb = v.astype(bf16)
    s = lax.dot_general(qb, kb, NT_DIMS, preferred_element_type=f32) * dm
    cmat = cs[...]
    nrow = ns[...]
    num = jnp.dot(s.astype(bf16), vb, preferred_element_type=f32) \
        + inter * jnp.dot(qb, cmat.astype(bf16), preferred_element_type=f32)
    den = jnp.sum(s, axis=1, keepdims=True) + inter * jnp.sum(q * nrow, axis=1, keepdims=True)
    hh = num / jnp.maximum(jnp.abs(den), jnp.exp(-m_row))

    mu = jnp.mean(hh, axis=1, keepdims=True)
    dv = hh - mu
    var = jnp.mean(dv * dv, axis=1, keepdims=True)
    hn = dv * lax.rsqrt(var + LN_EPS) * ng_ref[...]
    og = jnp.where(vrow, _pad_rows(o_ref[...], L), 0.0)
    out = hn * jax.nn.sigmoid(og)
    h_ref[...] = out[0:lb]

    fl = fc[L - 1:L, :]
    wj_c = fl - fc + igc
    wj_r = fl - fr + igr
    m_new = jnp.maximum(fl + m_prev, jnp.max(wj_r, axis=1, keepdims=True))
    wexp = jnp.exp(wj_c - m_new)
    dec = jnp.exp(fl + m_prev - m_new)
    kw = k * wexp
    c_new = dec * cmat + lax.dot_general(kw.astype(bf16), vb, TN_DIMS, preferred_element_type=f32)
    n_new = dec * nrow + jnp.sum(kw, axis=0, keepdims=True)
    cs[...] = c_new
    ns[...] = n_new
    ms[...] = jnp.broadcast_to(m_new, ms.shape)

    @pl.when(c == pl.num_programs(2) - 1)
    def _():
        c_ref[...] = c_new
        n_ref[...] = n_new
        m_ref[...] = jnp.broadcast_to(m_new, m_ref.shape)


def _mlstm(z, gates, c0, n0, m0, norm_g, t_real):
    B, T, _ = z.shape
    lb = min(MLSTM_CHUNK, T)
    nc = T // lb
    hpg = MIX_A // HD_A
    m0b = jnp.broadcast_to(m0[:, :, None, None], (B, NH_A, 1, LANES))

    def zspec(group):
        return pl.BlockSpec((None, lb, HD_A), lambda b, h, c: (b, c, group * hpg + h))

    st_spec = lambda r, w: pl.BlockSpec((None, None, r, w), lambda b, h, c: (b, h, 0, 0))
    outs = pl.pallas_call(
        functools.partial(_mlstm_kernel, t_real=t_real),
        out_shape=[jax.ShapeDtypeStruct((B, T, MIX_A), f32),
                   jax.ShapeDtypeStruct((B, NH_A, HD_A, HD_A), f32),
                   jax.ShapeDtypeStruct((B, NH_A, 1, HD_A), f32),
                   jax.ShapeDtypeStruct((B, NH_A, 1, LANES), f32)],
        grid=(B, NH_A, nc),
        in_specs=[zspec(0), zspec(1), zspec(2), zspec(3),
                  pl.BlockSpec((None, lb, LANES), lambda b, h, c: (b, c, 0)),
                  st_spec(HD_A, HD_A), st_spec(1, HD_A), st_spec(1, LANES),
                  pl.BlockSpec((None, 1, HD_A), lambda b, h, c: (h, 0, 0))],
        out_specs=[pl.BlockSpec((None, lb, HD_A), lambda b, h, c: (b, c, h)),
                   st_spec(HD_A, HD_A), st_spec(1, HD_A), st_spec(1, LANES)],
        scratch_shapes=[pltpu.VMEM((HD_A, HD_A), f32), pltpu.VMEM((1, HD_A), f32),
                        pltpu.VMEM((1, LANES), f32)],
        compiler_params=_cparams("parallel", "parallel", "arbitrary"),
        name="mlstm",
    )(z, z, z, z, gates, c0, n0.reshape(B, NH_A, 1, HD_A), m0b, norm_g.reshape(NH_A, 1, HD_A))
    hm, c_new, n_new, m_new = outs
    return hm, c_new, n_new.reshape(B, NH_A, HD_A), m_new[:, :, 0, 0]


def _one_minus_exp(x):
    u = jnp.exp(x)
    um1 = u - 1.0
    safe = jnp.where(um1 == 0.0, -x, -(um1 * x) / jnp.log(jnp.where(um1 == 0.0, 0.5, u)))
    return jnp.where(x > -0.5, safe, 1.0 - u)


def _rglru_kernel(xr_ref, gr_ref, conv0_ref, h0_ref, cw_ref, cb_ref, wa_ref, ba_ref, wx_ref,
                  bx_ref, lam_ref, y_ref, hl_ref, xp, a_s, u_s, h_s, hc, *, t_real):
    lr = xr_ref.shape[0]
    c = pl.program_id(1)
    hist = SUBLANES - (CONV_W - 1)

    @pl.when(c == 0)
    def _():
        xp[0:SUBLANES, :] = jnp.zeros((SUBLANES, RG_W), f32)
        xp[hist:SUBLANES, :] = conv0_ref[...]
        hc[...] = h0_ref[...]

    xp[SUBLANES:SUBLANES + lr, :] = xr_ref[...]
    xc = cb_ref[...]
    for j in range(CONV_W):
        xc = xc + xp[hist + j:hist + j + lr, :] * cw_ref[j:j + 1, :]
    xp[hist:SUBLANES, :] = xp[hist + lr:SUBLANES + lr, :]

    xcb = xc.astype(bf16)
    ra = []
    rx = []
    for n in range(RG_BLOCKS):
        xn = xcb[:, n * RG_BW:(n + 1) * RG_BW]
        ra.append(jnp.dot(xn, wa_ref[n].astype(bf16), preferred_element_type=f32))
        rx.append(jnp.dot(xn, wx_ref[n].astype(bf16), preferred_element_type=f32))
    r = jax.nn.sigmoid(jnp.concatenate(ra, axis=1) + ba_ref[...])
    gi = jax.nn.sigmoid(jnp.concatenate(rx, axis=1) + bx_ref[...])
    log_a = -RG_C * r * jax.nn.softplus(-lam_ref[...])
    a = jnp.exp(log_a)
    u = jnp.sqrt(_one_minus_exp(2.0 * log_a)) * (gi * xc)
    valid = (c * lr + lax.broadcasted_iota(jnp.int32, (lr, 1), 0)) < t_real
    a_s[...] = jnp.where(valid, a, 1.0)
    u_s[...] = jnp.where(valid, u, 0.0)

    def step(t, h):
        h = a_s[pl.ds(t, 1), :] * h + u_s[pl.ds(t, 1), :]
        h_s[pl.ds(t, 1), :] = h
        return h

    h_last = lax.fori_loop(0, lr, step, hc[...])
    hc[...] = h_last
    y_ref[...] = h_s[...] * jax.nn.gelu(gr_ref[...])

    @pl.when(c == pl.num_programs(1) - 1)
    def _():
        hl_ref[...] = h_last


def _rglru(z, conv0, h0, conv_w, conv_b, wa, ba, wx, bx, lam, t_real, lr_cap=544):
    B, T, _ = z.shape
    lr = _pick_tile(T, lr_cap)
    row = lambda a: a.reshape(1, RG_W)
    full = lambda shp: pl.BlockSpec(shp, lambda b, c: (0,) * len(shp))
    y, h_last = pl.pallas_call(
        functools.partial(_rglru_kernel, t_real=t_real),
        out_shape=[jax.ShapeDtypeStruct((B, T, RG_W), f32),
                   jax.ShapeDtypeStruct((B, 1, RG_W), f32)],
        grid=(B, T // lr),
        in_specs=[pl.BlockSpec((None, lr, RG_W), lambda b, c: (b, c, 4)),
                  pl.BlockSpec((None, lr, RG_W), lambda b, c: (b, c, 5)),
                  pl.BlockSpec((None, CONV_W - 1, RG_W), lambda b, c: (b, 0, 0)),
                  pl.BlockSpec((None, 1, RG_W), lambda b, c: (b, 0, 0)),
                  full((CONV_W, RG_W)), full((1, RG_W)),
                  full((RG_BLOCKS, RG_BW, RG_BW)), full((1, RG_W)),
                  full((RG_BLOCKS, RG_BW, RG_BW)), full((1, RG_W)), full((1, RG_W))],
        out_specs=[pl.BlockSpec((None, lr, RG_W), lambda b, c: (b, c, 0)),
                   pl.BlockSpec((None, 1, RG_W), lambda b, c: (b, 0, 0))],
        scratch_shapes=[pltpu.VMEM((SUBLANES + lr, RG_W), f32), pltpu.VMEM((lr, RG_W), f32),
                        pltpu.VMEM((lr, RG_W), f32), pltpu.VMEM((lr, RG_W), f32),
                        pltpu.VMEM((1, RG_W), f32)],
        compiler_params=_cparams("parallel", "arbitrary"),
        name="rglru",
    )(z, z, conv0, h0.reshape(B, 1, RG_W), conv_w, row(conv_b), wa, row(ba), wx, row(bx), row(lam))
    return y, h_last.reshape(B, RG_W)


def _rwkv_rkv_kernel(x_ref, xp_ref, mu_ref, w_ref, o_ref, xs_ref):
    @pl.when(pl.program_id(2) == 0)
    def _():
        x = x_ref[...]
        xs_ref[...] = (x + (xp_ref[...] - x) * mu_ref[...]).astype(bf16)

    o_ref[...] = jnp.dot(xs_ref[...], w_ref[...], preferred_element_type=f32)


def _rwkv_rkv(x, xprev, mu3, w3, tm_cap=512, tn=1024):
    M, K = x.shape
    tm = _pick_tile(M, tm_cap)
    return pl.pallas_call(
        _rwkv_rkv_kernel,
        out_shape=jax.ShapeDtypeStruct((3, M, K), f32),
        grid=(M // tm, 3, K // tn),
        in_specs=[pl.BlockSpec((tm, K), lambda i, s, j: (i, 0)),
                  pl.BlockSpec((tm, K), lambda i, s, j: (i, 0)),
                  pl.BlockSpec((None, 1, K), lambda i, s, j: (s, 0, 0)),
                  pl.BlockSpec((None, K, tn), lambda i, s, j: (s, 0, j))],
        out_specs=pl.BlockSpec((None, tm, tn), lambda i, s, j: (s, i, j)),
        scratch_shapes=[pltpu.VMEM((tm, K), bf16)],
        compiler_params=_cparams("parallel", "arbitrary", "arbitrary"),
        name="rwkv_rkv",
    )(x, xprev, mu3, w3)


def _rwkv_lora_kernel(x_ref, xp_ref, mu_ref, w1_ref, w2w_ref, w2a_ref, w2g_ref, w0_ref, a0_ref,
                      lw_ref, a_ref, g_ref):
    x = x_ref[...]
    dx = xp_ref[...] - x
    xw = (x + dx * mu_ref[0]).astype(bf16)
    xa = (x + dx * mu_ref[1]).astype(bf16)
    xg = (x + dx * mu_ref[2]).astype(bf16)
    r = LORA_PAD
    hw = jnp.tanh(jnp.dot(xw, w1_ref[:, 0:r], preferred_element_type=f32))
    ha = jnp.dot(xa, w1_ref[:, r:2 * r], preferred_element_type=f32)
    hg = jax.nn.sigmoid(jnp.dot(xg, w1_ref[:, 2 * r:], preferred_element_type=f32))
    w = w0_ref[...] + jnp.dot(hw.astype(bf16), w2w_ref[...], preferred_element_type=f32)
    w = -jax.nn.softplus(-w) - 0.5
    lw_ref[...] = -jnp.exp(w)
    a = a0_ref[...] + jnp.dot(ha.astype(bf16), w2a_ref[...], preferred_element_type=f32)
    a_ref[...] = jax.nn.sigmoid(a)
    g_ref[...] = jnp.dot(hg.astype(bf16), w2g_ref[...], preferred_element_type=f32)


def _rwkv_lora(x, xprev, mu3, w1, w2w, w2a, w2g, w0, a0, tm_cap=256):
    M, K = x.shape
    tm = _pick_tile(M, tm_cap)
    full = lambda a: pl.BlockSpec(a.shape, lambda i: (0,) * a.ndim)
    row = pl.BlockSpec((tm, K), lambda i: (i, 0))
    return pl.pallas_call(
        _rwkv_lora_kernel,
        out_shape=[jax.ShapeDtypeStruct((M, K), f32)] * 3,
        grid=(M // tm,),
        in_specs=[row, row, full(mu3), full(w1), full(w2w), full(w2a), full(w2g), full(w0), full(a0)],
        out_specs=[row, row, row],
        compiler_params=_cparams("parallel"),
        name="rwkv_lora",
    )(x, xprev, mu3, w1, w2w, w2a, w2g, w0, a0)


def _stack_heads(x, m0, m1):
    return jnp.concatenate([x * m0, x * m1], axis=0)


def _rwkv_scan_kernel(r_ref, k_ref, v_ref, lw_ref, ag_ref, g_ref, s0_ref, kk_ref, ka_ref, rk_ref,
                      lg_ref, lb_ref, y_ref, s_ref, sbd, *, t_real):
    L = RWKV_CHUNK
    L2 = 2 * L
    lc = r_ref.shape[0]
    c = pl.program_id(2)

    @pl.when(c == 0)
    def _():
        sbd[...] = jnp.zeros((L2, L2), f32)
        sbd[0:HD_C, 0:HD_C] = s0_ref[0]
        sbd[HD_C:L2, HD_C:L2] = s0_ref[1]

    row1 = lax.broadcasted_iota(jnp.int32, (L, 1), 0)
    valid = jnp.logical_and(row1 < lc, c * lc + row1 < t_real)
    ld = lambda ref: jnp.where(valid, _pad_rows(ref[...], L), 0.0)
    r = ld(r_ref)
    k = ld(k_ref)
    v = ld(v_ref)
    lw = ld(lw_ref)
    ag = ld(ag_ref)

    lane = lax.broadcasted_iota(jnp.int32, (1, LANES), 1)
    m0 = (lane < HD_C).astype(f32)
    m1 = 1.0 - m0
    row2 = lax.broadcasted_iota(jnp.int32, (L2, L2), 0)
    col2 = lax.broadcasted_iota(jnp.int32, (L2, L2), 1)
    same_head = (row2 < L) == (col2 < L)
    ones_bd = same_head.astype(f32)
    rin = jnp.where(row2 < L, row2, row2 - L)
    cin = jnp.where(col2 < L, col2, col2 - L)
    strict = jnp.logical_and(same_head, rin > cin)
    incl = jnp.logical_and(same_head, rin >= cin)
    eye = (row2 == col2).astype(f32)

    kk = k * kk_ref[...]
    ss = _dot_hi(kk * kk, ones_bd)
    kk = kk / jnp.maximum(jnp.sqrt(ss), 1e-12)
    kp = k * (1.0 + (ag - 1.0) * ka_ref[...])
    av = -kk
    bv = kk * ag

    rowL = lax.broadcasted_iota(jnp.int32, (L, L), 0)
    colL = lax.broadcasted_iota(jnp.int32, (L, L), 1)
    cw = _dot_hi((rowL >= colL).astype(f32), lw)
    cwl = cw[L - 1:L, :]
    p_in = jnp.exp(cw)
    p_ex = jnp.exp(cw - lw)
    p_inv = jnp.exp(-cw)
    p_tail = jnp.exp(cwl - cw)
    a_st = _stack_heads(av * p_ex, m0, m1)
    r_st = _stack_heads(r * p_in, m0, m1)
    k_st = _stack_heads(kp * p_inv, m0, m1)
    b_st = _stack_heads(bv * p_inv, m0, m1)
    kh_st = _stack_heads(kp * p_tail, m0, m1)
    bh_st = _stack_heads(bv * p_tail, m0, m1)
    v_st = _stack_heads(v, m0, m1)

    m_ab = jnp.where(strict, _dot_hi(a_st, b_st, NT_DIMS), 0.0)
    m_ak = jnp.where(strict, _dot_hi(a_st, k_st, NT_DIMS), 0.0)
    a_rk = jnp.where(incl, _dot_hi(r_st, k_st, NT_DIMS), 0.0)
    a_rb = jnp.where(incl, _dot_hi(r_st, b_st, NT_DIMS), 0.0)

    x = m_ab
    tinv = eye + x
    n_sq = (L - 1).bit_length() - 1
    for _ in range(n_sq):
        x = _dot_hi(x, x)
        tinv = tinv + _dot_hi(tinv, x)

    s_prev = sbd[...]
    rhs = _dot_hi(a_st, s_prev, NT_DIMS) + _dot_hi(m_ak, v_st)
    u_st = _dot_hi(tinv, rhs)
    o_st = _dot_hi(r_st, s_prev, NT_DIMS) + _dot_hi(a_rk, v_st) + _dot_hi(a_rb, u_st)
    s_new = s_prev * jnp.exp(cwl) + _dot_hi(u_st, bh_st, TN_DIMS) + _dot_hi(v_st, kh_st, TN_DIMS)
    s_new = jnp.where(same_head, s_new, 0.0)
    sbd[...] = s_new

    y = o_st[0:L] + o_st[L:L2]
    inv_n = 1.0 / HD_C
    mu = _dot_hi(y, ones_bd) * inv_n
    dy = y - mu
    var = _dot_hi(dy * dy, ones_bd) * inv_n
    yn = dy * lax.rsqrt(var + GN_EPS) * lg_ref[...] + lb_ref[...]
    bonus = _dot_hi(r * kp * rk_ref[...], ones_bd) * v
    g = _pad_rows(g_ref[...], L)
    y_ref[...] = ((yn + bonus) * g)[0:lc]

    @pl.when(c == pl.num_programs(2) - 1)
    def _():
        s_ref[0] = s_new[0:HD_C, 0:HD_C]
        s_ref[1] = s_new[HD_C:L2, HD_C:L2]


def _rwkv_scan(rkv, lw, ag, g, s0, kk_p, ka_p, rk_p, lnx_g, lnx_b, t_real):
    _, B, T, D = rkv.shape
    lc = min(RWKV_CHUNK, T)
    nc = T // lc
    npair = D // LANES
    tok = pl.BlockSpec((None, lc, LANES), lambda b, p, c: (b, c, p))
    rkv_spec = lambda s: pl.BlockSpec((None, None, lc, LANES), lambda b, p, c: (s, b, c, p))
    prm = pl.BlockSpec((1, LANES), lambda b, p, c: (0, p))
    st = pl.BlockSpec((None, 2, HD_C, HD_C), lambda b, p, c: (b, p, 0, 0))
    row = lambda a: a.reshape(1, D)
    return pl.pallas_call(
        functools.partial(_rwkv_scan_kernel, t_real=t_real),
        out_shape=[jax.ShapeDtypeStruct((B, T, D), f32),
                   jax.ShapeDtypeStruct((B, NH_C, HD_C, HD_C), f32)],
        grid=(B, npair, nc),
        in_specs=[rkv_spec(0), rkv_spec(1), rkv_spec(2), tok, tok, tok, st, prm, prm, prm, prm, prm],
        out_specs=[tok, st],
        scratch_shapes=[pltpu.VMEM((2 * RWKV_CHUNK, 2 * RWKV_CHUNK), f32)],
        compiler_params=_cparams("parallel", "parallel", "arbitrary"),
        name="rwkv_scan",
    )(rkv, rkv, rkv, lw, ag, g, s0, row(kk_p), row(ka_p), row(rk_p), row(lnx_g), row(lnx_b))


def _pad_lora(w1, w2):
    rank = w1.shape[1]
    return (jnp.pad(w1, ((0, 0), (0, LORA_PAD - rank))), jnp.pad(w2, ((0, LORA_PAD - rank), (0, 0))))


def _prepare_params(p):
    w_in = p['w_in_ab'][0]
    n_qkvo = 4 * MIX_A
    n_g = 2 * NH_A
    q = {}
    q['w_in'] = jnp.concatenate([w_in[:, :n_qkvo], w_in[:, n_qkvo + n_g:]], axis=1).astype(bf16)
    q['w_gate'] = jnp.pad(w_in[:, n_qkvo:n_qkvo + n_g], ((0, 0), (0, LANES - n_g)))
    q['b_gate'] = jnp.pad(p['b_if_ab'][0], (0, LANES - n_g)).reshape(1, LANES)
    q['w_out_ab'] = p['w_out_ab'][0].astype(bf16)
    q['w_up'] = p['w_up'].astype(bf16)
    q['w_down'] = p['w_down'].astype(bf16)
    q['w_rkv'] = jnp.stack([p['rw_wr'][0], p['rw_wk'][0], p['rw_wv'][0]]).astype(bf16)
    mu = p['rw_mu'][0]
    q['mu_rkv'] = jnp.stack([mu[0], mu[2], mu[3]])[:, None, :]
    q['mu_lora'] = jnp.stack([mu[1], mu[4], mu[5]])[:, None, :]
    w1, w2w = _pad_lora(p['rw_w1'][0], p['rw_w2'][0])
    a1, w2a = _pad_lora(p['rw_a1'][0], p['rw_a2'][0])
    q['lora_w1'] = jnp.concatenate([w1, a1, p['rw_g1'][0]], axis=1).astype(bf16)
    q['lora_w2w'] = w2w.astype(bf16)
    q['lora_w2a'] = w2a.astype(bf16)
    q['lora_w2g'] = p['rw_g2'][0].astype(bf16)
    q['w_o'] = p['rw_wo'][0].astype(bf16)
    return q


def _trunk(h, states, p, q, t_real):
    mC, mn, mm, rgh, rgc, rwS, rwx = states
    B, T, D = h.shape
    M = B * T
    row = lambda a: a.reshape(1, -1)
    hf = h.reshape(M, D)

    z = _linear(hf, q['w_in'], q['w_in'].shape[1])
    gates = _gates(hf, q['w_gate'], q['b_gate'])
    z3 = z.reshape(B, T, -1)
    hm, c_new, n_new, m_new = _mlstm(z3, gates.reshape(B, T, LANES), mC[0], mn[0], mm[0],
                                     p['mlstm_norm_g'][0], t_real)
    yr, h_last = _rglru(z3, rgc[0], rgh[0], p['rg_conv_w'][0], p['rg_conv_b'][0], p['rg_wa'][0],
                        p['rg_ba'][0], p['rg_wx'][0], p['rg_bx'][0], p['rg_lambda'][0], t_real)
    assert t_real >= CONV_W - 1
    conv_new = z3[:, t_real - (CONV_W - 1):t_real, 4 * MIX_A:4 * MIX_A + RG_W]
    h1 = _proj_ln([(hm.reshape(M, MIX_A), MIX_A, 0), (yr.reshape(M, RG_W), RG_W, 0)],
                  q['w_out_ab'], hf, row(p['ln1_g'][0]), row(p['ln1_b'][0]))
    h2 = _mlp_ln(h1, q['w_up'][0], q['w_down'][0], row(p['ln2_g'][0]), row(p['ln2_b'][0]))

    h2_3 = h2.reshape(B, T, D)
    xprev = jnp.concatenate([rwx[0].astype(f32)[:, None], h2_3[:, :-1]], axis=1).reshape(M, D)
    rkv = _rwkv_rkv(h2, xprev, q['mu_rkv'], q['w_rkv'])
    lw, ag, g = _rwkv_lora(h2, xprev, q['mu_lora'], q['lora_w1'], q['lora_w2w'], q['lora_w2a'],
                           q['lora_w2g'], row(p['rw_w0'][0]), row(p['rw_a0'][0]))
    r3 = lambda a: a.reshape(B, T, D)
    xo, s_new = _rwkv_scan(rkv.reshape(3, B, T, D), r3(lw), r3(ag), r3(g), rwS[0], p['rw_kk'][0],
                           p['rw_ka'][0], p['rw_rk'][0], p['rw_lnx_g'][0], p['rw_lnx_b'][0], t_real)
    xo = xo.reshape(M, D)
    h3 = _proj_ln([(xo, D, 0)], q['w_o'], h2, row(p['ln1_g'][1]), row(p['ln1_b'][1]))
    h4 = _mlp_ln(h3, q['w_up'][1], q['w_down'][1], row(p['ln2_g'][1]), row(p['ln2_b'][1]))
    shift_new = h2_3[:, t_real - 1]

    new_states = (c_new[None], n_new[None], m_new[None], h_last[None], conv_new[None],
                  s_new[None], shift_new[None])
    return h4.reshape(B, T, D), new_states


def kernel(x_prompt, x_sample, state_mlstm_C, state_mlstm_n, state_mlstm_m, state_rglru_h, state_rglru_conv, state_rwkv_S, state_rwkv_shift, meta_tokens, w_in_ab, b_if_ab, mlstm_norm_g, rg_conv_w, rg_conv_b, rg_wa, rg_ba, rg_wx, rg_bx, rg_lambda, w_out_ab, rw_mu, rw_wr, rw_wk, rw_wv, rw_wo, rw_w0, rw_w1, rw_w2, rw_a0, rw_a1, rw_a2, rw_g1, rw_g2, rw_kk, rw_ka, rw_rk, rw_lnx_g, rw_lnx_b, ln1_g, ln1_b, ln2_g, ln2_b, w_up, w_down):
    p = dict(w_in_ab=w_in_ab, b_if_ab=b_if_ab, mlstm_norm_g=mlstm_norm_g, rg_conv_w=rg_conv_w,
             rg_conv_b=rg_conv_b, rg_wa=rg_wa, rg_ba=rg_ba, rg_wx=rg_wx, rg_bx=rg_bx,
             rg_lambda=rg_lambda, w_out_ab=w_out_ab, rw_mu=rw_mu, rw_wr=rw_wr, rw_wk=rw_wk,
             rw_wv=rw_wv, rw_wo=rw_wo, rw_w0=rw_w0, rw_w1=rw_w1, rw_w2=rw_w2, rw_a0=rw_a0,
             rw_a1=rw_a1, rw_a2=rw_a2, rw_g1=rw_g1, rw_g2=rw_g2, rw_kk=rw_kk, rw_ka=rw_ka,
             rw_rk=rw_rk, rw_lnx_g=rw_lnx_g, rw_lnx_b=rw_lnx_b, ln1_g=ln1_g, ln1_b=ln1_b,
             ln2_g=ln2_g, ln2_b=ln2_b, w_up=w_up, w_down=w_down)
    q = _prepare_params(p)
    B, S, D = x_prompt.shape
    dt = x_prompt.dtype
    t_prompt = N_META + S
    t_pad = -(-t_prompt // MLSTM_CHUNK) * MLSTM_CHUNK
    zero_states = (
        jnp.zeros((1, B, NH_A, HD_A, HD_A), f32),
        jnp.zeros((1, B, NH_A, HD_A), f32),
        jnp.zeros((1, B, NH_A), f32),
        jnp.zeros((1, B, RG_W), f32),
        jnp.zeros((1, B, CONV_W - 1, RG_W), dt),
        jnp.zeros((1, B, NH_C, HD_C, HD_C), f32),
        jnp.zeros((1, B, D), dt),
    )
    meta = jnp.broadcast_to(meta_tokens.astype(dt)[None], (B, N_META, D))
    tail = jnp.zeros((B, t_pad - t_prompt, D), dt)
    hp, ps = _trunk(jnp.concatenate([meta, x_prompt, tail], axis=1), zero_states, p, q, t_prompt)
    sample_states = (state_mlstm_C, state_mlstm_n, state_mlstm_m, state_rglru_h, state_rglru_conv,
                     state_rwkv_S, state_rwkv_shift)
    hs, ss = _trunk(x_sample, sample_states, p, q, x_sample.shape[1])
    return (hp[:, N_META:t_prompt], hs, *ps, *ss)
```

```python
import functools

import jax
import jax.numpy as jnp
from jax import lax
from jax.experimental import pallas as pl
from jax.experimental.pallas import tpu as pltpu

f32 = jnp.float32
bf16 = jnp.bfloat16
HI = lax.Precision.HIGHEST

D_MODEL = 2048
N_META = 16
MIX_A = 1024
NH_A = 4
HD_A = 256
RG_W = 1024
RG_BLOCKS = 8
RG_BW = 128
CONV_W = 4
RG_C = 8.0
HD_C = 64
NH_C = 32
D_FF = 8192
LN_EPS = 1e-5
GN_EPS = 64e-5
DEPTH = 2
ALPHA = (2.0 * DEPTH) ** 0.25

LANES = 128
SUBLANES = 8
VMEM_LIMIT_BYTES = 56 * 1024 * 1024

MLSTM_CHUNK = 128
RWKV_TILE = 64
LORA_PAD = 128
NEG_BIG = -1e30

NT_DIMS = (((1,), (1,)), ((), ()))
TN_DIMS = (((0,), (0,)), ((), ()))


def _cparams(*sem):
    return pltpu.CompilerParams(dimension_semantics=sem, vmem_limit_bytes=VMEM_LIMIT_BYTES)


def _pick_tile(n, cap):
    best = None
    for d in range(SUBLANES, min(n, cap) + 1, SUBLANES):
        if n % d == 0:
            best = d
    assert best is not None, (n, cap)
    return best


def _dot_bf16(a, b):
    return jnp.dot(a.astype(bf16), b.astype(bf16), preferred_element_type=f32)


def _dot_hi(a, b, dims=None):
    if dims is None:
        return jnp.dot(a, b, precision=HI, preferred_element_type=f32)
    return lax.dot_general(a, b, dims, precision=HI, preferred_element_type=f32)


def _layer_norm_rows(t, g, b):
    mu = jnp.mean(t, axis=-1, keepdims=True)
    d = t - mu
    var = jnp.mean(d * d, axis=-1, keepdims=True)
    return d * lax.rsqrt(var + LN_EPS) * g + b


def _linear_kernel(x_ref, w_ref, o_ref, xb_ref):
    @pl.when(pl.program_id(1) == 0)
    def _():
        xb_ref[...] = x_ref[...].astype(bf16)

    o_ref[...] = jnp.dot(xb_ref[...], w_ref[...], preferred_element_type=f32)


def _linear(x, w, n_out, tm_cap=512, tn=1024):
    M, K = x.shape
    tm = _pick_tile(M, tm_cap)
    return pl.pallas_call(
        _linear_kernel,
        out_shape=jax.ShapeDtypeStruct((M, n_out), f32),
        grid=(M // tm, n_out // tn),
        in_specs=[pl.BlockSpec((tm, K), lambda i, j: (i, 0)),
                  pl.BlockSpec((K, tn), lambda i, j: (0, j))],
        out_specs=pl.BlockSpec((tm, tn), lambda i, j: (i, j)),
        scratch_shapes=[pltpu.VMEM((tm, K), bf16)],
        compiler_params=_cparams("parallel", "arbitrary"),
        name="linear",
    )(x, w)


def _gates_kernel(x_ref, w_ref, b_ref, o_ref):
    pre = _dot_hi(x_ref[...], w_ref[...]) + b_ref[...]
    lane = lax.broadcasted_iota(jnp.int32, pre.shape, 1)
    o_ref[...] = jnp.where(lane < NH_A, pre,
                           jnp.where(lane < 2 * NH_A, jax.nn.log_sigmoid(pre), 0.0))


def _gates(x, w_g, b_g, tm_cap=512):
    M, K = x.shape
    tm = _pick_tile(M, tm_cap)
    return pl.pallas_call(
        _gates_kernel,
        out_shape=jax.ShapeDtypeStruct((M, LANES), f32),
        grid=(M // tm,),
        in_specs=[pl.BlockSpec((tm, K), lambda i: (i, 0)),
                  pl.BlockSpec((K, LANES), lambda i: (0, 0)),
                  pl.BlockSpec((1, LANES), lambda i: (0, 0))],
        out_specs=pl.BlockSpec((tm, LANES), lambda i: (i, 0)),
        compiler_params=_cparams("parallel"),
        name="gates",
    )(x, w_g, b_g)


def _proj_ln_kernel(*refs, n_x):
    x_refs = refs[:n_x]
    w_ref, res_ref, g_ref, b_ref, o_ref = refs[n_x:]
    y = None
    k0 = 0
    for x_ref in x_refs:
        kc = x_ref.shape[1]
        part = _dot_bf16(x_ref[...], w_ref[k0:k0 + kc, :])
        y = part if y is None else y + part
        k0 += kc
    t = ALPHA * res_ref[...] + y
    o_ref[...] = _layer_norm_rows(t, g_ref[...], b_ref[...])


def _proj_ln(xs, w, res, g, b, tm_cap=256):
    M, N = res.shape
    tm = _pick_tile(M, tm_cap)
    in_specs = []
    args = []
    for arr, width, blk in xs:
        in_specs.append(pl.BlockSpec((tm, width), functools.partial(lambda i, blk: (i, blk), blk=blk)))
        args.append(arr)
    K = w.shape[0]
    in_specs += [pl.BlockSpec((K, N), lambda i: (0, 0)),
                 pl.BlockSpec((tm, N), lambda i: (i, 0)),
                 pl.BlockSpec((1, N), lambda i: (0, 0)),
                 pl.BlockSpec((1, N), lambda i: (0, 0))]
    return pl.pallas_call(
        functools.partial(_proj_ln_kernel, n_x=len(xs)),
        out_shape=jax.ShapeDtypeStruct((M, N), f32),
        grid=(M // tm,),
        in_specs=in_specs,
        out_specs=pl.BlockSpec((tm, N), lambda i: (i, 0)),
        compiler_params=_cparams("parallel"),
        name="proj_ln",
    )(*args, w, res, g, b)


def _mlp_ln_kernel(x_ref, wu_ref, wd_ref, g_ref, b_ref, o_ref, xb_ref, acc_ref):
    f = pl.program_id(1)

    @pl.when(f == 0)
    def _():
        xb_ref[...] = x_ref[...].astype(bf16)
        acc_ref[...] = jnp.zeros_like(acc_ref)

    a = jnp.dot(xb_ref[...], wu_ref[...], preferred_element_type=f32)
    a = jnp.square(jnp.maximum(a, 0.0))
    acc_ref[...] += jnp.dot(a.astype(bf16), wd_ref[...], preferred_element_type=f32)

    @pl.when(f == pl.num_programs(1) - 1)
    def _():
        t = ALPHA * x_ref[...] + acc_ref[...]
        o_ref[...] = _layer_norm_rows(t, g_ref[...], b_ref[...])


def _mlp_ln(x, w_up, w_down, g, b, tm_cap=512, tf=512):
    M, K = x.shape
    F = w_up.shape[1]
    tm = _pick_tile(M, tm_cap)
    return pl.pallas_call(
        _mlp_ln_kernel,
        out_shape=jax.ShapeDtypeStruct((M, K), f32),
        grid=(M // tm, F // tf),
        in_specs=[pl.BlockSpec((tm, K), lambda i, j: (i, 0)),
                  pl.BlockSpec((K, tf), lambda i, j: (0, j)),
                  pl.BlockSpec((tf, K), lambda i, j: (j, 0)),
                  pl.BlockSpec((1, K), lambda i, j: (0, 0)),
                  pl.BlockSpec((1, K), lambda i, j: (0, 0))],
        out_specs=pl.BlockSpec((tm, K), lambda i, j: (i, 0)),
        scratch_shapes=[pltpu.VMEM((tm, K), bf16), pltpu.VMEM((tm, K), f32)],
        compiler_params=_cparams("parallel", "arbitrary"),
        name="mlp_ln",
    )(x, w_up, w_down, g, b)


def _pad_rows(x, rows):
    if x.shape[0] == rows:
        return x
    return jnp.concatenate([x, jnp.zeros((rows - x.shape[0],) + x.shape[1:], x.dtype)], axis=0)


def _mlstm_kernel(q_ref, k_ref, v_ref, o_ref, g_ref, c0_ref, n0_ref, m0_ref, ng_ref,
                  h_ref, c_ref, n_ref, m_ref, cs, ns, ms, *, t_real):
    L = MLSTM_CHUNK
    lb = q_ref.shape[0]
    head = pl.program_id(1)
    c = pl.program_id(2)

    @pl.when(c == 0)
    def _():
        cs[...] = c0_ref[...]
        ns[...] = n0_ref[...]
        ms[...] = m0_ref[...]

    nvalid = jnp.minimum(t_real - c * lb, lb)
    row1 = lax.broadcasted_iota(jnp.int32, (L, 1), 0)
    col1 = lax.broadcasted_iota(jnp.int32, (1, L), 1)
    vrow = row1 < nvalid
    vcol = col1 < nvalid

    q = jnp.where(vrow, _pad_rows(q_ref[...], L), 0.0)
    k = jnp.where(vrow, _pad_rows(k_ref[...], L), 0.0) * (HD_A ** -0.5)
    v = jnp.where(vrow, _pad_rows(v_ref[...], L), 0.0)
    G = jnp.where(vrow, _pad_rows(g_ref[...], L), 0.0)

    lane = lax.broadcasted_iota(jnp.int32, (1, LANES), 1)
    ig_sel = lane == head
    lf_sel = lane == NH_A + head
    rowL = lax.broadcasted_iota(jnp.int32, (L, L), 0)
    colL = lax.broadcasted_iota(jnp.int32, (L, L), 1)
    causal = rowL >= colL

    fcum = _dot_hi(causal.astype(f32), G)
    fc = jnp.sum(jnp.where(lf_sel, fcum, 0.0), axis=1, keepdims=True)
    igc = jnp.where(vrow, jnp.sum(jnp.where(ig_sel, G, 0.0), axis=1, keepdims=True), NEG_BIG)
    e_lf = jnp.broadcast_to(lf_sel.astype(f32), (SUBLANES, LANES))
    e_ig = jnp.broadcast_to(ig_sel.astype(f32), (SUBLANES, LANES))
    fr = _dot_hi(e_lf, fcum, NT_DIMS)[0:1, :]
    igr = jnp.where(vcol, _dot_hi(e_ig, G, NT_DIMS)[0:1, :], NEG_BIG)

    m_prev = ms[0:1, 0:1]
    logd = jnp.where(causal, fc - fr + igr, NEG_BIG)
    b_in = fc + m_prev
    m_row = jnp.maximum(b_in, jnp.max(logd, axis=1, keepdims=True))
    inter = jnp.exp(b_in - m_row)
    dm = jnp.exp(logd - m_row)

    qb = q.astype(bf16)
    kb = k.astype(bf16)
    vb = v.astype(bf16)
    s = lax.dot_general(qb, kb, NT_DIMS, preferred_element_type=f32) * dm
    cmat = cs[...]
    nrow = ns[...]
    num = jnp.dot(s.astype(bf16), vb, preferred_element_type=f32) \
        + inter * jnp.dot(qb, cmat.astype(bf16), preferred_element_type=f32)
    den = jnp.sum(s, axis=1, keepdims=True) + inter * jnp.sum(q * nrow, axis=1, keepdims=True)
    hh = num / jnp.maximum(jnp.abs(den), jnp.exp(-m_row))

    mu = jnp.mean(hh, axis=1, keepdims=True)
    dv = hh - mu
    var = jnp.mean(dv * dv, axis=1, keepdims=True)
    hn = dv * lax.rsqrt(var + LN_EPS) * ng_ref[...]
    og = jnp.where(vrow, _pad_rows(o_ref[...], L), 0.0)
    out = hn * jax.nn.sigmoid(og)
    h_ref[...] = out[0:lb]

    fl = fc[L - 1:L, :]
    wj_c = fl - fc + igc
    wj_r = fl - fr + igr
    m_new = jnp.maximum(fl + m_prev, jnp.max(wj_r, axis=1, keepdims=True))
    wexp = jnp.exp(wj_c - m_new)
    dec = jnp.exp(fl + m_prev - m_new)
    kw = k * wexp
    c_new = dec * cmat + lax.dot_general(kw.astype(bf16), vb, TN_DIMS, preferred_element_type=f32)
    n_new = dec * nrow + jnp.sum(kw, axis=0, keepdims=True)
    cs[...] = c_new
    ns[...] = n_new
    ms[...] = jnp.broadcast_to(m_new, ms.shape)

    @pl.when(c == pl.num_programs(2) - 1)
    def _():
        c_ref[...] = c_new
        n_ref[...] = n_new
        m_ref[...] = jnp.broadcast_to(m_new, m_ref.shape)


def _mlstm(z, gates, c0, n0, m0, norm_g, t_real):
    B, T, _ = z.shape
    lb = min(MLSTM_CHUNK, T)
    nc = T // lb
    hpg = MIX_A // HD_A
    m0b = jnp.broadcast_to(m0[:, :, None, None], (B, NH_A, 1, LANES))

    def zspec(group):
        return pl.BlockSpec((None, lb, HD_A), lambda b, h, c: (b, c, group * hpg + h))

    st_spec = lambda r, w: pl.BlockSpec((None, None, r, w), lambda b, h, c: (b, h, 0, 0))
    outs = pl.pallas_call(
        functools.partial(_mlstm_kernel, t_real=t_real),
        out_shape=[jax.ShapeDtypeStruct((B, T, MIX_A), f32),
                   jax.ShapeDtypeStruct((B, NH_A, HD_A, HD_A), f32),
                   jax.ShapeDtypeStruct((B, NH_A, 1, HD_A), f32),
                   jax.ShapeDtypeStruct((B, NH_A, 1, LANES), f32)],
        grid=(B, NH_A, nc),
        in_specs=[zspec(0), zspec(1), zspec(2), zspec(3),
                  pl.BlockSpec((None, lb, LANES), lambda b, h, c: (b, c, 0)),
                  st_spec(HD_A, HD_A), st_spec(1, HD_A), st_spec(1, LANES),
                  pl.BlockSpec((None, 1, HD_A), lambda b, h, c: (h, 0, 0))],
        out_specs=[pl.BlockSpec((None, lb, HD_A), lambda b, h, c: (b, c, h)),
                   st_spec(HD_A, HD_A), st_spec(1, HD_A), st_spec(1, LANES)],
        scratch_shapes=[pltpu.VMEM((HD_A, HD_A), f32), pltpu.VMEM((1, HD_A), f32),
                        pltpu.VMEM((1, LANES), f32)],
        compiler_params=_cparams("parallel", "parallel", "arbitrary"),
        name="mlstm",
    )(z, z, z, z, gates, c0, n0.reshape(B, NH_A, 1, HD_A), m0b, norm_g.reshape(NH_A, 1, HD_A))
    hm, c_new, n_new, m_new = outs
    return hm, c_new, n_new.reshape(B, NH_A, HD_A), m_new[:, :, 0, 0]


def _one_minus_exp(x):
    u = jnp.exp(x)
    um1 = u - 1.0
    safe = jnp.where(um1 == 0.0, -x, -(um1 * x) / jnp.log(jnp.where(um1 == 0.0, 0.5, u)))
    return jnp.where(x > -0.5, safe, 1.0 - u)


def _rglru_kernel(xr_ref, gr_ref, conv0_ref, h0_ref, cw_ref, cb_ref, wa_ref, ba_ref, wx_ref,
                  bx_ref, lam_ref, y_ref, hl_ref, xp, a_s, u_s, h_s, hc, *, t_real):
    lr = xr_ref.shape[0]
    c = pl.program_id(1)
    hist = SUBLANES - (CONV_W - 1)

    @pl.when(c == 0)
    def _():
        xp[0:SUBLANES, :] = jnp.zeros((SUBLANES, RG_W), f32)
        xp[hist:SUBLANES, :] = conv0_ref[...]
        hc[...] = h0_ref[...]

    xp[SUBLANES:SUBLANES + lr, :] = xr_ref[...]
    xc = cb_ref[...]
    for j in range(CONV_W):
        xc = xc + xp[hist + j:hist + j + lr, :] * cw_ref[j:j + 1, :]
    xp[hist:SUBLANES, :] = xp[hist + lr:SUBLANES + lr, :]

    xcb = xc.astype(bf16)
    ra = []
    rx = []
    for n in range(RG_BLOCKS):
        xn = xcb[:, n * RG_BW:(n + 1) * RG_BW]
        ra.append(jnp.dot(xn, wa_ref[n].astype(bf16), preferred_element_type=f32))
        rx.append(jnp.dot(xn, wx_ref[n].astype(bf16), preferred_element_type=f32))
    r = jax.nn.sigmoid(jnp.concatenate(ra, axis=1) + ba_ref[...])
    gi = jax.nn.sigmoid(jnp.concatenate(rx, axis=1) + bx_ref[...])
    log_a = -RG_C * r * jax.nn.softplus(-lam_ref[...])
    a = jnp.exp(log_a)
    u = jnp.sqrt(_one_minus_exp(2.0 * log_a)) * (gi * xc)
    valid = (c * lr + lax.broadcasted_iota(jnp.int32, (lr, 1), 0)) < t_real
    a_s[...] = jnp.where(valid, a, 1.0)
    u_s[...] = jnp.where(valid, u, 0.0)

    def step(t, h):
        h = a_s[pl.ds(t, 1), :] * h + u_s[pl.ds(t, 1), :]
        h_s[pl.ds(t, 1), :] = h
        return h

    h_last = lax.fori_loop(0, lr, step, hc[...])
    hc[...] = h_last
    y_ref[...] = h_s[...] * jax.nn.gelu(gr_ref[...])

    @pl.when(c == pl.num_programs(1) - 1)
    def _():
        hl_ref[...] = h_last


def _rglru(z, conv0, h0, conv_w, conv_b, wa, ba, wx, bx, lam, t_real, lr_cap=544):
    B, T, _ = z.shape
    lr = _pick_tile(T, lr_cap)
    row = lambda a: a.reshape(1, RG_W)
    full = lambda shp: pl.BlockSpec(shp, lambda b, c: (0,) * len(shp))
    y, h_last = pl.pallas_call(
        functools.partial(_rglru_kernel, t_real=t_real),
        out_shape=[jax.ShapeDtypeStruct((B, T, RG_W), f32),
                   jax.ShapeDtypeStruct((B, 1, RG_W), f32)],
        grid=(B, T // lr),
        in_specs=[pl.BlockSpec((None, lr, RG_W), lambda b, c: (b, c, 4)),
                  pl.BlockSpec((None, lr, RG_W), lambda b, c: (b, c, 5)),
                  pl.BlockSpec((None, CONV_W - 1, RG_W), lambda b, c: (b, 0, 0)),
                  pl.BlockSpec((None, 1, RG_W), lambda b, c: (b, 0, 0)),
                  full((CONV_W, RG_W)), full((1, RG_W)),
                  full((RG_BLOCKS, RG_BW, RG_BW)), full((1, RG_W)),
                  full((RG_BLOCKS, RG_BW, RG_BW)), full((1, RG_W)), full((1, RG_W))],
        out_specs=[pl.BlockSpec((None, lr, RG_W), lambda b, c: (b, c, 0)),
                   pl.BlockSpec((None, 1, RG_W), lambda b, c: (b, 0, 0))],
        scratch_shapes=[pltpu.VMEM((SUBLANES + lr, RG_W), f32), pltpu.VMEM((lr, RG_W), f32),
                        pltpu.VMEM((lr, RG_W), f32), pltpu.VMEM((lr, RG_W), f32),
                        pltpu.VMEM((1, RG_W), f32)],
        compiler_params=_cparams("parallel", "arbitrary"),
        name="rglru",
    )(z, z, conv0, h0.reshape(B, 1, RG_W), conv_w, row(conv_b), wa, row(ba), wx, row(bx), row(lam))
    return y, h_last.reshape(B, RG_W)


def _rwkv_rkv_kernel(x_ref, xp_ref, mu_ref, w_ref, o_ref, xs_ref):
    @pl.when(pl.program_id(2) == 0)
    def _():
        x = x_ref[...]
        xs_ref[...] = (x + (xp_ref[...] - x) * mu_ref[...]).astype(bf16)

    o_ref[...] = jnp.dot(xs_ref[...], w_ref[...], preferred_element_type=f32)


def _rwkv_rkv(x, xprev, mu3, w3, tm_cap=512, tn=1024):
    M, K = x.shape
    tm = _pick_tile(M, tm_cap)
    return pl.pallas_call(
        _rwkv_rkv_kernel,
        out_shape=jax.ShapeDtypeStruct((3, M, K), f32),
        grid=(M // tm, 3, K // tn),
        in_specs=[pl.BlockSpec((tm, K), lambda i, s, j: (i, 0)),
                  pl.BlockSpec((tm, K), lambda i, s, j: (i, 0)),
                  pl.BlockSpec((None, 1, K), lambda i, s, j: (s, 0, 0)),
                  pl.BlockSpec((None, K, tn), lambda i, s, j: (s, 0, j))],
        out_specs=pl.BlockSpec((None, tm, tn), lambda i, s, j: (s, i, j)),
        scratch_shapes=[pltpu.VMEM((tm, K), bf16)],
        compiler_params=_cparams("parallel", "arbitrary", "arbitrary"),
        name="rwkv_rkv",
    )(x, xprev, mu3, w3)


def _rwkv_lora_kernel(x_ref, xp_ref, mu_ref, w1_ref, w2w_ref, w2a_ref, w2g_ref, w0_ref, a0_ref,
                      lw_ref, a_ref, g_ref):
    x = x_ref[...]
    dx = xp_ref[...] - x
    xw = (x + dx * mu_ref[0]).astype(bf16)
    xa = (x + dx * mu_ref[1]).astype(bf16)
    xg = (x + dx * mu_ref[2]).astype(bf16)
    r = LORA_PAD
    hw = jnp.tanh(jnp.dot(xw, w1_ref[:, 0:r], preferred_element_type=f32))
    ha = jnp.dot(xa, w1_ref[:, r:2 * r], preferred_element_type=f32)
    hg = jax.nn.sigmoid(jnp.dot(xg, w1_ref[:, 2 * r:], preferred_element_type=f32))
    w = w0_ref[...] + jnp.dot(hw.astype(bf16), w2w_ref[...], preferred_element_type=f32)
    w = -jax.nn.softplus(-w) - 0.5
    lw_ref[...] = -jnp.exp(w)
    a = a0_ref[...] + jnp.dot(ha.astype(bf16), w2a_ref[...], preferred_element_type=f32)
    a_ref[...] = jax.nn.sigmoid(a)
    g_ref[...] = jnp.dot(hg.astype(bf16), w2g_ref[...], preferred_element_type=f32)


def _rwkv_lora(x, xprev, mu3, w1, w2w, w2a, w2g, w0, a0, tm_cap=256):
    M, K = x.shape
    tm = _pick_tile(M, tm_cap)
    full = lambda a: pl.BlockSpec(a.shape, lambda i: (0,) * a.ndim)
    row = pl.BlockSpec((tm, K), lambda i: (i, 0))
    return pl.pallas_call(
        _rwkv_lora_kernel,
        out_shape=[jax.ShapeDtypeStruct((M, K), f32)] * 3,
        grid=(M // tm,),
        in_specs=[row, row, full(mu3), full(w1), full(w2w), full(w2a), full(w2g), full(w0), full(a0)],
        out_specs=[row, row, row],
        compiler_params=_cparams("parallel"),
        name="rwkv_lora",
    )(x, xprev, mu3, w1, w2w, w2a, w2g, w0, a0)


def _split_hi_lo(x):
    hi = x.astype(bf16)
    return hi, (x - hi.astype(f32)).astype(bf16)


def _dot_b2(a, b_exact):
    hi, lo = _split_hi_lo(a)
    return (jnp.dot(hi, b_exact, preferred_element_type=f32)
            + jnp.dot(lo, b_exact, preferred_element_type=f32))


def _head_sums(x, ones_bd, npair):
    rows = x.shape[0]
    xs = jnp.concatenate([x[:, g * LANES:(g + 1) * LANES] for g in range(npair)], axis=0)
    s = _dot_b2(xs, ones_bd)
    return jnp.concatenate([s[g * rows:(g + 1) * rows] for g in range(npair)], axis=1)


def _rwkv_scan_kernel(r_ref, k_ref, v_ref, lw_ref, ag_ref, g_ref, s0_ref, kk_ref, ka_ref, rk_ref,
                      lg_ref, lb_ref, y_ref, s_ref, sbd, *, t_real, seg, npair):
    R = RWKV_TILE
    R2 = 2 * R
    nseg = R // seg
    W = npair * LANES
    shift = seg.bit_length() - 1
    c = pl.program_id(2)
    mm = lambda a, b: jnp.dot(a.astype(bf16), b.astype(bf16), preferred_element_type=f32)
    mm_nt = lambda a, b: lax.dot_general(a.astype(bf16), b.astype(bf16), NT_DIMS, preferred_element_type=f32)
    mm_tn = lambda a, b: lax.dot_general(a.astype(bf16), b.astype(bf16), TN_DIMS, preferred_element_type=f32)

    @pl.when(c == 0)
    def _():
        sbd[...] = jnp.zeros(sbd.shape, f32)
        for gi in range(npair):
            for j in range(nseg):
                sbd[gi, j, 0:HD_C, 0:HD_C] = s0_ref[j, 2 * gi]
                sbd[gi, j, HD_C:R2, HD_C:R2] = s0_ref[j, 2 * gi + 1]

    row1 = lax.broadcasted_iota(jnp.int32, (R, 1), 0)
    valid = c * seg + (row1 & (seg - 1)) < t_real
    ld = lambda ref: jnp.where(valid, ref[...].reshape(R, W), 0.0)
    r = ld(r_ref)
    k = ld(k_ref)
    v = ld(v_ref)
    lw = ld(lw_ref)
    ag = ld(ag_ref)

    lane = lax.broadcasted_iota(jnp.int32, (1, LANES), 1)
    m0 = (lane < HD_C).astype(f32)
    m1 = 1.0 - m0
    row2 = lax.broadcasted_iota(jnp.int32, (R2, R2), 0)
    col2 = lax.broadcasted_iota(jnp.int32, (R2, R2), 1)
    ones_bd = ((row2 < HD_C) == (col2 < HD_C)).astype(bf16)
    same_blk = (row2 >> shift) == (col2 >> shift)
    strict = jnp.logical_and(same_blk, row2 > col2)
    incl = jnp.logical_and(same_blk, row2 >= col2)
    eye = (row2 == col2).astype(f32)

    kk = k * kk_ref[...]
    kk = kk / jnp.maximum(jnp.sqrt(_head_sums(kk * kk, ones_bd, npair)), 1e-12)
    kp = k * (1.0 + (ag - 1.0) * ka_ref[...])
    bv = kk * ag

    rowR = lax.broadcasted_iota(jnp.int32, (R, R), 0)
    colR = lax.broadcasted_iota(jnp.int32, (R, R), 1)
    tril = jnp.logical_and((rowR >> shift) == (colR >> shift), rowR >= colR).astype(bf16)
    lw_hi, lw_lo = _split_hi_lo(lw)
    cw = (jnp.dot(tril, lw_hi, preferred_element_type=f32)
          + jnp.dot(tril, lw_lo, preferred_element_type=f32))
    cwl = jnp.concatenate(
        [jnp.broadcast_to(cw[j * seg + seg - 1:(j + 1) * seg, :], (seg, W)) for j in range(nseg)], axis=0)
    p_in = jnp.exp(cw)
    p_inv = jnp.exp(-cw)
    p_tail = jnp.exp(cwl - cw)
    at = -kk * jnp.exp(cw - lw)
    rt = r * p_in
    kt = kp * p_inv
    bt = bv * p_inv
    kh = kp * p_tail
    bh = bv * p_tail
    p_last = jnp.exp(cwl)

    def stack(x, gi):
        xg = x[:, gi * LANES:(gi + 1) * LANES]
        return jnp.concatenate([xg * m0, xg * m1], axis=0)

    def pick(x_st, j):
        if nseg == 1:
            return x_st
        return jnp.concatenate([x_st[j * seg:(j + 1) * seg], x_st[R + j * seg:R + (j + 1) * seg]], axis=0)

    def unpick(parts):
        if nseg == 1:
            return parts[0]
        return jnp.concatenate([q[0:seg] for q in parts] + [q[seg:2 * seg] for q in parts], axis=0)

    n_sq = shift - 1
    y_parts = []
    for gi in range(npair):
        a_st = stack(at, gi)
        r_st = stack(rt, gi)
        v_st = stack(v, gi)
        bh_st = stack(bh, gi)
        kh_st = stack(kh, gi)
        prod = mm_nt(jnp.concatenate([a_st, r_st], axis=0),
                     jnp.concatenate([stack(kt, gi), stack(bt, gi)], axis=0))
        m_ak = jnp.where(strict, prod[0:R2, 0:R2], 0.0)
        m_ab = jnp.where(strict, prod[0:R2, R2:2 * R2], 0.0)
        a_rk = jnp.where(incl, prod[R2:2 * R2, 0:R2], 0.0)
        a_rb = jnp.where(incl, prod[R2:2 * R2, R2:2 * R2], 0.0)

        x = m_ab
        tinv = eye + x
        for _ in range(n_sq):
            x = mm(x, x)
            tinv = tinv + mm(tinv, x)

        as_parts = []
        rs_parts = []
        for j in range(nseg):
            ar_s = mm_nt(jnp.concatenate([pick(a_st, j), pick(r_st, j)], axis=0), sbd[gi, j])
            as_parts.append(ar_s[0:2 * seg])
            rs_parts.append(ar_s[2 * seg:4 * seg])
        u_st = mm(tinv, unpick(as_parts) + mm(m_ak, v_st))
        o_st = unpick(rs_parts) + mm(jnp.concatenate([a_rk, a_rb], axis=1),
                                     jnp.concatenate([v_st, u_st], axis=0))
        y_parts.append(o_st[0:R] + o_st[R:R2])
        for j in range(nseg):
            pl_row = p_last[j * seg:j * seg + 1, gi * LANES:(gi + 1) * LANES]
            upd = mm_tn(jnp.concatenate([pick(u_st, j), pick(v_st, j)], axis=0),
                        jnp.concatenate([pick(bh_st, j), pick(kh_st, j)], axis=0))
            sbd[gi, j] = sbd[gi, j] * pl_row + upd

    y = jnp.concatenate(y_parts, axis=1)
    inv_n = 1.0 / HD_C
    sums = _head_sums(jnp.concatenate([y, r * kp * rk_ref[...]], axis=0), ones_bd, npair)
    mu = sums[0:R] * inv_n
    dy = y - mu
    var = _head_sums(dy * dy, ones_bd, npair) * inv_n
    yn = dy * lax.rsqrt(var + GN_EPS) * lg_ref[...] + lb_ref[...]
    out = (yn + sums[R:R2] * v) * g_ref[...].reshape(R, W)
    y_ref[...] = out.reshape(y_ref.shape)

    @pl.when(c == pl.num_programs(2) - 1)
    def _():
        for gi in range(npair):
            for j in range(nseg):
                s_ref[j, 2 * gi] = sbd[gi, j, 0:HD_C, 0:HD_C]
                s_ref[j, 2 * gi + 1] = sbd[gi, j, HD_C:R2, HD_C:R2]


def _rwkv_scan(rkv, lw, ag, g, s0, kk_p, ka_p, rk_p, lnx_g, lnx_b, t_real, npair=4):
    _, B, T, D = rkv.shape
    seg = min(RWKV_TILE, T)
    nseg = RWKV_TILE // seg
    nc = T // seg
    assert nseg == 1 or nc == 1
    assert B % nseg == 0 and T % seg == 0
    W = npair * LANES
    tok = pl.BlockSpec((nseg, seg, W), lambda b, p, c: (b, c, p))
    rkv_spec = lambda s: pl.BlockSpec((None, nseg, seg, W), lambda b, p, c: (s, b, c, p))
    prm = pl.BlockSpec((1, W), lambda b, p, c: (0, p))
    st = pl.BlockSpec((nseg, 2 * npair, HD_C, HD_C), lambda b, p, c: (b, p, 0, 0))
    row = lambda a: a.reshape(1, D)
    return pl.pallas_call(
        functools.partial(_rwkv_scan_kernel, t_real=t_real, seg=seg, npair=npair),
        out_shape=[jax.ShapeDtypeStruct((B, T, D), f32),
                   jax.ShapeDtypeStruct((B, NH_C, HD_C, HD_C), f32)],
        grid=(B // nseg, D // W, nc),
        in_specs=[rkv_spec(0), rkv_spec(1), rkv_spec(2), tok, tok, tok, st, prm, prm, prm, prm, prm],
        out_specs=[tok, st],
        scratch_shapes=[pltpu.VMEM((npair, nseg, 2 * RWKV_TILE, 2 * RWKV_TILE), f32)],
        compiler_params=_cparams("parallel", "parallel", "arbitrary"),
        name="rwkv_scan",
    )(rkv, rkv, rkv, lw, ag, g, s0, row(kk_p), row(ka_p), row(rk_p), row(lnx_g), row(lnx_b))


def _pad_lora(w1, w2):
    rank = w1.shape[1]
    return (jnp.pad(w1, ((0, 0), (0, LORA_PAD - rank))), jnp.pad(w2, ((0, LORA_PAD - rank), (0, 0))))


def _prepare_params(p):
    w_in = p['w_in_ab'][0]
    n_qkvo = 4 * MIX_A
    n_g = 2 * NH_A
    q = {}
    q['w_in'] = jnp.concatenate([w_in[:, :n_qkvo], w_in[:, n_qkvo + n_g:]], axis=1).astype(bf16)
    q['w_gate'] = jnp.pad(w_in[:, n_qkvo:n_qkvo + n_g], ((0, 0), (0, LANES - n_g)))
    q['b_gate'] = jnp.pad(p['b_if_ab'][0], (0, LANES - n_g)).reshape(1, LANES)
    q['w_out_ab'] = p['w_out_ab'][0].astype(bf16)
    q['w_up'] = p['w_up'].astype(bf16)
    q['w_down'] = p['w_down'].astype(bf16)
    q['w_rkv'] = jnp.stack([p['rw_wr'][0], p['rw_wk'][0], p['rw_wv'][0]]).astype(bf16)
    mu = p['rw_mu'][0]
    q['mu_rkv'] = jnp.stack([mu[0], mu[2], mu[3]])[:, None, :]
    q['mu_lora'] = jnp.stack([mu[1], mu[4], mu[5]])[:, None, :]
    w1, w2w = _pad_lora(p['rw_w1'][0], p['rw_w2'][0])
    a1, w2a = _pad_lora(p['rw_a1'][0], p['rw_a2'][0])
    q['lora_w1'] = jnp.concatenate([w1, a1, p['rw_g1'][0]], axis=1).astype(bf16)
    q['lora_w2w'] = w2w.astype(bf16)
    q['lora_w2a'] = w2a.astype(bf16)
    q['lora_w2g'] = p['rw_g2'][0].astype(bf16)
    q['w_o'] = p['rw_wo'][0].astype(bf16)
    return q


def _trunk(h, states, p, q, t_real):
    mC, mn, mm, rgh, rgc, rwS, rwx = states
    B, T, D = h.shape
    M = B * T
    row = lambda a: a.reshape(1, -1)
    hf = h.reshape(M, D)

    z = _linear(hf, q['w_in'], q['w_in'].shape[1])
    gates = _gates(hf, q['w_gate'], q['b_gate'])
    z3 = z.reshape(B, T, -1)
    hm, c_new, n_new, m_new = _mlstm(z3, gates.reshape(B, T, LANES), mC[0], mn[0], mm[0],
                                     p['mlstm_norm_g'][0], t_real)
    yr, h_last = _rglru(z3, rgc[0], rgh[0], p['rg_conv_w'][0], p['rg_conv_b'][0], p['rg_wa'][0],
                        p['rg_ba'][0], p['rg_wx'][0], p['rg_bx'][0], p['rg_lambda'][0], t_real)
    assert t_real >= CONV_W - 1
    conv_new = z3[:, t_real - (CONV_W - 1):t_real, 4 * MIX_A:4 * MIX_A + RG_W]
    h1 = _proj_ln([(hm.reshape(M, MIX_A), MIX_A, 0), (yr.reshape(M, RG_W), RG_W, 0)],
                  q['w_out_ab'], hf, row(p['ln1_g'][0]), row(p['ln1_b'][0]))
    h2 = _mlp_ln(h1, q['w_up'][0], q['w_down'][0], row(p['ln2_g'][0]), row(p['ln2_b'][0]))

    h2_3 = h2.reshape(B, T, D)
    xprev = jnp.concatenate([rwx[0].astype(f32)[:, None], h2_3[:, :-1]], axis=1).reshape(M, D)
    rkv = _rwkv_rkv(h2, xprev, q['mu_rkv'], q['w_rkv'])
    lw, ag, g = _rwkv_lora(h2, xprev, q['mu_lora'], q['lora_w1'], q['lora_w2w'], q['lora_w2a'],
                           q['lora_w2g'], row(p['rw_w0'][0]), row(p['rw_a0'][0]))
    r3 = lambda a: a.reshape(B, T, D)
    xo, s_new = _rwkv_scan(rkv.reshape(3, B, T, D), r3(lw), r3(ag), r3(g), rwS[0], p['rw_kk'][0],
                           p['rw_ka'][0], p['rw_rk'][0], p['rw_lnx_g'][0], p['rw_lnx_b'][0], t_real)
    xo = xo.reshape(M, D)
    h3 = _proj_ln([(xo, D, 0)], q['w_o'], h2, row(p['ln1_g'][1]), row(p['ln1_b'][1]))
    h4 = _mlp_ln(h3, q['w_up'][1], q['w_down'][1], row(p['ln2_g'][1]), row(p['ln2_b'][1]))
    shift_new = h2_3[:, t_real - 1]

    new_states = (c_new[None], n_new[None], m_new[None], h_last[None], conv_new[None],
                  s_new[None], shift_new[None])
    return h4.reshape(B, T, D), new_states


def kernel(x_prompt, x_sample, state_mlstm_C, state_mlstm_n, state_mlstm_m, state_rglru_h, state_rglru_conv, state_rwkv_S, state_rwkv_shift, meta_tokens, w_in_ab, b_if_ab, mlstm_norm_g, rg_conv_w, rg_conv_b, rg_wa, rg_ba, rg_wx, rg_bx, rg_lambda, w_out_ab, rw_mu, rw_wr, rw_wk, rw_wv, rw_wo, rw_w0, rw_w1, rw_w2, rw_a0, rw_a1, rw_a2, rw_g1, rw_g2, rw_kk, rw_ka, rw_rk, rw_lnx_g, rw_lnx_b, ln1_g, ln1_b, ln2_g, ln2_b, w_up, w_down):
    p = dict(w_in_ab=w_in_ab, b_if_ab=b_if_ab, mlstm_norm_g=mlstm_norm_g, rg_conv_w=rg_conv_w,
             rg_conv_b=rg_conv_b, rg_wa=rg_wa, rg_ba=rg_ba, rg_wx=rg_wx, rg_bx=rg_bx,
             rg_lambda=rg_lambda, w_out_ab=w_out_ab, rw_mu=rw_mu, rw_wr=rw_wr, rw_wk=rw_wk,
             rw_wv=rw_wv, rw_wo=rw_wo, rw_w0=rw_w0, rw_w1=rw_w1, rw_w2=rw_w2, rw_a0=rw_a0,
             rw_a1=rw_a1, rw_a2=rw_a2, rw_g1=rw_g1, rw_g2=rw_g2, rw_kk=rw_kk, rw_ka=rw_ka,
             rw_rk=rw_rk, rw_lnx_g=rw_lnx_g, rw_lnx_b=rw_lnx_b, ln1_g=ln1_g, ln1_b=ln1_b,
             ln2_g=ln2_g, ln2_b=ln2_b, w_up=w_up, w_down=w_down)
    q = _prepare_params(p)
    B, S, D = x_prompt.shape
    dt = x_prompt.dtype
    t_prompt = N_META + S
    t_pad = -(-t_prompt // MLSTM_CHUNK) * MLSTM_CHUNK
    zero_states = (
        jnp.zeros((1, B, NH_A, HD_A, HD_A), f32),
        jnp.zeros((1, B, NH_A, HD_A), f32),
        jnp.zeros((1, B, NH_A), f32),
        jnp.zeros((1, B, RG_W), f32),
        jnp.zeros((1, B, CONV_W - 1, RG_W), dt),
        jnp.zeros((1, B, NH_C, HD_C, HD_C), f32),
        jnp.zeros((1, B, D), dt),
    )
    meta = jnp.broadcast_to(meta_tokens.astype(dt)[None], (B, N_META, D))
    tail = jnp.zeros((B, t_pad - t_prompt, D), dt)
    hp, ps = _trunk(jnp.concatenate([meta, x_prompt, tail], axis=1), zero_states, p, q, t_prompt)
    sample_states = (state_mlstm_C, state_mlstm_n, state_mlstm_m, state_rglru_h, state_rglru_conv,
                     state_rwkv_S, state_rwkv_shift)
    hs, ss = _trunk(x_sample, sample_states, p, q, x_sample.shape[1])
    return (hp[:, N_META:t_prompt], hs, *ps, *ss)
```

```python
import functools

import jax
import jax.numpy as jnp
from jax import lax
from jax.experimental import pallas as pl
from jax.experimental.pallas import tpu as pltpu

f32 = jnp.float32
bf16 = jnp.bfloat16
HI = lax.Precision.HIGHEST

D_MODEL = 2048
N_META = 16
MIX_A = 1024
NH_A = 4
HD_A = 256
RG_W = 1024
RG_BLOCKS = 8
RG_BW = 128
CONV_W = 4
RG_C = 8.0
HD_C = 64
NH_C = 32
D_FF = 8192
LN_EPS = 1e-5
GN_EPS = 64e-5
DEPTH = 2
ALPHA = (2.0 * DEPTH) ** 0.25

LANES = 128
SUBLANES = 8
VMEM_LIMIT_BYTES = 56 * 1024 * 1024

MLSTM_CHUNK = 128
RWKV_TILE = 64
LORA_PAD = 128
NEG_BIG = -1e30

NT_DIMS = (((1,), (1,)), ((), ()))
TN_DIMS = (((0,), (0,)), ((), ()))


def _cparams(*sem):
    return pltpu.CompilerParams(dimension_semantics=sem, vmem_limit_bytes=VMEM_LIMIT_BYTES)


def _pick_tile(n, cap):
    best = None
    for d in range(SUBLANES, min(n, cap) + 1, SUBLANES):
        if n % d == 0:
            best = d
    assert best is not None, (n, cap)
    return best


def _dot_bf16(a, b):
    return jnp.dot(a.astype(bf16), b.astype(bf16), preferred_element_type=f32)


def _dot_hi(a, b, dims=None):
    if dims is None:
        return jnp.dot(a, b, precision=HI, preferred_element_type=f32)
    return lax.dot_general(a, b, dims, precision=HI, preferred_element_type=f32)


def _layer_norm_rows(t, g, b):
    mu = jnp.mean(t, axis=-1, keepdims=True)
    d = t - mu
    var = jnp.mean(d * d, axis=-1, keepdims=True)
    return d * lax.rsqrt(var + LN_EPS) * g + b


def _linear_kernel(x_ref, w_ref, o_ref, xb_ref):
    @pl.when(pl.program_id(1) == 0)
    def _():
        xb_ref[...] = x_ref[...].astype(bf16)

    o_ref[...] = jnp.dot(xb_ref[...], w_ref[...], preferred_element_type=f32)


def _linear(x, w, n_out, tm_cap=512, tn=1024):
    M, K = x.shape
    tm = _pick_tile(M, tm_cap)
    return pl.pallas_call(
        _linear_kernel,
        out_shape=jax.ShapeDtypeStruct((M, n_out), f32),
        grid=(M // tm, n_out // tn),
        in_specs=[pl.BlockSpec((tm, K), lambda i, j: (i, 0)),
                  pl.BlockSpec((K, tn), lambda i, j: (0, j))],
        out_specs=pl.BlockSpec((tm, tn), lambda i, j: (i, j)),
        scratch_shapes=[pltpu.VMEM((tm, K), bf16)],
        compiler_params=_cparams("parallel", "arbitrary"),
        name="linear",
    )(x, w)


def _gates_kernel(x_ref, w_ref, b_ref, o_ref):
    pre = _dot_hi(x_ref[...], w_ref[...]) + b_ref[...]
    lane = lax.broadcasted_iota(jnp.int32, pre.shape, 1)
    o_ref[...] = jnp.where(lane < NH_A, pre,
                           jnp.where(lane < 2 * NH_A, jax.nn.log_sigmoid(pre), 0.0))


def _gates(x, w_g, b_g, tm_cap=512):
    M, K = x.shape
    tm = _pick_tile(M, tm_cap)
    return pl.pallas_call(
        _gates_kernel,
        out_shape=jax.ShapeDtypeStruct((M, LANES), f32),
        grid=(M // tm,),
        in_specs=[pl.BlockSpec((tm, K), lambda i: (i, 0)),
                  pl.BlockSpec((K, LANES), lambda i: (0, 0)),
                  pl.BlockSpec((1, LANES), lambda i: (0, 0))],
        out_specs=pl.BlockSpec((tm, LANES), lambda i: (i, 0)),
        compiler_params=_cparams("parallel"),
        name="gates",
    )(x, w_g, b_g)


def _proj_ln_kernel(*refs, n_x):
    x_refs = refs[:n_x]
    w_ref, res_ref, g_ref, b_ref, o_ref = refs[n_x:]
    y = None
    k0 = 0
    for x_ref in x_refs:
        kc = x_ref.shape[1]
        part = _dot_bf16(x_ref[...], w_ref[k0:k0 + kc, :])
        y = part if y is None else y + part
        k0 += kc
    t = ALPHA * res_ref[...] + y
    o_ref[...] = _layer_norm_rows(t, g_ref[...], b_ref[...])


def _proj_ln(xs, w, res, g, b, tm_cap=256):
    M, N = res.shape
    tm = _pick_tile(M, tm_cap)
    in_specs = []
    args = []
    for arr, width, blk in xs:
        in_specs.append(pl.BlockSpec((tm, width), functools.partial(lambda i, blk: (i, blk), blk=blk)))
        args.append(arr)
    K = w.shape[0]
    in_specs += [pl.BlockSpec((K, N), lambda i: (0, 0)),
                 pl.BlockSpec((tm, N), lambda i: (i, 0)),
                 pl.BlockSpec((1, N), lambda i: (0, 0)),
                 pl.BlockSpec((1, N), lambda i: (0, 0))]
    return pl.pallas_call(
        functools.partial(_proj_ln_kernel, n_x=len(xs)),
        out_shape=jax.ShapeDtypeStruct((M, N), f32),
        grid=(M // tm,),
        in_specs=in_specs,
        out_specs=pl.BlockSpec((tm, N), lambda i: (i, 0)),
        compiler_params=_cparams("parallel"),
        name="proj_ln",
    )(*args, w, res, g, b)


def _mlp_ln_kernel(x_ref, wu_ref, wd_ref, g_ref, b_ref, o_ref, xb_ref, acc_ref):
    f = pl.program_id(1)

    @pl.when(f == 0)
    def _():
        xb_ref[...] = x_ref[...].astype(bf16)
        acc_ref[...] = jnp.zeros_like(acc_ref)

    a = jnp.dot(xb_ref[...], wu_ref[...], preferred_element_type=f32)
    a = jnp.square(jnp.maximum(a, 0.0))
    acc_ref[...] += jnp.dot(a.astype(bf16), wd_ref[...], preferred_element_type=f32)

    @pl.when(f == pl.num_programs(1) - 1)
    def _():
        t = ALPHA * x_ref[...] + acc_ref[...]
        o_ref[...] = _layer_norm_rows(t, g_ref[...], b_ref[...])


def _mlp_ln(x, w_up, w_down, g, b, tm_cap=512, tf=512):
    M, K = x.shape
    F = w_up.shape[1]
    tm = _pick_tile(M, tm_cap)
    return pl.pallas_call(
        _mlp_ln_kernel,
        out_shape=jax.ShapeDtypeStruct((M, K), f32),
        grid=(M // tm, F // tf),
        in_specs=[pl.BlockSpec((tm, K), lambda i, j: (i, 0)),
                  pl.BlockSpec((K, tf), lambda i, j: (0, j)),
                  pl.BlockSpec((tf, K), lambda i, j: (j, 0)),
                  pl.BlockSpec((1, K), lambda i, j: (0, 0)),
                  pl.BlockSpec((1, K), lambda i, j: (0, 0))],
        out_specs=pl.BlockSpec((tm, K), lambda i, j: (i, 0)),
        scratch_shapes=[pltpu.VMEM((tm, K), bf16), pltpu.VMEM((tm, K), f32)],
        compiler_params=_cparams("parallel", "arbitrary"),
        name="mlp_ln",
    )(x, w_up, w_down, g, b)


def _pad_rows(x, rows):
    if x.shape[0] == rows:
        return x
    return jnp.concatenate([x, jnp.zeros((rows - x.shape[0],) + x.shape[1:], x.dtype)], axis=0)


def _mlstm_kernel(q_ref, k_ref, v_ref, o_ref, g_ref, c0_ref, n0_ref, m0_ref, ng_ref,
                  h_ref, c_ref, n_ref, m_ref, cs, ns, ms, *, t_real):
    L = MLSTM_CHUNK
    nb, lb = q_ref.shape[0], q_ref.shape[1]
    c = pl.program_id(1)

    @pl.when(c == 0)
    def _():
        cs[...] = c0_ref[...]
        ns[...] = n0_ref[...]
        ms[...] = m0_ref[...]

    nvalid = jnp.minimum(t_real - c * lb, lb)
    vrow = lax.broadcasted_iota(jnp.int32, (L, 1), 0) < nvalid
    vcol = lax.broadcasted_iota(jnp.int32, (1, L), 1) < nvalid
    rowL = lax.broadcasted_iota(jnp.int32, (L, L), 0)
    colL = lax.broadcasted_iota(jnp.int32, (L, L), 1)
    causal = rowL >= colL
    tril = causal.astype(f32)
    chains = [(b, h) for b in range(nb) for h in range(NH_A)]
    hs = lambda h: slice(h * HD_A, (h + 1) * HD_A)

    def ld(ref, b, h):
        return jnp.where(vrow, _pad_rows(ref[b, :, hs(h)], L), 0.0)

    q = [ld(q_ref, b, h) for b, h in chains]
    k = [ld(k_ref, b, h) * (HD_A ** -0.5) for b, h in chains]
    vb = [ld(v_ref, b, h).astype(bf16) for b, h in chains]
    qb = [x.astype(bf16) for x in q]
    qk = [lax.dot_general(qb[i], k[i].astype(bf16), NT_DIMS, preferred_element_type=f32)
          for i in range(len(chains))]

    gate = [jnp.where(vrow, _pad_rows(g_ref[b], L), 0.0) for b in range(nb)]
    fcum = [_dot_hi(tril, gate[b]) for b in range(nb)]
    gate_t = [gate[b].T for b in range(nb)]
    fcum_t = [fcum[b].T for b in range(nb)]

    m_prev, fc, fr, igc, igr, m_row, inter, s = [], [], [], [], [], [], [], []
    for i, (b, h) in enumerate(chains):
        fc.append(fcum[b][:, NH_A + h:NH_A + h + 1])
        fr.append(fcum_t[b][NH_A + h:NH_A + h + 1, :])
        igc.append(jnp.where(vrow, gate[b][:, h:h + 1], NEG_BIG))
        igr.append(jnp.where(vcol, gate_t[b][h:h + 1, :], NEG_BIG))
        m_prev.append(ms[b, h, 0:1, 0:1])
        logd = jnp.where(causal, fc[i] - fr[i] + igr[i], NEG_BIG)
        b_in = fc[i] + m_prev[i]
        m_row.append(jnp.maximum(b_in, jnp.max(logd, axis=1, keepdims=True)))
        inter.append(jnp.exp(b_in - m_row[i]))
        s.append(qk[i] * jnp.exp(logd - m_row[i]))

    cmat = [cs[b, h] for b, h in chains]
    nrow = [ns[b, h] for b, h in chains]
    sv = [jnp.dot(s[i].astype(bf16), vb[i], preferred_element_type=f32) for i in range(len(chains))]
    qc = [jnp.dot(qb[i], cmat[i].astype(bf16), preferred_element_type=f32) for i in range(len(chains))]
    for i, (b, h) in enumerate(chains):
        num = sv[i] + inter[i] * qc[i]
        den = jnp.sum(s[i], axis=1, keepdims=True) \
            + inter[i] * jnp.sum(q[i] * nrow[i], axis=1, keepdims=True)
        hh = num / jnp.maximum(jnp.abs(den), jnp.exp(-m_row[i]))
        mu = jnp.mean(hh, axis=1, keepdims=True)
        dv = hh - mu
        var = jnp.mean(dv * dv, axis=1, keepdims=True)
        hn = dv * lax.rsqrt(var + LN_EPS) * ng_ref[h]
        out = hn * jax.nn.sigmoid(ld(o_ref, b, h))
        h_ref[b, :, hs(h)] = out[0:lb]

    kw, m_new, dec = [], [], []
    for i in range(len(chains)):
        fl = fc[i][L - 1:L, :]
        m_new.append(jnp.maximum(fl + m_prev[i], jnp.max(fl - fr[i] + igr[i], axis=1, keepdims=True)))
        dec.append(jnp.exp(fl + m_prev[i] - m_new[i]))
        kw.append(k[i] * jnp.exp(fl - fc[i] + igc[i] - m_new[i]))
    kv = [lax.dot_general(kw[i].astype(bf16), vb[i], TN_DIMS, preferred_element_type=f32)
          for i in range(len(chains))]
    for i, (b, h) in enumerate(chains):
        cs[b, h] = dec[i] * cmat[i] + kv[i]
        ns[b, h] = dec[i] * nrow[i] + jnp.sum(kw[i], axis=0, keepdims=True)
        ms[b, h] = jnp.broadcast_to(m_new[i], (1, LANES))

    @pl.when(c == pl.num_programs(1) - 1)
    def _():
        c_ref[...] = cs[...]
        n_ref[...] = ns[...]
        m_ref[...] = ms[...]


def _mlstm(z, gates, c0, n0, m0, norm_g, t_real):
    B, T, _ = z.shape
    lb = min(MLSTM_CHUNK, T)
    nc = T // lb
    nb = 1 if nc > 1 else 2
    assert B % nb == 0 and T % lb == 0
    m0b = jnp.broadcast_to(m0[:, :, None, None], (B, NH_A, 1, LANES))
    zspec = lambda group: pl.BlockSpec((nb, lb, MIX_A), lambda b, c: (b, c, group))
    st_spec = lambda r, w: pl.BlockSpec((nb, NH_A, r, w), lambda b, c: (b, 0, 0, 0))
    st_shapes = [(HD_A, HD_A), (1, HD_A), (1, LANES)]
    outs = pl.pallas_call(
        functools.partial(_mlstm_kernel, t_real=t_real),
        out_shape=[jax.ShapeDtypeStruct((B, T, MIX_A), f32)]
        + [jax.ShapeDtypeStruct((B, NH_A) + s, f32) for s in st_shapes],
        grid=(B // nb, nc),
        in_specs=[zspec(0), zspec(1), zspec(2), zspec(3),
                  pl.BlockSpec((nb, lb, LANES), lambda b, c: (b, c, 0))]
        + [st_spec(*s) for s in st_shapes]
        + [pl.BlockSpec((NH_A, 1, HD_A), lambda b, c: (0, 0, 0))],
        out_specs=[pl.BlockSpec((nb, lb, MIX_A), lambda b, c: (b, c, 0))]
        + [st_spec(*s) for s in st_shapes],
        scratch_shapes=[pltpu.VMEM((nb, NH_A) + s, f32) for s in st_shapes],
        compiler_params=_cparams("parallel", "arbitrary"),
        name="mlstm",
    )(z, z, z, z, gates, c0, n0.reshape(B, NH_A, 1, HD_A), m0b, norm_g.reshape(NH_A, 1, HD_A))
    hm, c_new, n_new, m_new = outs
    return hm, c_new, n_new.reshape(B, NH_A, HD_A), m_new[:, :, 0, 0]


def _one_minus_exp(x):
    u = jnp.exp(x)
    um1 = u - 1.0
    safe = jnp.where(um1 == 0.0, -x, -(um1 * x) / jnp.log(jnp.where(um1 == 0.0, 0.5, u)))
    return jnp.where(x > -0.5, safe, 1.0 - u)


def _rglru_kernel(xr_ref, gr_ref, conv0_ref, h0_ref, cw_ref, cb_ref, wa_ref, ba_ref, wx_ref,
                  bx_ref, lam_ref, y_ref, hl_ref, xp, a_s, u_s, h_s, hc, *, t_real):
    lr = xr_ref.shape[0]
    c = pl.program_id(1)
    hist = SUBLANES - (CONV_W - 1)

    @pl.when(c == 0)
    def _():
        xp[0:SUBLANES, :] = jnp.zeros((SUBLANES, RG_W), f32)
        xp[hist:SUBLANES, :] = conv0_ref[...]
        hc[...] = h0_ref[...]

    xp[SUBLANES:SUBLANES + lr, :] = xr_ref[...]
    xc = cb_ref[...]
    for j in range(CONV_W):
        xc = xc + xp[hist + j:hist + j + lr, :] * cw_ref[j:j + 1, :]
    xp[hist:SUBLANES, :] = xp[hist + lr:SUBLANES + lr, :]

    xcb = xc.astype(bf16)
    ra = []
    rx = []
    for n in range(RG_BLOCKS):
        xn = xcb[:, n * RG_BW:(n + 1) * RG_BW]
        ra.append(jnp.dot(xn, wa_ref[n].astype(bf16), preferred_element_type=f32))
        rx.append(jnp.dot(xn, wx_ref[n].astype(bf16), preferred_element_type=f32))
    r = jax.nn.sigmoid(jnp.concatenate(ra, axis=1) + ba_ref[...])
    gi = jax.nn.sigmoid(jnp.concatenate(rx, axis=1) + bx_ref[...])
    log_a = -RG_C * r * jax.nn.softplus(-lam_ref[...])
    a = jnp.exp(log_a)
    u = jnp.sqrt(_one_minus_exp(2.0 * log_a)) * (gi * xc)
    valid = (c * lr + lax.broadcasted_iota(jnp.int32, (lr, 1), 0)) < t_real
    a_s[...] = jnp.where(valid, a, 1.0)
    u_s[...] = jnp.where(valid, u, 0.0)

    def step(t, h):
        h = a_s[pl.ds(t, 1), :] * h + u_s[pl.ds(t, 1), :]
        h_s[pl.ds(t, 1), :] = h
        return h

    h_last = lax.fori_loop(0, lr, step, hc[...])
    hc[...] = h_last
    y_ref[...] = h_s[...] * jax.nn.gelu(gr_ref[...])

    @pl.when(c == pl.num_programs(1) - 1)
    def _():
        hl_ref[...] = h_last


def _rglru(z, conv0, h0, conv_w, conv_b, wa, ba, wx, bx, lam, t_real, lr_cap=544):
    B, T, _ = z.shape
    lr = _pick_tile(T, lr_cap)
    row = lambda a: a.reshape(1, RG_W)
    full = lambda shp: pl.BlockSpec(shp, lambda b, c: (0,) * len(shp))
    y, h_last = pl.pallas_call(
        functools.partial(_rglru_kernel, t_real=t_real),
        out_shape=[jax.ShapeDtypeStruct((B, T, RG_W), f32),
                   jax.ShapeDtypeStruct((B, 1, RG_W), f32)],
        grid=(B, T // lr),
        in_specs=[pl.BlockSpec((None, lr, RG_W), lambda b, c: (b, c, 4)),
                  pl.BlockSpec((None, lr, RG_W), lambda b, c: (b, c, 5)),
                  pl.BlockSpec((None, CONV_W - 1, RG_W), lambda b, c: (b, 0, 0)),
                  pl.BlockSpec((None, 1, RG_W), lambda b, c: (b, 0, 0)),
                  full((CONV_W, RG_W)), full((1, RG_W)),
                  full((RG_BLOCKS, RG_BW, RG_BW)), full((1, RG_W)),
                  full((RG_BLOCKS, RG_BW, RG_BW)), full((1, RG_W)), full((1, RG_W))],
        out_specs=[pl.BlockSpec((None, lr, RG_W), lambda b, c: (b, c, 0)),
                   pl.BlockSpec((None, 1, RG_W), lambda b, c: (b, 0, 0))],
        scratch_shapes=[pltpu.VMEM((SUBLANES + lr, RG_W), f32), pltpu.VMEM((lr, RG_W), f32),
                        pltpu.VMEM((lr, RG_W), f32), pltpu.VMEM((lr, RG_W), f32),
                        pltpu.VMEM((1, RG_W), f32)],
        compiler_params=_cparams("parallel", "arbitrary"),
        name="rglru",
    )(z, z, conv0, h0.reshape(B, 1, RG_W), conv_w, row(conv_b), wa, row(ba), wx, row(bx), row(lam))
    return y, h_last.reshape(B, RG_W)


def _rwkv_rkv_kernel(x_ref, xp_ref, mu_ref, w_ref, o_ref, xs_ref):
    @pl.when(pl.program_id(2) == 0)
    def _():
        x = x_ref[...]
        xs_ref[...] = (x + (xp_ref[...] - x) * mu_ref[...]).astype(bf16)

    o_ref[...] = jnp.dot(xs_ref[...], w_ref[...], preferred_element_type=f32)


def _rwkv_rkv(x, xprev, mu3, w3, tm_cap=512, tn=1024):
    M, K = x.shape
    tm = _pick_tile(M, tm_cap)
    return pl.pallas_call(
        _rwkv_rkv_kernel,
        out_shape=jax.ShapeDtypeStruct((3, M, K), f32),
        grid=(M // tm, 3, K // tn),
        in_specs=[pl.BlockSpec((tm, K), lambda i, s, j: (i, 0)),
                  pl.BlockSpec((tm, K), lambda i, s, j: (i, 0)),
                  pl.BlockSpec((None, 1, K), lambda i, s, j: (s, 0, 0)),
                  pl.BlockSpec((None, K, tn), lambda i, s, j: (s, 0, j))],
        out_specs=pl.BlockSpec((None, tm, tn), lambda i, s, j: (s, i, j)),
        scratch_shapes=[pltpu.VMEM((tm, K), bf16)],
        compiler_params=_cparams("parallel", "arbitrary", "arbitrary"),
        name="rwkv_rkv",
    )(x, xprev, mu3, w3)


def _rwkv_lora_kernel(x_ref, xp_ref, mu_ref, w1_ref, w2w_ref, w2a_ref, w2g_ref, w0_ref, a0_ref,
                      lw_ref, a_ref, g_ref):
    x = x_ref[...]
    dx = xp_ref[...] - x
    xw = (x + dx * mu_ref[0]).astype(bf16)
    xa = (x + dx * mu_ref[1]).astype(bf16)
    xg = (x + dx * mu_ref[2]).astype(bf16)
    r = LORA_PAD
    hw = jnp.tanh(jnp.dot(xw, w1_ref[:, 0:r], preferred_element_type=f32))
    ha = jnp.dot(xa, w1_ref[:, r:2 * r], preferred_element_type=f32)
    hg = jax.nn.sigmoid(jnp.dot(xg, w1_ref[:, 2 * r:], preferred_element_type=f32))
    w = w0_ref[...] + jnp.dot(hw.astype(bf16), w2w_ref[...], preferred_element_type=f32)
    w = -jax.nn.softplus(-w) - 0.5
    lw_ref[...] = -jnp.exp(w)
    a = a0_ref[...] + jnp.dot(ha.astype(bf16), w2a_ref[...], preferred_element_type=f32)
    a_ref[...] = jax.nn.sigmoid(a)
    g_ref[...] = jnp.dot(hg.astype(bf16), w2g_ref[...], preferred_element_type=f32)


def _rwkv_lora(x, xprev, mu3, w1, w2w, w2a, w2g, w0, a0, tm_cap=256):
    M, K = x.shape
    tm = _pick_tile(M, tm_cap)
    full = lambda a: pl.BlockSpec(a.shape, lambda i: (0,) * a.ndim)
    row = pl.BlockSpec((tm, K), lambda i: (i, 0))
    return pl.pallas_call(
        _rwkv_lora_kernel,
        out_shape=[jax.ShapeDtypeStruct((M, K), f32)] * 3,
        grid=(M // tm,),
        in_specs=[row, row, full(mu3), full(w1), full(w2w), full(w2a), full(w2g), full(w0), full(a0)],
        out_specs=[row, row, row],
        compiler_params=_cparams("parallel"),
        name="rwkv_lora",
    )(x, xprev, mu3, w1, w2w, w2a, w2g, w0, a0)


def _split_hi_lo(x):
    hi = x.astype(bf16)
    return hi, (x - hi.astype(f32)).astype(bf16)


def _dot_b2(a, b_exact):
    hi, lo = _split_hi_lo(a)
    return (jnp.dot(hi, b_exact, preferred_element_type=f32)
            + jnp.dot(lo, b_exact, preferred_element_type=f32))


def _head_sums(x, ones_bd, npair):
    rows = x.shape[0]
    xs = jnp.concatenate([x[:, g * LANES:(g + 1) * LANES] for g in range(npair)], axis=0)
    s = _dot_b2(xs, ones_bd)
    return jnp.concatenate([s[g * rows:(g + 1) * rows] for g in range(npair)], axis=1)


def _rwkv_scan_kernel(r_ref, k_ref, v_ref, lw_ref, ag_ref, g_ref, s0_ref, kk_ref, ka_ref, rk_ref,
                      lg_ref, lb_ref, y_ref, s_ref, sbd, *, t_real, seg, npair):
    R = RWKV_TILE
    R2 = 2 * R
    nseg = R // seg
    W = npair * LANES
    shift = seg.bit_length() - 1
    c = pl.program_id(2)
    mm = lambda a, b: jnp.dot(a.astype(bf16), b.astype(bf16), preferred_element_type=f32)
    mm_nt = lambda a, b: lax.dot_general(a.astype(bf16), b.astype(bf16), NT_DIMS, preferred_element_type=f32)
    mm_tn = lambda a, b: lax.dot_general(a.astype(bf16), b.astype(bf16), TN_DIMS, preferred_element_type=f32)

    @pl.when(c == 0)
    def _():
        sbd[...] = jnp.zeros(sbd.shape, f32)
        for gi in range(npair):
            for j in range(nseg):
                sbd[gi, j, 0:HD_C, 0:HD_C] = s0_ref[j, 2 * gi]
                sbd[gi, j, HD_C:R2, HD_C:R2] = s0_ref[j, 2 * gi + 1]

    row1 = lax.broadcasted_iota(jnp.int32, (R, 1), 0)
    valid = c * seg + (row1 & (seg - 1)) < t_real
    ld = lambda ref: jnp.where(valid, ref[...].reshape(R, W), 0.0)
    r = ld(r_ref)
    k = ld(k_ref)
    v = ld(v_ref)
    lw = ld(lw_ref)
    ag = ld(ag_ref)

    lane = lax.broadcasted_iota(jnp.int32, (1, LANES), 1)
    m0 = (lane < HD_C).astype(f32)
    m1 = 1.0 - m0
    row2 = lax.broadcasted_iota(jnp.int32, (R2, R2), 0)
    col2 = lax.broadcasted_iota(jnp.int32, (R2, R2), 1)
    ones_bd = ((row2 < HD_C) == (col2 < HD_C)).astype(bf16)
    same_blk = (row2 >> shift) == (col2 >> shift)
    strict = jnp.logical_and(same_blk, row2 > col2)
    incl = jnp.logical_and(same_blk, row2 >= col2)
    eye = (row2 == col2).astype(f32)

    kk = k * kk_ref[...]
    kk = kk / jnp.maximum(jnp.sqrt(_head_sums(kk * kk, ones_bd, npair)), 1e-12)
    kp = k * (1.0 + (ag - 1.0) * ka_ref[...])
    bv = kk * ag

    rowR = lax.broadcasted_iota(jnp.int32, (R, R), 0)
    colR = lax.broadcasted_iota(jnp.int32, (R, R), 1)
    tril = jnp.logical_and((rowR >> shift) == (colR >> shift), rowR >= colR).astype(bf16)
    lw_hi, lw_lo = _split_hi_lo(lw)
    cw = (jnp.dot(tril, lw_hi, preferred_element_type=f32)
          + jnp.dot(tril, lw_lo, preferred_element_type=f32))
    cwl = jnp.concatenate(
        [jnp.broadcast_to(cw[j * seg + seg - 1:(j + 1) * seg, :], (seg, W)) for j in range(nseg)], axis=0)
    p_in = jnp.exp(cw)
    p_inv = jnp.exp(-cw)
    p_tail = jnp.exp(cwl - cw)
    at = -kk * jnp.exp(cw - lw)
    rt = r * p_in
    kt = kp * p_inv
    bt = bv * p_inv
    kh = kp * p_tail
    bh = bv * p_tail
    p_last = jnp.exp(cwl)

    def stack(x, gi):
        xg = x[:, gi * LANES:(gi + 1) * LANES]
        return jnp.concatenate([xg * m0, xg * m1], axis=0)

    def pick(x_st, j):
        if nseg == 1:
            return x_st
        return jnp.concatenate([x_st[j * seg:(j + 1) * seg], x_st[R + j * seg:R + (j + 1) * seg]], axis=0)

    def unpick(parts):
        if nseg == 1:
            return parts[0]
        return jnp.concatenate([q[0:seg] for q in parts] + [q[seg:2 * seg] for q in parts], axis=0)

    n_sq = shift - 1
    pairs = range(npair)
    s_prev = [[sbd[gi, j] for j in range(nseg)] for gi in pairs]
    a_st = [stack(at, gi) for gi in pairs]
    r_st = [stack(rt, gi) for gi in pairs]
    v_st = [stack(v, gi) for gi in pairs]
    prod = [mm_nt(jnp.concatenate([a_st[gi], r_st[gi]], axis=0),
                  jnp.concatenate([stack(kt, gi), stack(bt, gi)], axis=0)) for gi in pairs]
    m_ak = [jnp.where(strict, prod[gi][0:R2, 0:R2], 0.0) for gi in pairs]
    a_rk = [jnp.where(incl, prod[gi][R2:2 * R2, 0:R2], 0.0) for gi in pairs]
    a_rb = [jnp.where(incl, prod[gi][R2:2 * R2, R2:2 * R2], 0.0) for gi in pairs]

    x = [jnp.where(strict, prod[gi][0:R2, R2:2 * R2], 0.0) for gi in pairs]
    tinv = [eye + x[gi] for gi in pairs]
    for _ in range(n_sq):
        x = [mm(x[gi], x[gi]) for gi in pairs]
        tinv = [tinv[gi] + mm(tinv[gi], x[gi]) for gi in pairs]

    ar_s = [[mm_nt(jnp.concatenate([pick(a_st[gi], j), pick(r_st[gi], j)], axis=0), s_prev[gi][j])
             for j in range(nseg)] for gi in pairs]
    mv = [mm(m_ak[gi], v_st[gi]) for gi in pairs]
    u_st = [mm(tinv[gi], unpick([q[0:2 * seg] for q in ar_s[gi]]) + mv[gi]) for gi in pairs]
    o_st = [unpick([q[2 * seg:4 * seg] for q in ar_s[gi]])
            + mm(jnp.concatenate([a_rk[gi], a_rb[gi]], axis=1),
                 jnp.concatenate([v_st[gi], u_st[gi]], axis=0)) for gi in pairs]
    for gi in pairs:
        bh_st = stack(bh, gi)
        kh_st = stack(kh, gi)
        for j in range(nseg):
            pl_row = p_last[j * seg:j * seg + 1, gi * LANES:(gi + 1) * LANES]
            upd = mm_tn(jnp.concatenate([pick(u_st[gi], j), pick(v_st[gi], j)], axis=0),
                        jnp.concatenate([pick(bh_st, j), pick(kh_st, j)], axis=0))
            sbd[gi, j] = s_prev[gi][j] * pl_row + upd

    y = jnp.concatenate([o[0:R] + o[R:R2] for o in o_st], axis=1)
    inv_n = 1.0 / HD_C
    sums = _head_sums(jnp.concatenate([y, r * kp * rk_ref[...]], axis=0), ones_bd, npair)
    mu = sums[0:R] * inv_n
    dy = y - mu
    var = _head_sums(dy * dy, ones_bd, npair) * inv_n
    yn = dy * lax.rsqrt(var + GN_EPS) * lg_ref[...] + lb_ref[...]
    out = (yn + sums[R:R2] * v) * g_ref[...].reshape(R, W)
    y_ref[...] = out.reshape(y_ref.shape)

    @pl.when(c == pl.num_programs(2) - 1)
    def _():
        for gi in range(npair):
            for j in range(nseg):
                s_ref[j, 2 * gi] = sbd[gi, j, 0:HD_C, 0:HD_C]
                s_ref[j, 2 * gi + 1] = sbd[gi, j, HD_C:R2, HD_C:R2]


def _rwkv_scan(rkv, lw, ag, g, s0, kk_p, ka_p, rk_p, lnx_g, lnx_b, t_real, npair=16):
    _, B, T, D = rkv.shape
    seg = min(RWKV_TILE, T)
    nseg = RWKV_TILE // seg
    nc = T // seg
    assert nseg == 1 or nc == 1
    assert B % nseg == 0 and T % seg == 0
    W = npair * LANES
    tok = pl.BlockSpec((nseg, seg, W), lambda b, p, c: (b, c, p))
    rkv_spec = lambda s: pl.BlockSpec((None, nseg, seg, W), lambda b, p, c: (s, b, c, p))
    prm = pl.BlockSpec((1, W), lambda b, p, c: (0, p))
    st = pl.BlockSpec((nseg, 2 * npair, HD_C, HD_C), lambda b, p, c: (b, p, 0, 0))
    row = lambda a: a.reshape(1, D)
    return pl.pallas_call(
        functools.partial(_rwkv_scan_kernel, t_real=t_real, seg=seg, npair=npair),
        out_shape=[jax.ShapeDtypeStruct((B, T, D), f32),
                   jax.ShapeDtypeStruct((B, NH_C, HD_C, HD_C), f32)],
        grid=(B // nseg, D // W, nc),
        in_specs=[rkv_spec(0), rkv_spec(1), rkv_spec(2), tok, tok, tok, st, prm, prm, prm, prm, prm],
        out_specs=[tok, st],
        scratch_shapes=[pltpu.VMEM((npair, nseg, 2 * RWKV_TILE, 2 * RWKV_TILE), f32)],
        compiler_params=_cparams("parallel", "parallel", "arbitrary"),
        name="rwkv_scan",
    )(rkv, rkv, rkv, lw, ag, g, s0, row(kk_p), row(ka_p), row(rk_p), row(lnx_g), row(lnx_b))


def _pad_lora(w1, w2):
    rank = w1.shape[1]
    return (jnp.pad(w1, ((0, 0), (0, LORA_PAD - rank))), jnp.pad(w2, ((0, LORA_PAD - rank), (0, 0))))


def _prepare_params(p):
    w_in = p['w_in_ab'][0]
    n_qkvo = 4 * MIX_A
    n_g = 2 * NH_A
    q = {}
    q['w_in'] = jnp.concatenate([w_in[:, :n_qkvo], w_in[:, n_qkvo + n_g:]], axis=1).astype(bf16)
    q['w_gate'] = jnp.pad(w_in[:, n_qkvo:n_qkvo + n_g], ((0, 0), (0, LANES - n_g)))
    q['b_gate'] = jnp.pad(p['b_if_ab'][0], (0, LANES - n_g)).reshape(1, LANES)
    q['w_out_ab'] = p['w_out_ab'][0].astype(bf16)
    q['w_up'] = p['w_up'].astype(bf16)
    q['w_down'] = p['w_down'].astype(bf16)
    q['w_rkv'] = jnp.stack([p['rw_wr'][0], p['rw_wk'][0], p['rw_wv'][0]]).astype(bf16)
    mu = p['rw_mu'][0]
    q['mu_rkv'] = jnp.stack([mu[0], mu[2], mu[3]])[:, None, :]
    q['mu_lora'] = jnp.stack([mu[1], mu[4], mu[5]])[:, None, :]
    w1, w2w = _pad_lora(p['rw_w1'][0], p['rw_w2'][0])
    a1, w2a = _pad_lora(p['rw_a1'][0], p['rw_a2'][0])
    q['lora_w1'] = jnp.concatenate([w1, a1, p['rw_g1'][0]], axis=1).astype(bf16)
    q['lora_w2w'] = w2w.astype(bf16)
    q['lora_w2a'] = w2a.astype(bf16)
    q['lora_w2g'] = p['rw_g2'][0].astype(bf16)
    q['w_o'] = p['rw_wo'][0].astype(bf16)
    return q


def _trunk(h, states, p, q, t_real):
    mC, mn, mm, rgh, rgc, rwS, rwx = states
    B, T, D = h.shape
    M = B * T
    row = lambda a: a.reshape(1, -1)
    hf = h.reshape(M, D)

    z = _linear(hf, q['w_in'], q['w_in'].shape[1])
    gates = _gates(hf, q['w_gate'], q['b_gate'])
    z3 = z.reshape(B, T, -1)
    hm, c_new, n_new, m_new = _mlstm(z3, gates.reshape(B, T, LANES), mC[0], mn[0], mm[0],
                                     p['mlstm_norm_g'][0], t_real)
    yr, h_last = _rglru(z3, rgc[0], rgh[0], p['rg_conv_w'][0], p['rg_conv_b'][0], p['rg_wa'][0],
                        p['rg_ba'][0], p['rg_wx'][0], p['rg_bx'][0], p['rg_lambda'][0], t_real)
    assert t_real >= CONV_W - 1
    conv_new = z3[:, t_real - (CONV_W - 1):t_real, 4 * MIX_A:4 * MIX_A + RG_W]
    h1 = _proj_ln([(hm.reshape(M, MIX_A), MIX_A, 0), (yr.reshape(M, RG_W), RG_W, 0)],
                  q['w_out_ab'], hf, row(p['ln1_g'][0]), row(p['ln1_b'][0]))
    h2 = _mlp_ln(h1, q['w_up'][0], q['w_down'][0], row(p['ln2_g'][0]), row(p['ln2_b'][0]))

    h2_3 = h2.reshape(B, T, D)
    xprev = jnp.concatenate([rwx[0].astype(f32)[:, None], h2_3[:, :-1]], axis=1).reshape(M, D)
    rkv = _rwkv_rkv(h2, xprev, q['mu_rkv'], q['w_rkv'])
    lw, ag, g = _rwkv_lora(h2, xprev, q['mu_lora'], q['lora_w1'], q['lora_w2w'], q['lora_w2a'],
                           q['lora_w2g'], row(p['rw_w0'][0]), row(p['rw_a0'][0]))
    r3 = lambda a: a.reshape(B, T, D)
    xo, s_new = _rwkv_scan(rkv.reshape(3, B, T, D), r3(lw), r3(ag), r3(g), rwS[0], p['rw_kk'][0],
                           p['rw_ka'][0], p['rw_rk'][0], p['rw_lnx_g'][0], p['rw_lnx_b'][0], t_real)
    xo = xo.reshape(M, D)
    h3 = _proj_ln([(xo, D, 0)], q['w_o'], h2, row(p['ln1_g'][1]), row(p['ln1_b'][1]))
    h4 = _mlp_ln(h3, q['w_up'][1], q['w_down'][1], row(p['ln2_g'][1]), row(p['ln2_b'][1]))
    shift_new = h2_3[:, t_real - 1]

    new_states = (c_new[None], n_new[None], m_new[None], h_last[None], conv_new[None],
                  s_new[None], shift_new[None])
    return h4.reshape(B, T, D), new_states


def kernel(x_prompt, x_sample, state_mlstm_C, state_mlstm_n, state_mlstm_m, state_rglru_h, state_rglru_conv, state_rwkv_S, state_rwkv_shift, meta_tokens, w_in_ab, b_if_ab, mlstm_norm_g, rg_conv_w, rg_conv_b, rg_wa, rg_ba, rg_wx, rg_bx, rg_lambda, w_out_ab, rw_mu, rw_wr, rw_wk, rw_wv, rw_wo, rw_w0, rw_w1, rw_w2, rw_a0, rw_a1, rw_a2, rw_g1, rw_g2, rw_kk, rw_ka, rw_rk, rw_lnx_g, rw_lnx_b, ln1_g, ln1_b, ln2_g, ln2_b, w_up, w_down):
    p = dict(w_in_ab=w_in_ab, b_if_ab=b_if_ab, mlstm_norm_g=mlstm_norm_g, rg_conv_w=rg_conv_w,
             rg_conv_b=rg_conv_b, rg_wa=rg_wa, rg_ba=rg_ba, rg_wx=rg_wx, rg_bx=rg_bx,
             rg_lambda=rg_lambda, w_out_ab=w_out_ab, rw_mu=rw_mu, rw_wr=rw_wr, rw_wk=rw_wk,
             rw_wv=rw_wv, rw_wo=rw_wo, rw_w0=rw_w0, rw_w1=rw_w1, rw_w2=rw_w2, rw_a0=rw_a0,
             rw_a1=rw_a1, rw_a2=rw_a2, rw_g1=rw_g1, rw_g2=rw_g2, rw_kk=rw_kk, rw_ka=rw_ka,
             rw_rk=rw_rk, rw_lnx_g=rw_lnx_g, rw_lnx_b=rw_lnx_b, ln1_g=ln1_g, ln1_b=ln1_b,
             ln2_g=ln2_g, ln2_b=ln2_b, w_up=w_up, w_down=w_down)
    q = _prepare_params(p)
    B, S, D = x_prompt.shape
    dt = x_prompt.dtype
    t_prompt = N_META + S
    t_pad = -(-t_prompt // MLSTM_CHUNK) * MLSTM_CHUNK
    zero_states = (
        jnp.zeros((1, B, NH_A, HD_A, HD_A), f32),
        jnp.zeros((1, B, NH_A, HD_A), f32),
        jnp.zeros((1, B, NH_A), f32),
        jnp.zeros((1, B, RG_W), f32),
        jnp.zeros((1, B, CONV_W - 1, RG_W), dt),
        jnp.zeros((1, B, NH_C, HD_C, HD_C), f32),
        jnp.zeros((1, B, D), dt),
    )
    meta = jnp.broadcast_to(meta_tokens.astype(dt)[None], (B, N_META, D))
    tail = jnp.zeros((B, t_pad - t_prompt, D), dt)
    hp, ps = _trunk(jnp.concatenate([meta, x_prompt, tail], axis=1), zero_states, p, q, t_prompt)
    sample_states = (state_mlstm_C, state_mlstm_n, state_mlstm_m, state_rglru_h, state_rglru_conv,
                     state_rwkv_S, state_rwkv_shift)
    hs, ss = _trunk(x_sample, sample_states, p, q, x_sample.shape[1])
    return (hp[:, N_META:t_prompt], hs, *ps, *ss)
```

```python
import functools

import jax
import jax.numpy as jnp
from jax import lax
from jax.experimental import pallas as pl
from jax.experimental.pallas import tpu as pltpu

f32 = jnp.float32
bf16 = jnp.bfloat16
HI = lax.Precision.HIGHEST

D_MODEL = 2048
N_META = 16
MIX_A = 1024
NH_A = 4
HD_A = 256
RG_W = 1024
RG_BLOCKS = 8
RG_BW = 128
CONV_W = 4
RG_C = 8.0
HD_C = 64
NH_C = 32
D_FF = 8192
LN_EPS = 1e-5
GN_EPS = 64e-5
DEPTH = 2
ALPHA = (2.0 * DEPTH) ** 0.25

LANES = 128
SUBLANES = 8
VMEM_LIMIT_BYTES = 56 * 1024 * 1024

MLSTM_CHUNK = 128
RWKV_TILE = 64
LORA_PAD = 128
NEG_BIG = -1e30

NT_DIMS = (((1,), (1,)), ((), ()))
TN_DIMS = (((0,), (0,)), ((), ()))


def _cparams(*sem):
    return pltpu.CompilerParams(dimension_semantics=sem, vmem_limit_bytes=VMEM_LIMIT_BYTES)


def _pick_tile(n, cap):
    best = None
    for d in range(SUBLANES, min(n, cap) + 1, SUBLANES):
        if n % d == 0:
            best = d
    assert best is not None, (n, cap)
    return best


def _dot_bf16(a, b):
    return jnp.dot(a.astype(bf16), b.astype(bf16), preferred_element_type=f32)


def _dot_hi(a, b, dims=None):
    if dims is None:
        return jnp.dot(a, b, precision=HI, preferred_element_type=f32)
    return lax.dot_general(a, b, dims, precision=HI, preferred_element_type=f32)


def _split_hi_lo(x):
    hi = x.astype(bf16)
    return hi, (x - hi.astype(f32)).astype(bf16)


def _layer_norm_rows(t, g, b):
    mu = jnp.mean(t, axis=-1, keepdims=True)
    d = t - mu
    var = jnp.mean(d * d, axis=-1, keepdims=True)
    return d * lax.rsqrt(var + LN_EPS) * g + b


def _inproj_kernel(x_ref, w_ref, wg_ref, bg_ref, o_ref, g_ref, xb_ref):
    @pl.when(pl.program_id(1) == 0)
    def _():
        x = x_ref[...]
        x_hi, x_lo = _split_hi_lo(x)
        xb_ref[...] = x_hi
        pre = (jnp.dot(x_hi, wg_ref[0], preferred_element_type=f32)
               + (jnp.dot(x_hi, wg_ref[1], preferred_element_type=f32)
                  + jnp.dot(x_lo, wg_ref[0], preferred_element_type=f32))) + bg_ref[...]
        lane = lax.broadcasted_iota(jnp.int32, pre.shape, 1)
        g_ref[...] = jnp.where(lane < NH_A, pre,
                               jnp.where(lane < 2 * NH_A, jax.nn.log_sigmoid(pre), 0.0))

    o_ref[...] = jnp.dot(xb_ref[...], w_ref[...], preferred_element_type=f32)


def _inproj(x, w, w_g, b_g, tm_cap=1088, tn=1024):
    M, K = x.shape
    n_out = w.shape[1]
    tm = _pick_tile(M, tm_cap)
    return pl.pallas_call(
        _inproj_kernel,
        out_shape=[jax.ShapeDtypeStruct((M, n_out), f32), jax.ShapeDtypeStruct((M, LANES), f32)],
        grid=(M // tm, n_out // tn),
        in_specs=[pl.BlockSpec((tm, K), lambda i, j: (i, 0)),
                  pl.BlockSpec((K, tn), lambda i, j: (0, j)),
                  pl.BlockSpec((2, K, LANES), lambda i, j: (0, 0, 0)),
                  pl.BlockSpec((1, LANES), lambda i, j: (0, 0))],
        out_specs=[pl.BlockSpec((tm, tn), lambda i, j: (i, j)),
                   pl.BlockSpec((tm, LANES), lambda i, j: (i, 0))],
        scratch_shapes=[pltpu.VMEM((tm, K), bf16)],
        compiler_params=_cparams("parallel", "arbitrary"),
        name="inproj",
    )(x, w, w_g, b_g)


def _proj_ln_kernel(*refs, n_x):
    x_refs = refs[:n_x]
    w_ref, res_ref, g_ref, b_ref, o_ref = refs[n_x:]
    y = None
    k0 = 0
    for x_ref in x_refs:
        kc = x_ref.shape[1]
        part = _dot_bf16(x_ref[...], w_ref[k0:k0 + kc, :])
        y = part if y is None else y + part
        k0 += kc
    t = ALPHA * res_ref[...] + y
    o_ref[...] = _layer_norm_rows(t, g_ref[...], b_ref[...])


def _proj_ln(xs, w, res, g, b, tm_cap=256):
    M, N = res.shape
    tm = _pick_tile(M, tm_cap)
    in_specs = []
    args = []
    for arr, width, blk in xs:
        in_specs.append(pl.BlockSpec((tm, width), functools.partial(lambda i, blk: (i, blk), blk=blk)))
        args.append(arr)
    K = w.shape[0]
    in_specs += [pl.BlockSpec((K, N), lambda i: (0, 0)),
                 pl.BlockSpec((tm, N), lambda i: (i, 0)),
                 pl.BlockSpec((1, N), lambda i: (0, 0)),
                 pl.BlockSpec((1, N), lambda i: (0, 0))]
    return pl.pallas_call(
        functools.partial(_proj_ln_kernel, n_x=len(xs)),
        out_shape=jax.ShapeDtypeStruct((M, N), f32),
        grid=(M // tm,),
        in_specs=in_specs,
        out_specs=pl.BlockSpec((tm, N), lambda i: (i, 0)),
        compiler_params=_cparams("parallel"),
        name="proj_ln",
    )(*args, w, res, g, b)


def _mlp_ln_kernel(x_ref, wu_ref, wd_ref, g_ref, b_ref, o_ref, xb_ref):
    f = pl.program_id(1)

    @pl.when(f == 0)
    def _():
        xb_ref[...] = x_ref[...].astype(bf16)
        o_ref[...] = jnp.zeros_like(o_ref)

    a = jnp.dot(xb_ref[...], wu_ref[...], preferred_element_type=f32)
    a = jnp.square(jnp.maximum(a, 0.0))
    o_ref[...] += jnp.dot(a.astype(bf16), wd_ref[...], preferred_element_type=f32)

    @pl.when(f == pl.num_programs(1) - 1)
    def _():
        t = ALPHA * x_ref[...] + o_ref[...]
        o_ref[...] = _layer_norm_rows(t, g_ref[...], b_ref[...])


def _mlp_ln(x, w_up, w_down, layer, g, b, tm_cap=1088, tf=512):
    M, K = x.shape
    F = w_up.shape[2]
    tm = _pick_tile(M, tm_cap)
    return pl.pallas_call(
        _mlp_ln_kernel,
        out_shape=jax.ShapeDtypeStruct((M, K), f32),
        grid=(M // tm, F // tf),
        in_specs=[pl.BlockSpec((tm, K), lambda i, j: (i, 0), pipeline_mode=pl.Buffered(1)),
                  pl.BlockSpec((None, K, tf), lambda i, j: (layer, 0, j)),
                  pl.BlockSpec((None, tf, K), lambda i, j: (layer, j, 0)),
                  pl.BlockSpec((1, K), lambda i, j: (0, 0)),
                  pl.BlockSpec((1, K), lambda i, j: (0, 0))],
        out_specs=pl.BlockSpec((tm, K), lambda i, j: (i, 0)),
        scratch_shapes=[pltpu.VMEM((tm, K), bf16)],
        compiler_params=_cparams("parallel", "arbitrary"),
        name="mlp_ln",
    )(x, w_up, w_down, g, b)


def _pad_rows(x, rows):
    if x.shape[0] == rows:
        return x
    return jnp.concatenate([x, jnp.zeros((rows - x.shape[0],) + x.shape[1:], x.dtype)], axis=0)


def _mlstm_kernel(q_ref, k_ref, v_ref, o_ref, g_ref, c0_ref, n0_ref, m0_ref, ng_ref,
                  h_ref, c_ref, n_ref, m_ref, cs, ns, ms, *, t_real):
    L = MLSTM_CHUNK
    nb, lb = q_ref.shape[0], q_ref.shape[1]
    c = pl.program_id(1)

    @pl.when(c == 0)
    def _():
        cs[...] = c0_ref[...]
        ns[...] = n0_ref[...]
        ms[...] = m0_ref[...]

    nvalid = jnp.minimum(t_real - c * lb, lb)
    vrow = lax.broadcasted_iota(jnp.int32, (L, 1), 0) < nvalid
    vcol = lax.broadcasted_iota(jnp.int32, (1, L), 1) < nvalid
    rowL = lax.broadcasted_iota(jnp.int32, (L, L), 0)
    colL = lax.broadcasted_iota(jnp.int32, (L, L), 1)
    causal = rowL >= colL
    tril = causal.astype(f32)
    chains = [(b, h) for b in range(nb) for h in range(NH_A)]
    hs = lambda h: slice(h * HD_A, (h + 1) * HD_A)

    def ld(ref, b, h):
        return jnp.where(vrow, _pad_rows(ref[b, :, hs(h)], L), 0.0)

    q = [ld(q_ref, b, h) for b, h in chains]
    k = [ld(k_ref, b, h) * (HD_A ** -0.5) for b, h in chains]
    vb = [ld(v_ref, b, h).astype(bf16) for b, h in chains]
    qb = [x.astype(bf16) for x in q]
    qk = [lax.dot_general(qb[i], k[i].astype(bf16), NT_DIMS, preferred_element_type=f32)
          for i in range(len(chains))]

    gate = [jnp.where(vrow, _pad_rows(g_ref[b], L), 0.0) for b in range(nb)]
    fcum = [_dot_hi(tril, gate[b]) for b in range(nb)]
    gate_t = [gate[b].T for b in range(nb)]
    fcum_t = [fcum[b].T for b in range(nb)]

    m_prev, fc, fr, igc, igr, m_row, inter, s = [], [], [], [], [], [], [], []
    for i, (b, h) in enumerate(chains):
        fc.append(fcum[b][:, NH_A + h:NH_A + h + 1])
        fr.append(fcum_t[b][NH_A + h:NH_A + h + 1, :])
        igc.append(jnp.where(vrow, gate[b][:, h:h + 1], NEG_BIG))
        igr.append(jnp.where(vcol, gate_t[b][h:h + 1, :], NEG_BIG))
        m_prev.append(ms[b, h, 0:1, 0:1])
        logd = jnp.where(causal, fc[i] - fr[i] + igr[i], NEG_BIG)
        b_in = fc[i] + m_prev[i]
        m_row.append(jnp.maximum(b_in, jnp.max(logd, axis=1, keepdims=True)))
        inter.append(jnp.exp(b_in - m_row[i]))
        s.append(qk[i] * jnp.exp(logd - m_row[i]))

    cmat = [cs[b, h] for b, h in chains]
    nrow = [ns[b, h] for b, h in chains]
    sv = [jnp.dot(s[i].astype(bf16), vb[i], preferred_element_type=f32) for i in range(len(chains))]
    qc = [jnp.dot(qb[i], cmat[i].astype(bf16), preferred_element_type=f32) for i in range(len(chains))]
    for i, (b, h) in enumerate(chains):
        num = sv[i] + inter[i] * qc[i]
        den = jnp.sum(s[i], axis=1, keepdims=True) \
            + inter[i] * jnp.sum(q[i] * nrow[i], axis=1, keepdims=True)
        hh = num / jnp.maximum(jnp.abs(den), jnp.exp(-m_row[i]))
        mu = jnp.mean(hh, axis=1, keepdims=True)
        dv = hh - mu
        var = jnp.mean(dv * dv, axis=1, keepdims=True)
        hn = dv * lax.rsqrt(var + LN_EPS) * ng_ref[h]
        out = hn * jax.nn.sigmoid(ld(o_ref, b, h))
        h_ref[b, :, hs(h)] = out[0:lb]

    kw, m_new, dec = [], [], []
    for i in range(len(chains)):
        fl = fc[i][L - 1:L, :]
        m_new.append(jnp.maximum(fl + m_prev[i], jnp.max(fl - fr[i] + igr[i], axis=1, keepdims=True)))
        dec.append(jnp.exp(fl + m_prev[i] - m_new[i]))
        kw.append(k[i] * jnp.exp(fl - fc[i] + igc[i] - m_new[i]))
    kv = [lax.dot_general(kw[i].astype(bf16), vb[i], TN_DIMS, preferred_element_type=f32)
          for i in range(len(chains))]
    for i, (b, h) in enumerate(chains):
        cs[b, h] = dec[i] * cmat[i] + kv[i]
        ns[b, h] = dec[i] * nrow[i] + jnp.sum(kw[i], axis=0, keepdims=True)
        ms[b, h] = jnp.broadcast_to(m_new[i], (1, LANES))

    @pl.when(c == pl.num_programs(1) - 1)
    def _():
        c_ref[...] = cs[...]
        n_ref[...] = ns[...]
        m_ref[...] = ms[...]


def _mlstm(z, gates, c0, n0, m0, norm_g, t_real):
    B, T, _ = z.shape
    lb = min(MLSTM_CHUNK, T)
    nc = T // lb
    nb = 1 if nc > 1 else 2
    assert B % nb == 0 and T % lb == 0
    m0b = jnp.broadcast_to(m0[:, :, None, None], (B, NH_A, 1, LANES))
    zspec = lambda group: pl.BlockSpec((nb, lb, MIX_A), lambda b, c: (b, c, group))
    st_spec = lambda r, w: pl.BlockSpec((nb, NH_A, r, w), lambda b, c: (b, 0, 0, 0))
    st_shapes = [(HD_A, HD_A), (1, HD_A), (1, LANES)]
    outs = pl.pallas_call(
        functools.partial(_mlstm_kernel, t_real=t_real),
        out_shape=[jax.ShapeDtypeStruct((B, T, MIX_A), f32)]
        + [jax.ShapeDtypeStruct((B, NH_A) + s, f32) for s in st_shapes],
        grid=(B // nb, nc),
        in_specs=[zspec(0), zspec(1), zspec(2), zspec(3),
                  pl.BlockSpec((nb, lb, LANES), lambda b, c: (b, c, 0))]
        + [st_spec(*s) for s in st_shapes]
        + [pl.BlockSpec((NH_A, 1, HD_A), lambda b, c: (0, 0, 0))],
        out_specs=[pl.BlockSpec((nb, lb, MIX_A), lambda b, c: (b, c, 0))]
        + [st_spec(*s) for s in st_shapes],
        scratch_shapes=[pltpu.VMEM((nb, NH_A) + s, f32) for s in st_shapes],
        compiler_params=_cparams("parallel", "arbitrary"),
        name="mlstm",
    )(z, z, z, z, gates, c0, n0.reshape(B, NH_A, 1, HD_A), m0b, norm_g.reshape(NH_A, 1, HD_A))
    hm, c_new, n_new, m_new = outs
    return hm, c_new, n_new.reshape(B, NH_A, HD_A), m_new[:, :, 0, 0]


def _one_minus_exp(x):
    u = jnp.exp(x)
    um1 = u - 1.0
    safe = jnp.where(um1 == 0.0, -x, -(um1 * x) / jnp.log(jnp.where(um1 == 0.0, 0.5, u)))
    return jnp.where(x > -0.5, safe, 1.0 - u)


def _rglru_kernel(xr_ref, gr_ref, conv0_ref, h0_ref, cw_ref, cb_ref, wa_ref, ba_ref, wx_ref,
                  bx_ref, lam_ref, y_ref, hl_ref, xp, a_s, h_s, hc, *, t_real):
    nb, lr = xr_ref.shape[0], xr_ref.shape[1]
    rows = nb * lr
    ngroup = rows // SUBLANES
    c = pl.program_id(1)
    hist = SUBLANES - (CONV_W - 1)

    @pl.when(c == 0)
    def _():
        xp[:, 0:SUBLANES, :] = jnp.zeros((nb, SUBLANES, RG_W), f32)
        xp[:, hist:SUBLANES, :] = conv0_ref[...]
        hc[...] = h0_ref[...]

    xp[:, SUBLANES:SUBLANES + lr, :] = xr_ref[...]
    xc = cb_ref[...]
    for j in range(CONV_W):
        xc = xc + xp[:, hist + j:hist + j + lr, :] * cw_ref[j:j + 1, :]
    xp[:, hist:SUBLANES, :] = xp[:, hist + lr:SUBLANES + lr, :]
    xc = xc.reshape(rows, RG_W)

    xcb = xc.astype(bf16)
    ra = []
    rx = []
    for n in range(RG_BLOCKS):
        xn = xcb[:, n * RG_BW:(n + 1) * RG_BW]
        ra.append(jnp.dot(xn, wa_ref[n].astype(bf16), preferred_element_type=f32))
        rx.append(jnp.dot(xn, wx_ref[n].astype(bf16), preferred_element_type=f32))
    r = jax.nn.sigmoid(jnp.concatenate(ra, axis=1) + ba_ref[...])
    gi = jax.nn.sigmoid(jnp.concatenate(rx, axis=1) + bx_ref[...])
    log_a = -RG_C * r * jax.nn.softplus(-lam_ref[...])
    t_idx = c * lr + lax.broadcasted_iota(jnp.int32, (nb, lr, 1), 1).reshape(rows, 1)
    valid = t_idx < t_real
    a = jnp.where(valid, jnp.exp(log_a), 1.0)
    u = jnp.where(valid, jnp.sqrt(_one_minus_exp(2.0 * log_a)) * (gi * xc), 0.0)

    a = a.reshape(ngroup, SUBLANES, RG_W)
    u = u.reshape(ngroup, SUBLANES, RG_W)
    r8 = lax.broadcasted_iota(jnp.int32, (1, SUBLANES, 1), 1)
    d = 1
    while d < SUBLANES:
        inside = r8 >= d
        a_sh = jnp.where(inside, pltpu.roll(a, d, axis=1), 1.0)
        u_sh = jnp.where(inside, pltpu.roll(u, d, axis=1), 0.0)
        u = a * u_sh + u
        a = a * a_sh
        d *= 2

    if lr == SUBLANES:
        h = u + a * hc[...]
        h_last = h[:, SUBLANES - 1:SUBLANES, :]
        h = h.reshape(rows, RG_W)
    else:
        assert nb == 1
        a_s[...] = a.reshape(rows, RG_W)
        h_s[...] = u.reshape(rows, RG_W)

        def step(g, h_prev):
            i = pl.multiple_of(g * SUBLANES, SUBLANES)
            hg = h_s[pl.ds(i, SUBLANES), :] + a_s[pl.ds(i, SUBLANES), :] * h_prev
            h_s[pl.ds(i, SUBLANES), :] = hg
            return hg[SUBLANES - 1:SUBLANES, :]

        h_last = lax.fori_loop(0, ngroup, step, hc[0]).reshape(1, 1, RG_W)
        h = h_s[...]
    hc[...] = h_last
    y_ref[...] = (h * jax.nn.gelu(gr_ref[...].reshape(rows, RG_W))).reshape(nb, lr, RG_W)

    @pl.when(c == pl.num_programs(1) - 1)
    def _():
        hl_ref[...] = h_last


def _rglru(z, conv0, h0, conv_w, conv_b, wa, ba, wx, bx, lam, t_real, lr_cap=544, nb_short=16):
    B, T, _ = z.shape
    if T == SUBLANES:
        nb, lr = nb_short, T
    else:
        nb, lr = 1, _pick_tile(T, lr_cap)
    assert B % nb == 0
    row = lambda a: a.reshape(1, RG_W)
    full = lambda shp: pl.BlockSpec(shp, lambda b, c: (0,) * len(shp))
    y, h_last = pl.pallas_call(
        functools.partial(_rglru_kernel, t_real=t_real),
        out_shape=[jax.ShapeDtypeStruct((B, T, RG_W), f32),
                   jax.ShapeDtypeStruct((B, 1, RG_W), f32)],
        grid=(B // nb, T // lr),
        in_specs=[pl.BlockSpec((nb, lr, RG_W), lambda b, c: (b, c, 4)),
                  pl.BlockSpec((nb, lr, RG_W), lambda b, c: (b, c, 5)),
                  pl.BlockSpec((nb, CONV_W - 1, RG_W), lambda b, c: (b, 0, 0)),
                  pl.BlockSpec((nb, 1, RG_W), lambda b, c: (b, 0, 0)),
                  full((CONV_W, RG_W)), full((1, RG_W)),
                  full((RG_BLOCKS, RG_BW, RG_BW)), full((1, RG_W)),
                  full((RG_BLOCKS, RG_BW, RG_BW)), full((1, RG_W)), full((1, RG_W))],
        out_specs=[pl.BlockSpec((nb, lr, RG_W), lambda b, c: (b, c, 0)),
                   pl.BlockSpec((nb, 1, RG_W), lambda b, c: (b, 0, 0))],
        scratch_shapes=[pltpu.VMEM((nb, SUBLANES + lr, RG_W), f32), pltpu.VMEM((nb * lr, RG_W), f32),
                        pltpu.VMEM((nb * lr, RG_W), f32), pltpu.VMEM((nb, 1, RG_W), f32)],
        compiler_params=_cparams("parallel", "arbitrary"),
        name="rglru",
    )(z, z, conv0, h0.reshape(B, 1, RG_W), conv_w, row(conv_b), wa, row(ba), wx, row(bx), row(lam))
    return y, h_last.reshape(B, RG_W)


def _rwkv_rkv_kernel(x_ref, xp_ref, mu_ref, w_ref, o_ref, xs_ref):
    @pl.when(pl.program_id(2) == 0)
    def _():
        x = x_ref[...]
        xs_ref[...] = (x + (xp_ref[...] - x) * mu_ref[...]).astype(bf16)

    o_ref[...] = jnp.dot(xs_ref[...], w_ref[...], preferred_element_type=f32)


def _rwkv_rkv(x, xprev, mu3, w3, tm_cap=512, tn=1024):
    M, K = x.shape
    tm = _pick_tile(M, tm_cap)
    return pl.pallas_call(
        _rwkv_rkv_kernel,
        out_shape=jax.ShapeDtypeStruct((3, M, K), f32),
        grid=(M // tm, 3, K // tn),
        in_specs=[pl.BlockSpec((tm, K), lambda i, s, j: (i, 0)),
                  pl.BlockSpec((tm, K), lambda i, s, j: (i, 0)),
                  pl.BlockSpec((None, 1, K), lambda i, s, j: (s, 0, 0)),
                  pl.BlockSpec((None, K, tn), lambda i, s, j: (s, 0, j))],
        out_specs=pl.BlockSpec((None, tm, tn), lambda i, s, j: (s, i, j)),
        scratch_shapes=[pltpu.VMEM((tm, K), bf16)],
        compiler_params=_cparams("parallel", "arbitrary", "arbitrary"),
        name="rwkv_rkv",
    )(x, xprev, mu3, w3)


def _rwkv_lora_kernel(x_ref, xp_ref, mu_ref, w1_ref, w2w_ref, w2a_ref, w2g_ref, w0_ref, a0_ref,
                      lw_ref, a_ref, g_ref):
    x = x_ref[...]
    dx = xp_ref[...] - x
    xw = (x + dx * mu_ref[0]).astype(bf16)
    xa = (x + dx * mu_ref[1]).astype(bf16)
    xg = (x + dx * mu_ref[2]).astype(bf16)
    r = LORA_PAD
    hw = jnp.tanh(jnp.dot(xw, w1_ref[:, 0:r], preferred_element_type=f32))
    ha = jnp.dot(xa, w1_ref[:, r:2 * r], preferred_element_type=f32)
    hg = jax.nn.sigmoid(jnp.dot(xg, w1_ref[:, 2 * r:], preferred_element_type=f32))
    w = w0_ref[...] + jnp.dot(hw.astype(bf16), w2w_ref[...], preferred_element_type=f32)
    w = -jax.nn.softplus(-w) - 0.5
    lw_ref[...] = -jnp.exp(w)
    a = a0_ref[...] + jnp.dot(ha.astype(bf16), w2a_ref[...], preferred_element_type=f32)
    a_ref[...] = jax.nn.sigmoid(a)
    g_ref[...] = jnp.dot(hg.astype(bf16), w2g_ref[...], preferred_element_type=f32)


def _rwkv_lora(x, xprev, mu3, w1, w2w, w2a, w2g, w0, a0, tm_cap=256):
    M, K = x.shape
    tm = _pick_tile(M, tm_cap)
    full = lambda a: pl.BlockSpec(a.shape, lambda i: (0,) * a.ndim)
    row = pl.BlockSpec((tm, K), lambda i: (i, 0))
    return pl.pallas_call(
        _rwkv_lora_kernel,
        out_shape=[jax.ShapeDtypeStruct((M, K), f32)] * 3,
        grid=(M // tm,),
        in_specs=[row, row, full(mu3), full(w1), full(w2w), full(w2a), full(w2g), full(w0), full(a0)],
        out_specs=[row, row, row],
        compiler_params=_cparams("parallel"),
        name="rwkv_lora",
    )(x, xprev, mu3, w1, w2w, w2a, w2g, w0, a0)


def _dot_b2(a, b_exact):
    hi, lo = _split_hi_lo(a)
    return (jnp.dot(hi, b_exact, preferred_element_type=f32)
            + jnp.dot(lo, b_exact, preferred_element_type=f32))


def _head_sums(x, ones_bd, npair):
    rows = x.shape[0]
    xs = jnp.concatenate([x[:, g * LANES:(g + 1) * LANES] for g in range(npair)], axis=0)
    s = _dot_b2(xs, ones_bd)
    return jnp.concatenate([s[g * rows:(g + 1) * rows] for g in range(npair)], axis=1)


def _rwkv_scan_kernel(r_ref, k_ref, v_ref, lw_ref, ag_ref, g_ref, s0_ref, kk_ref, ka_ref, rk_ref,
                      lg_ref, lb_ref, y_ref, s_ref, sbd, *, t_real, seg, npair):
    R = RWKV_TILE
    R2 = 2 * R
    nseg = R // seg
    W = npair * LANES
    shift = seg.bit_length() - 1
    c = pl.program_id(2)
    mm = lambda a, b: jnp.dot(a.astype(bf16), b.astype(bf16), preferred_element_type=f32)
    mm_nt = lambda a, b: lax.dot_general(a.astype(bf16), b.astype(bf16), NT_DIMS, preferred_element_type=f32)
    mm_tn = lambda a, b: lax.dot_general(a.astype(bf16), b.astype(bf16), TN_DIMS, preferred_element_type=f32)

    @pl.when(c == 0)
    def _():
        sbd[...] = jnp.zeros(sbd.shape, f32)
        for gi in range(npair):
            for j in range(nseg):
                sbd[gi, j, 0:HD_C, 0:HD_C] = s0_ref[j, 2 * gi]
                sbd[gi, j, HD_C:R2, HD_C:R2] = s0_ref[j, 2 * gi + 1]

    row1 = lax.broadcasted_iota(jnp.int32, (R, 1), 0)
    valid = c * seg + (row1 & (seg - 1)) < t_real
    ld = lambda ref: jnp.where(valid, ref[...].reshape(R, W), 0.0)
    r = ld(r_ref)
    k = ld(k_ref)
    v = ld(v_ref)
    lw = ld(lw_ref)
    ag = ld(ag_ref)

    lane = lax.broadcasted_iota(jnp.int32, (1, LANES), 1)
    m0 = (lane < HD_C).astype(f32)
    m1 = 1.0 - m0
    row2 = lax.broadcasted_iota(jnp.int32, (R2, R2), 0)
    col2 = lax.broadcasted_iota(jnp.int32, (R2, R2), 1)
    ones_bd = ((row2 < HD_C) == (col2 < HD_C)).astype(bf16)
    same_blk = (row2 >> shift) == (col2 >> shift)
    strict = jnp.logical_and(same_blk, row2 > col2)
    incl = jnp.logical_and(same_blk, row2 >= col2)
    eye = (row2 == col2).astype(f32)

    kk = k * kk_ref[...]
    kk = kk / jnp.maximum(jnp.sqrt(_head_sums(kk * kk, ones_bd, npair)), 1e-12)
    kp = k * (1.0 + (ag - 1.0) * ka_ref[...])
    bv = kk * ag

    rowR = lax.broadcasted_iota(jnp.int32, (R, R), 0)
    colR = lax.broadcasted_iota(jnp.int32, (R, R), 1)
    tril = jnp.logical_and((rowR >> shift) == (colR >> shift), rowR >= colR).astype(bf16)
    lw_hi, lw_lo = _split_hi_lo(lw)
    cw = (jnp.dot(tril, lw_hi, preferred_element_type=f32)
          + jnp.dot(tril, lw_lo, preferred_element_type=f32))
    cwl = jnp.concatenate(
        [jnp.broadcast_to(cw[j * seg + seg - 1:(j + 1) * seg, :], (seg, W)) for j in range(nseg)], axis=0)
    p_in = jnp.exp(cw)
    p_inv = jnp.exp(-cw)
    p_tail = jnp.exp(cwl - cw)
    at = -kk * jnp.exp(cw - lw)
    rt = r * p_in
    kt = kp * p_inv
    bt = bv * p_inv
    kh = kp * p_tail
    bh = bv * p_tail
    p_last = jnp.exp(cwl)

    def stack(x, gi):
        xg = x[:, gi * LANES:(gi + 1) * LANES]
        return jnp.concatenate([xg * m0, xg * m1], axis=0)

    def pick(x_st, j):
        if nseg == 1:
            return x_st
        return jnp.concatenate([x_st[j * seg:(j + 1) * seg], x_st[R + j * seg:R + (j + 1) * seg]], axis=0)

    def unpick(parts):
        if nseg == 1:
            return parts[0]
        return jnp.concatenate([q[0:seg] for q in parts] + [q[seg:2 * seg] for q in parts], axis=0)

    n_sq = shift - 1
    pairs = range(npair)
    s_prev = [[sbd[gi, j] for j in range(nseg)] for gi in pairs]
    a_st = [stack(at, gi) for gi in pairs]
    r_st = [stack(rt, gi) for gi in pairs]
    v_st = [stack(v, gi) for gi in pairs]
    prod = [mm_nt(jnp.concatenate([a_st[gi], r_st[gi]], axis=0),
                  jnp.concatenate([stack(kt, gi), stack(bt, gi)], axis=0)) for gi in pairs]
    m_ak = [jnp.where(strict, prod[gi][0:R2, 0:R2], 0.0) for gi in pairs]
    a_rk = [jnp.where(incl, prod[gi][R2:2 * R2, 0:R2], 0.0) for gi in pairs]
    a_rb = [jnp.where(incl, prod[gi][R2:2 * R2, R2:2 * R2], 0.0) for gi in pairs]

    x = [jnp.where(strict, prod[gi][0:R2, R2:2 * R2], 0.0) for gi in pairs]
    tinv = [eye + x[gi] for gi in pairs]
    for _ in range(n_sq):
        x = [mm(x[gi], x[gi]) for gi in pairs]
        tinv = [tinv[gi] + mm(tinv[gi], x[gi]) for gi in pairs]

    ar_s = [[mm_nt(jnp.concatenate([pick(a_st[gi], j), pick(r_st[gi], j)], axis=0), s_prev[gi][j])
             for j in range(nseg)] for gi in pairs]
    mv = [mm(m_ak[gi], v_st[gi]) for gi in pairs]
    u_st = [mm(tinv[gi], unpick([q[0:2 * seg] for q in ar_s[gi]]) + mv[gi]) for gi in pairs]
    o_st = [unpick([q[2 * seg:4 * seg] for q in ar_s[gi]])
            + mm(jnp.concatenate([a_rk[gi], a_rb[gi]], axis=1),
                 jnp.concatenate([v_st[gi], u_st[gi]], axis=0)) for gi in pairs]
    for gi in pairs:
        bh_st = stack(bh, gi)
        kh_st = stack(kh, gi)
        for j in range(nseg):
            pl_row = p_last[j * seg:j * seg + 1, gi * LANES:(gi + 1) * LANES]
            upd = mm_tn(jnp.concatenate([pick(u_st[gi], j), pick(v_st[gi], j)], axis=0),
                        jnp.concatenate([pick(bh_st, j), pick(kh_st, j)], axis=0))
            sbd[gi, j] = s_prev[gi][j] * pl_row + upd

    y = jnp.concatenate([o[0:R] + o[R:R2] for o in o_st], axis=1)
    inv_n = 1.0 / HD_C
    sums = _head_sums(jnp.concatenate([y, r * kp * rk_ref[...]], axis=0), ones_bd, npair)
    mu = sums[0:R] * inv_n
    dy = y - mu
    var = _head_sums(dy * dy, ones_bd, npair) * inv_n
    yn = dy * lax.rsqrt(var + GN_EPS) * lg_ref[...] + lb_ref[...]
    out = (yn + sums[R:R2] * v) * g_ref[...].reshape(R, W)
    y_ref[...] = out.reshape(y_ref.shape)

    @pl.when(c == pl.num_programs(2) - 1)
    def _():
        for gi in range(npair):
            for j in range(nseg):
                s_ref[j, 2 * gi] = sbd[gi, j, 0:HD_C, 0:HD_C]
                s_ref[j, 2 * gi + 1] = sbd[gi, j, HD_C:R2, HD_C:R2]


def _rwkv_scan(rkv, lw, ag, g, s0, kk_p, ka_p, rk_p, lnx_g, lnx_b, t_real, npair=16):
    _, B, T, D = rkv.shape
    seg = min(RWKV_TILE, T)
    nseg = RWKV_TILE // seg
    nc = T // seg
    assert nseg == 1 or nc == 1
    assert B % nseg == 0 and T % seg == 0
    W = npair * LANES
    tok = pl.BlockSpec((nseg, seg, W), lambda b, p, c: (b, c, p))
    rkv_spec = lambda s: pl.BlockSpec((None, nseg, seg, W), lambda b, p, c: (s, b, c, p))
    prm = pl.BlockSpec((1, W), lambda b, p, c: (0, p))
    st = pl.BlockSpec((nseg, 2 * npair, HD_C, HD_C), lambda b, p, c: (b, p, 0, 0))
    row = lambda a: a.reshape(1, D)
    return pl.pallas_call(
        functools.partial(_rwkv_scan_kernel, t_real=t_real, seg=seg, npair=npair),
        out_shape=[jax.ShapeDtypeStruct((B, T, D), f32),
                   jax.ShapeDtypeStruct((B, NH_C, HD_C, HD_C), f32)],
        grid=(B // nseg, D // W, nc),
        in_specs=[rkv_spec(0), rkv_spec(1), rkv_spec(2), tok, tok, tok, st, prm, prm, prm, prm, prm],
        out_specs=[tok, st],
        scratch_shapes=[pltpu.VMEM((npair, nseg, 2 * RWKV_TILE, 2 * RWKV_TILE), f32)],
        compiler_params=_cparams("parallel", "parallel", "arbitrary"),
        name="rwkv_scan",
    )(rkv, rkv, rkv, lw, ag, g, s0, row(kk_p), row(ka_p), row(rk_p), row(lnx_g), row(lnx_b))


def _pad_lora(w1, w2):
    rank = w1.shape[1]
    return (jnp.pad(w1, ((0, 0), (0, LORA_PAD - rank))), jnp.pad(w2, ((0, LORA_PAD - rank), (0, 0))))


def _prepare_params(p):
    w_in = p['w_in_ab'][0]
    n_qkvo = 4 * MIX_A
    n_g = 2 * NH_A
    q = {}
    q['w_in'] = jnp.concatenate([w_in[:, :n_qkvo], w_in[:, n_qkvo + n_g:]], axis=1).astype(bf16)
    q['w_gate'] = jnp.stack(_split_hi_lo(jnp.pad(w_in[:, n_qkvo:n_qkvo + n_g], ((0, 0), (0, LANES - n_g)))))
    q['b_gate'] = jnp.pad(p['b_if_ab'][0], (0, LANES - n_g)).reshape(1, LANES)
    q['w_out_ab'] = p['w_out_ab'][0].astype(bf16)
    q['w_up'] = p['w_up'].astype(bf16)
    q['w_down'] = p['w_down'].astype(bf16)
    q['w_rkv'] = jnp.stack([p['rw_wr'][0], p['rw_wk'][0], p['rw_wv'][0]]).astype(bf16)
    mu = p['rw_mu'][0]
    q['mu_rkv'] = jnp.stack([mu[0], mu[2], mu[3]])[:, None, :]
    q['mu_lora'] = jnp.stack([mu[1], mu[4], mu[5]])[:, None, :]
    w1, w2w = _pad_lora(p['rw_w1'][0], p['rw_w2'][0])
    a1, w2a = _pad_lora(p['rw_a1'][0], p['rw_a2'][0])
    q['lora_w1'] = jnp.concatenate([w1, a1, p['rw_g1'][0]], axis=1).astype(bf16)
    q['lora_w2w'] = w2w.astype(bf16)
    q['lora_w2a'] = w2a.astype(bf16)
    q['lora_w2g'] = p['rw_g2'][0].astype(bf16)
    q['w_o'] = p['rw_wo'][0].astype(bf16)
    return q


def _trunk(h, states, p, q, t_real):
    mC, mn, mm, rgh, rgc, rwS, rwx = states
    B, T, D = h.shape
    M = B * T
    row = lambda a: a.reshape(1, -1)
    hf = h.reshape(M, D)

    z, gates = _inproj(hf, q['w_in'], q['w_gate'], q['b_gate'])
    z3 = z.reshape(B, T, -1)
    hm, c_new, n_new, m_new = _mlstm(z3, gates.reshape(B, T, LANES), mC[0], mn[0], mm[0],
                                     p['mlstm_norm_g'][0], t_real)
    yr, h_last = _rglru(z3, rgc[0], rgh[0], p['rg_conv_w'][0], p['rg_conv_b'][0], p['rg_wa'][0],
                        p['rg_ba'][0], p['rg_wx'][0], p['rg_bx'][0], p['rg_lambda'][0], t_real)
    assert t_real >= CONV_W - 1
    conv_new = z3[:, t_real - (CONV_W - 1):t_real, 4 * MIX_A:4 * MIX_A + RG_W]
    h1 = _proj_ln([(hm.reshape(M, MIX_A), MIX_A, 0), (yr.reshape(M, RG_W), RG_W, 0)],
                  q['w_out_ab'], hf, row(p['ln1_g'][0]), row(p['ln1_b'][0]))
    h2 = _mlp_ln(h1, q['w_up'], q['w_down'], 0, row(p['ln2_g'][0]), row(p['ln2_b'][0]))

    h2_3 = h2.reshape(B, T, D)
    xprev = jnp.concatenate([rwx[0].astype(f32)[:, None], h2_3[:, :-1]], axis=1).reshape(M, D)
    rkv = _rwkv_rkv(h2, xprev, q['mu_rkv'], q['w_rkv'])
    lw, ag, g = _rwkv_lora(h2, xprev, q['mu_lora'], q['lora_w1'], q['lora_w2w'], q['lora_w2a'],
                           q['lora_w2g'], row(p['rw_w0'][0]), row(p['rw_a0'][0]))
    r3 = lambda a: a.reshape(B, T, D)
    xo, s_new = _rwkv_scan(rkv.reshape(3, B, T, D), r3(lw), r3(ag), r3(g), rwS[0], p['rw_kk'][0],
                           p['rw_ka'][0], p['rw_rk'][0], p['rw_lnx_g'][0], p['rw_lnx_b'][0], t_real)
    xo = xo.reshape(M, D)
    h3 = _proj_ln([(xo, D, 0)], q['w_o'], h2, row(p['ln1_g'][1]), row(p['ln1_b'][1]))
    h4 = _mlp_ln(h3, q['w_up'], q['w_down'], 1, row(p['ln2_g'][1]), row(p['ln2_b'][1]))
    shift_new = h2_3[:, t_real - 1]

    new_states = (c_new[None], n_new[None], m_new[None], h_last[None], conv_new[None],
                  s_new[None], shift_new[None])
    return h4.reshape(B, T, D), new_states


def kernel(x_prompt, x_sample, state_mlstm_C, state_mlstm_n, state_mlstm_m, state_rglru_h, state_rglru_conv, state_rwkv_S, state_rwkv_shift, meta_tokens, w_in_ab, b_if_ab, mlstm_norm_g, rg_conv_w, rg_conv_b, rg_wa, rg_ba, rg_wx, rg_bx, rg_lambda, w_out_ab, rw_mu, rw_wr, rw_wk, rw_wv, rw_wo, rw_w0, rw_w1, rw_w2, rw_a0, rw_a1, rw_a2, rw_g1, rw_g2, rw_kk, rw_ka, rw_rk, rw_lnx_g, rw_lnx_b, ln1_g, ln1_b, ln2_g, ln2_b, w_up, w_down):
    p = dict(w_in_ab=w_in_ab, b_if_ab=b_if_ab, mlstm_norm_g=mlstm_norm_g, rg_conv_w=rg_conv_w,
             rg_conv_b=rg_conv_b, rg_wa=rg_wa, rg_ba=rg_ba, rg_wx=rg_wx, rg_bx=rg_bx,
             rg_lambda=rg_lambda, w_out_ab=w_out_ab, rw_mu=rw_mu, rw_wr=rw_wr, rw_wk=rw_wk,
             rw_wv=rw_wv, rw_wo=rw_wo, rw_w0=rw_w0, rw_w1=rw_w1, rw_w2=rw_w2, rw_a0=rw_a0,
             rw_a1=rw_a1, rw_a2=rw_a2, rw_g1=rw_g1, rw_g2=rw_g2, rw_kk=rw_kk, rw_ka=rw_ka,
             rw_rk=rw_rk, rw_lnx_g=rw_lnx_g, rw_lnx_b=rw_lnx_b, ln1_g=ln1_g, ln1_b=ln1_b,
             ln2_g=ln2_g, ln2_b=ln2_b, w_up=w_up, w_down=w_down)
    q = _prepare_params(p)
    B, S, D = x_prompt.shape
    dt = x_prompt.dtype
    t_prompt = N_META + S
    t_pad = -(-t_prompt // MLSTM_CHUNK) * MLSTM_CHUNK
    zero_states = (
        jnp.zeros((1, B, NH_A, HD_A, HD_A), f32),
        jnp.zeros((1, B, NH_A, HD_A), f32),
        jnp.zeros((1, B, NH_A), f32),
        jnp.zeros((1, B, RG_W), f32),
        jnp.zeros((1, B, CONV_W - 1, RG_W), dt),
        jnp.zeros((1, B, NH_C, HD_C, HD_C), f32),
        jnp.zeros((1, B, D), dt),
    )
    meta = jnp.broadcast_to(meta_tokens.astype(dt)[None], (B, N_META, D))
    tail = jnp.zeros((B, t_pad - t_prompt, D), dt)
    hp, ps = _trunk(jnp.concatenate([meta, x_prompt, tail], axis=1), zero_states, p, q, t_prompt)
    sample_states = (state_mlstm_C, state_mlstm_n, state_mlstm_m, state_rglru_h, state_rglru_conv,
                     state_rwkv_S, state_rwkv_shift)
    hs, ss = _trunk(x_sample, sample_states, p, q, x_sample.shape[1])
    return (hp[:, N_META:t_prompt], hs, *ps, *ss)
```

```python
import functools

import jax
import jax.numpy as jnp
from jax import lax
from jax.experimental import pallas as pl
from jax.experimental.pallas import tpu as pltpu

f32 = jnp.float32
bf16 = jnp.bfloat16
HI = lax.Precision.HIGHEST

D_MODEL = 2048
N_META = 16
MIX_A = 1024
NH_A = 4
HD_A = 256
RG_W = 1024
RG_BLOCKS = 8
RG_BW = 128
CONV_W = 4
RG_C = 8.0
HD_C = 64
NH_C = 32
D_FF = 8192
LN_EPS = 1e-5
GN_EPS = 64e-5
DEPTH = 2
ALPHA = (2.0 * DEPTH) ** 0.25

LANES = 128
SUBLANES = 8
VMEM_LIMIT_BYTES = 56 * 1024 * 1024

MLSTM_CHUNK = 128
RWKV_TILE = 64
LORA_PAD = 128
NEG_BIG = -1e30

NT_DIMS = (((1,), (1,)), ((), ()))
TN_DIMS = (((0,), (0,)), ((), ()))


def _cparams(*sem):
    return pltpu.CompilerParams(dimension_semantics=sem, vmem_limit_bytes=VMEM_LIMIT_BYTES)


def _pick_tile(n, cap):
    best = None
    for d in range(SUBLANES, min(n, cap) + 1, SUBLANES):
        if n % d == 0:
            best = d
    assert best is not None, (n, cap)
    return best


def _dot_bf16(a, b):
    return jnp.dot(a.astype(bf16), b.astype(bf16), preferred_element_type=f32)


def _dot_hi(a, b, dims=None):
    if dims is None:
        return jnp.dot(a, b, precision=HI, preferred_element_type=f32)
    return lax.dot_general(a, b, dims, precision=HI, preferred_element_type=f32)


def _split_hi_lo(x):
    hi = x.astype(bf16)
    return hi, (x - hi.astype(f32)).astype(bf16)


def _layer_norm_rows(t, g, b):
    mu = jnp.mean(t, axis=-1, keepdims=True)
    d = t - mu
    var = jnp.mean(d * d, axis=-1, keepdims=True)
    return d * lax.rsqrt(var + LN_EPS) * g + b


def _inproj_kernel(x_ref, w_ref, wg_ref, bg_ref, o_ref, g_ref, xb_ref):
    @pl.when(pl.program_id(1) == 0)
    def _():
        x = x_ref[...]
        x_hi, x_lo = _split_hi_lo(x)
        xb_ref[...] = x_hi
        nt = lambda a, b: lax.dot_general(a, b, NT_DIMS, preferred_element_type=f32)
        pre = (nt(x_hi, wg_ref[0]) + (nt(x_hi, wg_ref[1]) + nt(x_lo, wg_ref[0]))) + bg_ref[...]
        lane = lax.broadcasted_iota(jnp.int32, pre.shape, 1)
        g_ref[...] = jnp.where(lane < NH_A, pre,
                               jnp.where(lane < 2 * NH_A, jax.nn.log_sigmoid(pre), 0.0))

    o_ref[...] = lax.dot_general(xb_ref[...], w_ref[...], NT_DIMS, preferred_element_type=f32)


def _inproj(x, w, w_g, b_g, tm_cap=1088, tn=1024):
    M, K = x.shape
    n_out = w.shape[0]
    tm = _pick_tile(M, tm_cap)
    return pl.pallas_call(
        _inproj_kernel,
        out_shape=[jax.ShapeDtypeStruct((M, n_out), f32), jax.ShapeDtypeStruct((M, LANES), f32)],
        grid=(M // tm, n_out // tn),
        in_specs=[pl.BlockSpec((tm, K), lambda i, j: (i, 0)),
                  pl.BlockSpec((tn, K), lambda i, j: (j, 0)),
                  pl.BlockSpec((2, LANES, K), lambda i, j: (0, 0, 0)),
                  pl.BlockSpec((1, LANES), lambda i, j: (0, 0))],
        out_specs=[pl.BlockSpec((tm, tn), lambda i, j: (i, j)),
                   pl.BlockSpec((tm, LANES), lambda i, j: (i, 0))],
        scratch_shapes=[pltpu.VMEM((tm, K), bf16)],
        compiler_params=_cparams("parallel", "arbitrary"),
        name="inproj",
    )(x, w, w_g, b_g)


def _proj_ln_kernel(*refs, n_x):
    x_refs = refs[:n_x]
    w_ref, res_ref, g_ref, b_ref, o_ref = refs[n_x:]
    y = None
    k0 = 0
    for x_ref in x_refs:
        kc = x_ref.shape[1]
        part = _dot_bf16(x_ref[...], w_ref[k0:k0 + kc, :])
        y = part if y is None else y + part
        k0 += kc
    t = ALPHA * res_ref[...] + y
    o_ref[...] = _layer_norm_rows(t, g_ref[...], b_ref[...])


def _proj_ln(xs, w, res, g, b, tm_cap=256):
    M, N = res.shape
    tm = _pick_tile(M, tm_cap)
    in_specs = []
    args = []
    for arr, width, blk in xs:
        in_specs.append(pl.BlockSpec((tm, width), functools.partial(lambda i, blk: (i, blk), blk=blk)))
        args.append(arr)
    K = w.shape[0]
    in_specs += [pl.BlockSpec((K, N), lambda i: (0, 0)),
                 pl.BlockSpec((tm, N), lambda i: (i, 0)),
                 pl.BlockSpec((1, N), lambda i: (0, 0)),
                 pl.BlockSpec((1, N), lambda i: (0, 0))]
    return pl.pallas_call(
        functools.partial(_proj_ln_kernel, n_x=len(xs)),
        out_shape=jax.ShapeDtypeStruct((M, N), f32),
        grid=(M // tm,),
        in_specs=in_specs,
        out_specs=pl.BlockSpec((tm, N), lambda i: (i, 0)),
        compiler_params=_cparams("parallel"),
        name="proj_ln",
    )(*args, w, res, g, b)


def _mlp_ln_kernel(x_ref, wu_ref, wd_ref, g_ref, b_ref, o_ref, xb_ref):
    f = pl.program_id(1)

    @pl.when(f == 0)
    def _():
        xb_ref[...] = x_ref[...].astype(bf16)
        o_ref[...] = jnp.zeros_like(o_ref)

    a = jnp.dot(xb_ref[...], wu_ref[...], preferred_element_type=f32)
    a = jnp.square(jnp.maximum(a, 0.0))
    o_ref[...] += jnp.dot(a.astype(bf16), wd_ref[...], preferred_element_type=f32)

    @pl.when(f == pl.num_programs(1) - 1)
    def _():
        t = ALPHA * x_ref[...] + o_ref[...]
        o_ref[...] = _layer_norm_rows(t, g_ref[...], b_ref[...])


def _mlp_ln(x, w_up, w_down, layer, g, b, tm_cap=1088, tf=512):
    M, K = x.shape
    F = w_up.shape[2]
    tm = _pick_tile(M, tm_cap)
    return pl.pallas_call(
        _mlp_ln_kernel,
        out_shape=jax.ShapeDtypeStruct((M, K), f32),
        grid=(M // tm, F // tf),
        in_specs=[pl.BlockSpec((tm, K), lambda i, j: (i, 0), pipeline_mode=pl.Buffered(1)),
                  pl.BlockSpec((None, K, tf), lambda i, j: (layer, 0, j)),
                  pl.BlockSpec((None, tf, K), lambda i, j: (layer, j, 0)),
                  pl.BlockSpec((1, K), lambda i, j: (0, 0)),
                  pl.BlockSpec((1, K), lambda i, j: (0, 0))],
        out_specs=pl.BlockSpec((tm, K), lambda i, j: (i, 0)),
        scratch_shapes=[pltpu.VMEM((tm, K), bf16)],
        compiler_params=_cparams("parallel", "arbitrary"),
        name="mlp_ln",
    )(x, w_up, w_down, g, b)


def _mlstm_kernel(q_ref, k_ref, v_ref, o_ref, g_ref, c0_ref, n0_ref, m0_ref, ng_ref,
                  h_ref, c_ref, n_ref, m_ref, *scratch, t_real, carry):
    nb, lb = q_ref.shape[0], q_ref.shape[1]
    R = nb * lb
    c = pl.program_id(1)
    if carry:
        cs, ns, ms = scratch

        @pl.when(c == 0)
        def _():
            cs[...] = c0_ref[...]
            ns[...] = n0_ref[...]
            ms[...] = m0_ref[...]
    else:
        cs, ns, ms = c0_ref, n0_ref, m0_ref

    heads = range(NH_A)
    seqs = range(nb)
    hs = lambda h: slice(h * HD_A, (h + 1) * HD_A)
    rows = lambda j: slice(j * lb, (j + 1) * lb)
    t_col = c * lb + lax.broadcasted_iota(jnp.int32, (nb, lb, 1), 1).reshape(R, 1)
    vrow = t_col < t_real
    rowi = lax.broadcasted_iota(jnp.int32, (R, R), 0)
    coli = lax.broadcasted_iota(jnp.int32, (R, R), 1)
    if nb == 1:
        causal = rowi >= coli
    else:
        shift = lb.bit_length() - 1
        causal = jnp.logical_and((rowi >> shift) == (coli >> shift), rowi >= coli)

    def per_seq(x):
        return jnp.broadcast_to(x, (nb, lb, x.shape[-1])).reshape(R, x.shape[-1])

    def last_row(x):
        return x.reshape(nb, lb, x.shape[-1])[:, lb - 1:lb, :]

    ld = lambda ref, h: jnp.where(vrow, ref[:, :, hs(h)].reshape(R, HD_A), 0.0)
    q = [ld(q_ref, h) for h in heads]
    k = [ld(k_ref, h) * (HD_A ** -0.5) for h in heads]
    vb = [ld(v_ref, h).astype(bf16) for h in heads]
    qb = [x.astype(bf16) for x in q]
    qk = [lax.dot_general(qb[h], k[h].astype(bf16), NT_DIMS, preferred_element_type=f32) for h in heads]

    gate = jnp.where(vrow, g_ref[...].reshape(R, LANES), 0.0)
    fcum = _dot_hi(causal.astype(f32), gate)
    gate_t = gate.T
    fcum_t = fcum.T
    vcol = (c * lb + (lax.broadcasted_iota(jnp.int32, (1, R), 1) & (lb - 1))) < t_real

    m_prev, fc, igc, fl, m_row, inter, s = [], [], [], [], [], [], []
    for h in heads:
        fc.append(fcum[:, NH_A + h:NH_A + h + 1])
        fr = fcum_t[NH_A + h:NH_A + h + 1, :]
        igc.append(jnp.where(vrow, gate[:, h:h + 1], NEG_BIG))
        igr = jnp.where(vcol, gate_t[h:h + 1, :], NEG_BIG)
        m_prev.append(ms[:, h, :, 0:1])
        logd = jnp.where(causal, fc[h] - fr + igr, NEG_BIG)
        b_in = fc[h] + per_seq(m_prev[h])
        m_row.append(jnp.maximum(b_in, jnp.max(logd, axis=1, keepdims=True)))
        inter.append(jnp.exp(b_in - m_row[h]))
        s.append(qk[h] * jnp.exp(logd - m_row[h]))
        fl.append(last_row(fc[h]))

    sv = [jnp.dot(s[h].astype(bf16), vb[h], preferred_element_type=f32) for h in heads]
    qc = [jnp.concatenate([jnp.dot(qb[h][rows(j)], cs[j, h].astype(bf16), preferred_element_type=f32)
                           for j in seqs], axis=0) for h in heads]
    for h in heads:
        num = sv[h] + inter[h] * qc[h]
        den = jnp.sum(s[h], axis=1, keepdims=True) \
            + inter[h] * jnp.sum(q[h] * per_seq(ns[:, h]), axis=1, keepdims=True)
        hh = num / jnp.maximum(jnp.abs(den), jnp.exp(-m_row[h]))
        mu = jnp.mean(hh, axis=1, keepdims=True)
        dv = hh - mu
        var = jnp.mean(dv * dv, axis=1, keepdims=True)
        hn = dv * lax.rsqrt(var + LN_EPS) * ng_ref[h]
        out = hn * jax.nn.sigmoid(ld(o_ref, h))
        h_ref[:, :, hs(h)] = out.reshape(nb, lb, HD_A)

    kw, m_new, dec = [], [], []
    for h in heads:
        wj = per_seq(fl[h]) - fc[h] + igc[h]
        wmax = jnp.max(wj.reshape(nb, lb, 1), axis=1, keepdims=True)
        m_new.append(jnp.maximum(fl[h] + m_prev[h], wmax))
        dec.append(jnp.exp(fl[h] + m_prev[h] - m_new[h]))
        kw.append(k[h] * jnp.exp(wj - per_seq(m_new[h])))
    kv = [[lax.dot_general(kw[h][rows(j)].astype(bf16), vb[h][rows(j)], TN_DIMS,
                           preferred_element_type=f32) for j in seqs] for h in heads]
    c_dst, n_dst, m_dst = (cs, ns, ms) if carry else (c_ref, n_ref, m_ref)
    for h in heads:
        n_new = dec[h] * ns[:, h] + jnp.sum(kw[h].reshape(nb, lb, HD_A), axis=1, keepdims=True)
        for j in seqs:
            c_dst[j, h] = dec[h][j] * cs[j, h] + kv[h][j]
        n_dst[:, h] = n_new
        m_dst[:, h] = jnp.broadcast_to(m_new[h], (nb, 1, LANES))

    if carry:
        @pl.when(c == pl.num_programs(1) - 1)
        def _():
            c_ref[...] = cs[...]
            n_ref[...] = ns[...]
            m_ref[...] = ms[...]


def _mlstm(z, gates, c0, n0, m0, norm_g, t_real, nb_short=8):
    B, T, _ = z.shape
    if T <= SUBLANES:
        nb, lb = nb_short, T
    else:
        nb, lb = 1, MLSTM_CHUNK
    nc = T // lb
    carry = nc > 1
    assert B % nb == 0 and T % lb == 0 and (carry or t_real == T)
    m0b = jnp.broadcast_to(m0[:, :, None, None], (B, NH_A, 1, LANES))
    zspec = lambda group: pl.BlockSpec((nb, lb, MIX_A), lambda b, c: (b, c, group))
    st_spec = lambda r, w: pl.BlockSpec((nb, NH_A, r, w), lambda b, c: (b, 0, 0, 0))
    st_shapes = [(HD_A, HD_A), (1, HD_A), (1, LANES)]
    outs = pl.pallas_call(
        functools.partial(_mlstm_kernel, t_real=t_real, carry=carry),
        out_shape=[jax.ShapeDtypeStruct((B, T, MIX_A), f32)]
        + [jax.ShapeDtypeStruct((B, NH_A) + s, f32) for s in st_shapes],
        grid=(B // nb, nc),
        in_specs=[zspec(0), zspec(1), zspec(2), zspec(3),
                  pl.BlockSpec((nb, lb, LANES), lambda b, c: (b, c, 0))]
        + [st_spec(*s) for s in st_shapes]
        + [pl.BlockSpec((NH_A, 1, HD_A), lambda b, c: (0, 0, 0))],
        out_specs=[pl.BlockSpec((nb, lb, MIX_A), lambda b, c: (b, c, 0))]
        + [st_spec(*s) for s in st_shapes],
        scratch_shapes=[pltpu.VMEM((nb, NH_A) + s, f32) for s in st_shapes] if carry else [],
        compiler_params=_cparams("parallel", "arbitrary"),
        name="mlstm",
    )(z, z, z, z, gates, c0, n0.reshape(B, NH_A, 1, HD_A), m0b, norm_g.reshape(NH_A, 1, HD_A))
    hm, c_new, n_new, m_new = outs
    return hm, c_new, n_new.reshape(B, NH_A, HD_A), m_new[:, :, 0, 0]


def _one_minus_exp(x):
    u = jnp.exp(x)
    um1 = u - 1.0
    safe = jnp.where(um1 == 0.0, -x, -(um1 * x) / jnp.log(jnp.where(um1 == 0.0, 0.5, u)))
    return jnp.where(x > -0.5, safe, 1.0 - u)


def _rglru_kernel(xr_ref, gr_ref, conv0_ref, h0_ref, cw_ref, cb_ref, wa_ref, ba_ref, wx_ref,
                  bx_ref, lam_ref, y_ref, hl_ref, xp, a_s, h_s, hc, *, t_real):
    nb, lr = xr_ref.shape[0], xr_ref.shape[1]
    rows = nb * lr
    ngroup = rows // SUBLANES
    c = pl.program_id(1)
    hist = SUBLANES - (CONV_W - 1)

    @pl.when(c == 0)
    def _():
        xp[:, 0:SUBLANES, :] = jnp.zeros((nb, SUBLANES, RG_W), f32)
        xp[:, hist:SUBLANES, :] = conv0_ref[...]
        hc[...] = h0_ref[...]

    xp[:, SUBLANES:SUBLANES + lr, :] = xr_ref[...]
    xc = cb_ref[...]
    for j in range(CONV_W):
        xc = xc + xp[:, hist + j:hist + j + lr, :] * cw_ref[j:j + 1, :]
    xp[:, hist:SUBLANES, :] = xp[:, hist + lr:SUBLANES + lr, :]
    xc = xc.reshape(rows, RG_W)

    xcb = xc.astype(bf16)
    ra = []
    rx = []
    for n in range(RG_BLOCKS):
        xn = xcb[:, n * RG_BW:(n + 1) * RG_BW]
        ra.append(jnp.dot(xn, wa_ref[n].astype(bf16), preferred_element_type=f32))
        rx.append(jnp.dot(xn, wx_ref[n].astype(bf16), preferred_element_type=f32))
    r = jax.nn.sigmoid(jnp.concatenate(ra, axis=1) + ba_ref[...])
    gi = jax.nn.sigmoid(jnp.concatenate(rx, axis=1) + bx_ref[...])
    log_a = -RG_C * r * jax.nn.softplus(-lam_ref[...])
    t_idx = c * lr + lax.broadcasted_iota(jnp.int32, (nb, lr, 1), 1).reshape(rows, 1)
    valid = t_idx < t_real
    a = jnp.where(valid, jnp.exp(log_a), 1.0)
    u = jnp.where(valid, jnp.sqrt(_one_minus_exp(2.0 * log_a)) * (gi * xc), 0.0)

    a = a.reshape(ngroup, SUBLANES, RG_W)
    u = u.reshape(ngroup, SUBLANES, RG_W)
    r8 = lax.broadcasted_iota(jnp.int32, (1, SUBLANES, 1), 1)
    d = 1
    while d < SUBLANES:
        inside = r8 >= d
        a_sh = jnp.where(inside, pltpu.roll(a, d, axis=1), 1.0)
        u_sh = jnp.where(inside, pltpu.roll(u, d, axis=1), 0.0)
        u = a * u_sh + u
        a = a * a_sh
        d *= 2

    if lr == SUBLANES:
        h = u + a * hc[...]
        h_last = h[:, SUBLANES - 1:SUBLANES, :]
        h = h.reshape(rows, RG_W)
    else:
        assert nb == 1
        a_s[...] = a.reshape(rows, RG_W)
        h_s[...] = u.reshape(rows, RG_W)

        def step(g, h_prev):
            i = pl.multiple_of(g * SUBLANES, SUBLANES)
            hg = h_s[pl.ds(i, SUBLANES), :] + a_s[pl.ds(i, SUBLANES), :] * h_prev
            h_s[pl.ds(i, SUBLANES), :] = hg
            return hg[SUBLANES - 1:SUBLANES, :]

        h_last = lax.fori_loop(0, ngroup, step, hc[0]).reshape(1, 1, RG_W)
        h = h_s[...]
    hc[...] = h_last
    y_ref[...] = (h * jax.nn.gelu(gr_ref[...].reshape(rows, RG_W))).reshape(nb, lr, RG_W)

    @pl.when(c == pl.num_programs(1) - 1)
    def _():
        hl_ref[...] = h_last


def _rglru(z, conv0, h0, conv_w, conv_b, wa, ba, wx, bx, lam, t_real, lr_cap=544, nb_short=16):
    B, T, _ = z.shape
    if T == SUBLANES:
        nb, lr = nb_short, T
    else:
        nb, lr = 1, _pick_tile(T, lr_cap)
    assert B % nb == 0
    row = lambda a: a.reshape(1, RG_W)
    full = lambda shp: pl.BlockSpec(shp, lambda b, c: (0,) * len(shp))
    y, h_last = pl.pallas_call(
        functools.partial(_rglru_kernel, t_real=t_real),
        out_shape=[jax.ShapeDtypeStruct((B, T, RG_W), f32),
                   jax.ShapeDtypeStruct((B, 1, RG_W), f32)],
        grid=(B // nb, T // lr),
        in_specs=[pl.BlockSpec((nb, lr, RG_W), lambda b, c: (b, c, 4)),
                  pl.BlockSpec((nb, lr, RG_W), lambda b, c: (b, c, 5)),
                  pl.BlockSpec((nb, CONV_W - 1, RG_W), lambda b, c: (b, 0, 0)),
                  pl.BlockSpec((nb, 1, RG_W), lambda b, c: (b, 0, 0)),
                  full((CONV_W, RG_W)), full((1, RG_W)),
                  full((RG_BLOCKS, RG_BW, RG_BW)), full((1, RG_W)),
                  full((RG_BLOCKS, RG_BW, RG_BW)), full((1, RG_W)), full((1, RG_W))],
        out_specs=[pl.BlockSpec((nb, lr, RG_W), lambda b, c: (b, c, 0)),
                   pl.BlockSpec((nb, 1, RG_W), lambda b, c: (b, 0, 0))],
        scratch_shapes=[pltpu.VMEM((nb, SUBLANES + lr, RG_W), f32), pltpu.VMEM((nb * lr, RG_W), f32),
                        pltpu.VMEM((nb * lr, RG_W), f32), pltpu.VMEM((nb, 1, RG_W), f32)],
        compiler_params=_cparams("parallel", "arbitrary"),
        name="rglru",
    )(z, z, conv0, h0.reshape(B, 1, RG_W), conv_w, row(conv_b), wa, row(ba), wx, row(bx), row(lam))
    return y, h_last.reshape(B, RG_W)


def _token_shift(x_ref, st_ref, xp_ref, c):
    tt = x_ref.shape[1]
    last = SUBLANES - 1

    @pl.when(c == 0)
    def _():
        xp_ref[:, last:SUBLANES, :] = st_ref[...]

    x = x_ref[...]
    xp_ref[:, SUBLANES:SUBLANES + tt, :] = x
    xprev = xp_ref[:, last:last + tt, :]
    xp_ref[:, last:SUBLANES, :] = xp_ref[:, last + tt:SUBLANES + tt, :]
    return x, xprev


def _seq_tile(B, T, rows_cap):
    if T <= SUBLANES:
        return _pick_tile(B * T, rows_cap) // T, T
    return 1, _pick_tile(T, rows_cap)


def _rwkv_rkv_kernel(x_ref, st_ref, mu_ref, w_ref, o_ref, xs_ref, xp_ref):
    c = pl.program_id(1)
    s = pl.program_id(2)

    @pl.when(jnp.logical_and(s == 0, pl.program_id(3) == 0))
    def _():
        x, xprev = _token_shift(x_ref, st_ref, xp_ref, c)
        dx = xprev - x
        for si in range(3):
            xs_ref[si] = (x + dx * mu_ref[si]).reshape(xs_ref.shape[1:]).astype(bf16)

    o_ref[...] = jnp.dot(xs_ref[s], w_ref[...], preferred_element_type=f32)


def _rwkv_rkv(x, shift0, mu3, w3, rows_cap=544, tn=1024):
    B, T, K = x.shape
    nb, tt = _seq_tile(B, T, rows_cap)
    rows = nb * tt
    nc = T // tt
    return pl.pallas_call(
        _rwkv_rkv_kernel,
        out_shape=jax.ShapeDtypeStruct((3, B * T, K), f32),
        grid=(B // nb, nc, 3, K // tn),
        in_specs=[pl.BlockSpec((nb, tt, K), lambda b, c, s, j: (b, c, 0)),
                  pl.BlockSpec((nb, 1, K), lambda b, c, s, j: (b, 0, 0)),
                  pl.BlockSpec((3, 1, K), lambda b, c, s, j: (0, 0, 0)),
                  pl.BlockSpec((None, K, tn), lambda b, c, s, j: (s, 0, j))],
        out_specs=pl.BlockSpec((None, rows, tn), lambda b, c, s, j: (s, b * nc + c, j)),
        scratch_shapes=[pltpu.VMEM((3, rows, K), bf16), pltpu.VMEM((nb, SUBLANES + tt, K), f32)],
        compiler_params=_cparams("parallel", "arbitrary", "arbitrary", "arbitrary"),
        name="rwkv_rkv",
    )(x, shift0, mu3, w3)


def _rwkv_lora_kernel(x_ref, st_ref, mu_ref, w1_ref, w2w_ref, w2a_ref, w2g_ref, w0_ref, a0_ref,
                      lw_ref, a_ref, g_ref, xp_ref):
    x, xprev = _token_shift(x_ref, st_ref, xp_ref, pl.program_id(1))
    dx = xprev - x
    mix = lambda i: (x + dx * mu_ref[i]).reshape(lw_ref.shape).astype(bf16)
    r = LORA_PAD
    hw = jnp.tanh(jnp.dot(mix(0), w1_ref[:, 0:r], preferred_element_type=f32))
    ha = jnp.dot(mix(1), w1_ref[:, r:2 * r], preferred_element_type=f32)
    hg = jax.nn.sigmoid(jnp.dot(mix(2), w1_ref[:, 2 * r:], preferred_element_type=f32))
    w = w0_ref[...] + jnp.dot(hw.astype(bf16), w2w_ref[...], preferred_element_type=f32)
    w = -jax.nn.softplus(-w) - 0.5
    lw_ref[...] = -jnp.exp(w)
    a = a0_ref[...] + jnp.dot(ha.astype(bf16), w2a_ref[...], preferred_element_type=f32)
    a_ref[...] = jax.nn.sigmoid(a)
    g_ref[...] = jnp.dot(hg.astype(bf16), w2g_ref[...], preferred_element_type=f32)


def _rwkv_lora(x, shift0, mu3, w1, w2w, w2a, w2g, w0, a0, rows_cap=272):
    B, T, K = x.shape
    nb, tt = _seq_tile(B, T, rows_cap)
    rows = nb * tt
    nc = T // tt
    full = lambda a: pl.BlockSpec(a.shape, lambda b, c: (0,) * a.ndim)
    out = pl.BlockSpec((rows, K), lambda b, c: (b * nc + c, 0))
    return pl.pallas_call(
        _rwkv_lora_kernel,
        out_shape=[jax.ShapeDtypeStruct((B * T, K), f32)] * 3,
        grid=(B // nb, nc),
        in_specs=[pl.BlockSpec((nb, tt, K), lambda b, c: (b, c, 0)),
                  pl.BlockSpec((nb, 1, K), lambda b, c: (b, 0, 0)),
                  full(mu3), full(w1), full(w2w), full(w2a), full(w2g), full(w0), full(a0)],
        out_specs=[out, out, out],
        scratch_shapes=[pltpu.VMEM((nb, SUBLANES + tt, K), f32)],
        compiler_params=_cparams("parallel", "arbitrary"),
        name="rwkv_lora",
    )(x, shift0, mu3, w1, w2w, w2a, w2g, w0, a0)


def _head_sums(x, ones_bd, npair):
    rows = x.shape[0]
    xs = jnp.concatenate([x[:, g * LANES:(g + 1) * LANES] for g in range(npair)], axis=0)
    s = jnp.dot(xs.astype(bf16), ones_bd, preferred_element_type=f32)
    return jnp.concatenate([s[g * rows:(g + 1) * rows] for g in range(npair)], axis=1)


def _rwkv_scan_kernel(r_ref, k_ref, v_ref, lw_ref, ag_ref, g_ref, s0_ref, kk_ref, ka_ref, rk_ref,
                      lg_ref, lb_ref, y_ref, s_ref, sbd, *, t_real, seg, npair):
    R = RWKV_TILE
    R2 = 2 * R
    nseg = R // seg
    W = npair * LANES
    shift = seg.bit_length() - 1
    c = pl.program_id(2)
    mm = lambda a, b: jnp.dot(a.astype(bf16), b.astype(bf16), preferred_element_type=f32)
    mm_nt = lambda a, b: lax.dot_general(a.astype(bf16), b.astype(bf16), NT_DIMS, preferred_element_type=f32)
    mm_tn = lambda a, b: lax.dot_general(a.astype(bf16), b.astype(bf16), TN_DIMS, preferred_element_type=f32)

    @pl.when(c == 0)
    def _():
        sbd[...] = jnp.zeros(sbd.shape, f32)
        for gi in range(npair):
            for j in range(nseg):
                sbd[gi, j, 0:HD_C, 0:HD_C] = s0_ref[j, 2 * gi]
                sbd[gi, j, HD_C:R2, HD_C:R2] = s0_ref[j, 2 * gi + 1]

    row1 = lax.broadcasted_iota(jnp.int32, (R, 1), 0)
    valid = c * seg + (row1 & (seg - 1)) < t_real
    ld = lambda ref: jnp.where(valid, ref[...].reshape(R, W), 0.0)
    r = ld(r_ref)
    k = ld(k_ref)
    v = ld(v_ref)
    lw = ld(lw_ref)
    ag = ld(ag_ref)

    lane = lax.broadcasted_iota(jnp.int32, (1, LANES), 1)
    m0 = (lane < HD_C).astype(f32)
    m1 = 1.0 - m0
    row2 = lax.broadcasted_iota(jnp.int32, (R2, R2), 0)
    col2 = lax.broadcasted_iota(jnp.int32, (R2, R2), 1)
    ones_bd = ((row2 < HD_C) == (col2 < HD_C)).astype(bf16)
    same_blk = (row2 >> shift) == (col2 >> shift)
    strict = jnp.logical_and(same_blk, row2 > col2)
    incl = jnp.logical_and(same_blk, row2 >= col2)
    eye = (row2 == col2).astype(f32)

    kk = k * kk_ref[...]
    kk = kk / jnp.maximum(jnp.sqrt(_head_sums(kk * kk, ones_bd, npair)), 1e-12)
    kp = k * (1.0 + (ag - 1.0) * ka_ref[...])
    bv = kk * ag

    rowR = lax.broadcasted_iota(jnp.int32, (R, R), 0)
    colR = lax.broadcasted_iota(jnp.int32, (R, R), 1)
    tril = jnp.logical_and((rowR >> shift) == (colR >> shift), rowR >= colR).astype(bf16)
    lw_hi, lw_lo = _split_hi_lo(lw)
    cw = (jnp.dot(tril, lw_hi, preferred_element_type=f32)
          + jnp.dot(tril, lw_lo, preferred_element_type=f32))
    cwl = jnp.concatenate(
        [jnp.broadcast_to(cw[j * seg + seg - 1:(j + 1) * seg, :], (seg, W)) for j in range(nseg)], axis=0)
    p_in = jnp.exp(cw)
    p_inv = jnp.exp(-cw)
    p_tail = jnp.exp(cwl - cw)
    at = -kk * jnp.exp(cw - lw)
    rt = r * p_in
    kt = kp * p_inv
    bt = bv * p_inv
    kh = kp * p_tail
    bh = bv * p_tail
    p_last = jnp.exp(cwl)

    def stack(x, gi):
        xg = x[:, gi * LANES:(gi + 1) * LANES]
        return jnp.concatenate([xg * m0, xg * m1], axis=0)

    def pick(x_st, j):
        if nseg == 1:
            return x_st
        return jnp.concatenate([x_st[j * seg:(j + 1) * seg], x_st[R + j * seg:R + (j + 1) * seg]], axis=0)

    def unpick(parts):
        if nseg == 1:
            return parts[0]
        return jnp.concatenate([q[0:seg] for q in parts] + [q[seg:2 * seg] for q in parts], axis=0)

    n_sq = shift - 1
    pairs = range(npair)
    s_prev = [[sbd[gi, j] for j in range(nseg)] for gi in pairs]
    a_st = [stack(at, gi) for gi in pairs]
    r_st = [stack(rt, gi) for gi in pairs]
    v_st = [stack(v, gi) for gi in pairs]
    prod = [mm_nt(jnp.concatenate([a_st[gi], r_st[gi]], axis=0),
                  jnp.concatenate([stack(kt, gi), stack(bt, gi)], axis=0)) for gi in pairs]
    m_ak = [jnp.where(strict, prod[gi][0:R2, 0:R2], 0.0) for gi in pairs]
    a_rk = [jnp.where(incl, prod[gi][R2:2 * R2, 0:R2], 0.0) for gi in pairs]
    a_rb = [jnp.where(incl, prod[gi][R2:2 * R2, R2:2 * R2], 0.0) for gi in pairs]

    x = [jnp.where(strict, prod[gi][0:R2, R2:2 * R2], 0.0) for gi in pairs]
    tinv = [eye + x[gi] for gi in pairs]
    for _ in range(n_sq):
        x = [mm(x[gi], x[gi]) for gi in pairs]
        tinv = [tinv[gi] + mm(tinv[gi], x[gi]) for gi in pairs]

    ar_s = [[mm_nt(jnp.concatenate([pick(a_st[gi], j), pick(r_st[gi], j)], axis=0), s_prev[gi][j])
             for j in range(nseg)] for gi in pairs]
    mv = [mm(m_ak[gi], v_st[gi]) for gi in pairs]
    u_st = [mm(tinv[gi], unpick([q[0:2 * seg] for q in ar_s[gi]]) + mv[gi]) for gi in pairs]
    o_st = [unpick([q[2 * seg:4 * seg] for q in ar_s[gi]])
            + mm(jnp.concatenate([a_rk[gi], a_rb[gi]], axis=1),
                 jnp.concatenate([v_st[gi], u_st[gi]], axis=0)) for gi in pairs]
    for gi in pairs:
        bh_st = stack(bh, gi)
        kh_st = stack(kh, gi)
        for j in range(nseg):
            pl_row = p_last[j * seg:j * seg + 1, gi * LANES:(gi + 1) * LANES]
            upd = mm_tn(jnp.concatenate([pick(u_st[gi], j), pick(v_st[gi], j)], axis=0),
                        jnp.concatenate([pick(bh_st, j), pick(kh_st, j)], axis=0))
            sbd[gi, j] = s_prev[gi][j] * pl_row + upd

    y = jnp.concatenate([o[0:R] + o[R:R2] for o in o_st], axis=1)
    inv_n = 1.0 / HD_C
    sums = _head_sums(jnp.concatenate([y, r * kp * rk_ref[...]], axis=0), ones_bd, npair)
    mu = sums[0:R] * inv_n
    dy = y - mu
    var = _head_sums(dy * dy, ones_bd, npair) * inv_n
    yn = dy * lax.rsqrt(var + GN_EPS) * lg_ref[...] + lb_ref[...]
    out = (yn + sums[R:R2] * v) * g_ref[...].reshape(R, W)
    y_ref[...] = out.reshape(y_ref.shape)

    @pl.when(c == pl.num_programs(2) - 1)
    def _():
        for gi in range(npair):
            for j in range(nseg):
                s_ref[j, 2 * gi] = sbd[gi, j, 0:HD_C, 0:HD_C]
                s_ref[j, 2 * gi + 1] = sbd[gi, j, HD_C:R2, HD_C:R2]


def _rwkv_scan(rkv, lw, ag, g, s0, kk_p, ka_p, rk_p, lnx_g, lnx_b, t_real, npair=16):
    _, B, T, D = rkv.shape
    seg = min(RWKV_TILE, T)
    nseg = RWKV_TILE // seg
    nc = T // seg
    assert nseg == 1 or nc == 1
    assert B % nseg == 0 and T % seg == 0
    W = npair * LANES
    tok = pl.BlockSpec((nseg, seg, W), lambda b, p, c: (b, c, p))
    rkv_spec = lambda s: pl.BlockSpec((None, nseg, seg, W), lambda b, p, c: (s, b, c, p))
    prm = pl.BlockSpec((1, W), lambda b, p, c: (0, p))
    st = pl.BlockSpec((nseg, 2 * npair, HD_C, HD_C), lambda b, p, c: (b, p, 0, 0))
    row = lambda a: a.reshape(1, D)
    return pl.pallas_call(
        functools.partial(_rwkv_scan_kernel, t_real=t_real, seg=seg, npair=npair),
        out_shape=[jax.ShapeDtypeStruct((B, T, D), f32),
                   jax.ShapeDtypeStruct((B, NH_C, HD_C, HD_C), f32)],
        grid=(B // nseg, D // W, nc),
        in_specs=[rkv_spec(0), rkv_spec(1), rkv_spec(2), tok, tok, tok, st, prm, prm, prm, prm, prm],
        out_specs=[tok, st],
        scratch_shapes=[pltpu.VMEM((npair, nseg, 2 * RWKV_TILE, 2 * RWKV_TILE), f32)],
        compiler_params=_cparams("parallel", "parallel", "arbitrary"),
        name="rwkv_scan",
    )(rkv, rkv, rkv, lw, ag, g, s0, row(kk_p), row(ka_p), row(rk_p), row(lnx_g), row(lnx_b))


def _pad_lora(w1, w2):
    rank = w1.shape[1]
    return (jnp.pad(w1, ((0, 0), (0, LORA_PAD - rank))), jnp.pad(w2, ((0, LORA_PAD - rank), (0, 0))))


def _prepare_params(p):
    w_in_t = jnp.swapaxes(p['w_in_ab'][0], 0, 1)
    n_qkvo = 4 * MIX_A
    n_g = 2 * NH_A
    q = {}
    q['w_in'] = jnp.concatenate([w_in_t[:n_qkvo], w_in_t[n_qkvo + n_g:]], axis=0).astype(bf16)
    q['w_gate'] = jnp.stack(_split_hi_lo(jnp.pad(w_in_t[n_qkvo:n_qkvo + n_g], ((0, LANES - n_g), (0, 0)))))
    q['b_gate'] = jnp.pad(p['b_if_ab'][0], (0, LANES - n_g)).reshape(1, LANES)
    q['w_out_ab'] = p['w_out_ab'][0].astype(bf16)
    q['w_up'] = p['w_up'].astype(bf16)
    q['w_down'] = p['w_down'].astype(bf16)
    q['w_rkv'] = jnp.stack([p['rw_wr'][0], p['rw_wk'][0], p['rw_wv'][0]]).astype(bf16)
    mu = p['rw_mu'][0]
    q['mu_rkv'] = jnp.stack([mu[0], mu[2], mu[3]])[:, None, :]
    q['mu_lora'] = jnp.stack([mu[1], mu[4], mu[5]])[:, None, :]
    w1, w2w = _pad_lora(p['rw_w1'][0], p['rw_w2'][0])
    a1, w2a = _pad_lora(p['rw_a1'][0], p['rw_a2'][0])
    q['lora_w1'] = jnp.concatenate([w1, a1, p['rw_g1'][0]], axis=1).astype(bf16)
    q['lora_w2w'] = w2w.astype(bf16)
    q['lora_w2a'] = w2a.astype(bf16)
    q['lora_w2g'] = p['rw_g2'][0].astype(bf16)
    q['w_o'] = p['rw_wo'][0].astype(bf16)
    return q


def _trunk(h, states, p, q, t_real):
    mC, mn, mm, rgh, rgc, rwS, rwx = states
    B, T, D = h.shape
    M = B * T
    row = lambda a: a.reshape(1, -1)
    hf = h.reshape(M, D)

    z, gates = _inproj(hf, q['w_in'], q['w_gate'], q['b_gate'])
    z3 = z.reshape(B, T, -1)
    hm, c_new, n_new, m_new = _mlstm(z3, gates.reshape(B, T, LANES), mC[0], mn[0], mm[0],
                                     p['mlstm_norm_g'][0], t_real)
    yr, h_last = _rglru(z3, rgc[0], rgh[0], p['rg_conv_w'][0], p['rg_conv_b'][0], p['rg_wa'][0],
                        p['rg_ba'][0], p['rg_wx'][0], p['rg_bx'][0], p['rg_lambda'][0], t_real)
    assert t_real >= CONV_W - 1
    conv_new = z3[:, t_real - (CONV_W - 1):t_real, 4 * MIX_A:4 * MIX_A + RG_W]
    h1 = _proj_ln([(hm.reshape(M, MIX_A), MIX_A, 0), (yr.reshape(M, RG_W), RG_W, 0)],
                  q['w_out_ab'], hf, row(p['ln1_g'][0]), row(p['ln1_b'][0]))
    h2 = _mlp_ln(h1, q['w_up'], q['w_down'], 0, row(p['ln2_g'][0]), row(p['ln2_b'][0]))

    h2_3 = h2.reshape(B, T, D)
    shift0 = rwx[0].astype(f32)[:, None]
    rkv = _rwkv_rkv(h2_3, shift0, q['mu_rkv'], q['w_rkv'])
    lw, ag, g = _rwkv_lora(h2_3, shift0, q['mu_lora'], q['lora_w1'], q['lora_w2w'], q['lora_w2a'],
                           q['lora_w2g'], row(p['rw_w0'][0]), row(p['rw_a0'][0]))
    r3 = lambda a: a.reshape(B, T, D)
    xo, s_new = _rwkv_scan(rkv.reshape(3, B, T, D), r3(lw), r3(ag), r3(g), rwS[0], p['rw_kk'][0],
                           p['rw_ka'][0], p['rw_rk'][0], p['rw_lnx_g'][0], p['rw_lnx_b'][0], t_real)
    xo = xo.reshape(M, D)
    h3 = _proj_ln([(xo, D, 0)], q['w_o'], h2, row(p['ln1_g'][1]), row(p['ln1_b'][1]))
    h4 = _mlp_ln(h3, q['w_up'], q['w_down'], 1, row(p['ln2_g'][1]), row(p['ln2_b'][1]))
    shift_new = h2_3[:, t_real - 1]

    new_states = (c_new[None], n_new[None], m_new[None], h_last[None], conv_new[None],
                  s_new[None], shift_new[None])
    return h4.reshape(B, T, D), new_states


def kernel(x_prompt, x_sample, state_mlstm_C, state_mlstm_n, state_mlstm_m, state_rglru_h, state_rglru_conv, state_rwkv_S, state_rwkv_shift, meta_tokens, w_in_ab, b_if_ab, mlstm_norm_g, rg_conv_w, rg_conv_b, rg_wa, rg_ba, rg_wx, rg_bx, rg_lambda, w_out_ab, rw_mu, rw_wr, rw_wk, rw_wv, rw_wo, rw_w0, rw_w1, rw_w2, rw_a0, rw_a1, rw_a2, rw_g1, rw_g2, rw_kk, rw_ka, rw_rk, rw_lnx_g, rw_lnx_b, ln1_g, ln1_b, ln2_g, ln2_b, w_up, w_down):
    p = dict(w_in_ab=w_in_ab, b_if_ab=b_if_ab, mlstm_norm_g=mlstm_norm_g, rg_conv_w=rg_conv_w,
             rg_conv_b=rg_conv_b, rg_wa=rg_wa, rg_ba=rg_ba, rg_wx=rg_wx, rg_bx=rg_bx,
             rg_lambda=rg_lambda, w_out_ab=w_out_ab, rw_mu=rw_mu, rw_wr=rw_wr, rw_wk=rw_wk,
             rw_wv=rw_wv, rw_wo=rw_wo, rw_w0=rw_w0, rw_w1=rw_w1, rw_w2=rw_w2, rw_a0=rw_a0,
             rw_a1=rw_a1, rw_a2=rw_a2, rw_g1=rw_g1, rw_g2=rw_g2, rw_kk=rw_kk, rw_ka=rw_ka,
             rw_rk=rw_rk, rw_lnx_g=rw_lnx_g, rw_lnx_b=rw_lnx_b, ln1_g=ln1_g, ln1_b=ln1_b,
             ln2_g=ln2_g, ln2_b=ln2_b, w_up=w_up, w_down=w_down)
    q = _prepare_params(p)
    B, S, D = x_prompt.shape
    dt = x_prompt.dtype
    t_prompt = N_META + S
    t_pad = -(-t_prompt // MLSTM_CHUNK) * MLSTM_CHUNK
    zero_states = (
        jnp.zeros((1, B, NH_A, HD_A, HD_A), f32),
        jnp.zeros((1, B, NH_A, HD_A), f32),
        jnp.zeros((1, B, NH_A), f32),
        jnp.zeros((1, B, RG_W), f32),
        jnp.zeros((1, B, CONV_W - 1, RG_W), dt),
        jnp.zeros((1, B, NH_C, HD_C, HD_C), f32),
        jnp.zeros((1, B, D), dt),
    )
    meta = jnp.broadcast_to(meta_tokens.astype(dt)[None], (B, N_META, D))
    tail = jnp.zeros((B, t_pad - t_prompt, D), dt)
    hp, ps = _trunk(jnp.concatenate([meta, x_prompt, tail], axis=1), zero_states, p, q, t_prompt)
    sample_states = (state_mlstm_C, state_mlstm_n, state_mlstm_m, state_rglru_h, state_rglru_conv,
                     state_rwkv_S, state_rwkv_shift)
    hs, ss = _trunk(x_sample, sample_states, p, q, x_sample.shape[1])
    return (hp[:, N_META:t_prompt], hs, *ps, *ss)
```

```python
import functools

import jax
import jax.numpy as jnp
from jax import lax
from jax.experimental import pallas as pl
from jax.experimental.pallas import tpu as pltpu

f32 = jnp.float32
bf16 = jnp.bfloat16
HI = lax.Precision.HIGHEST

D_MODEL = 2048
N_META = 16
MIX_A = 1024
NH_A = 4
HD_A = 256
RG_W = 1024
RG_BLOCKS = 8
RG_BW = 128
CONV_W = 4
RG_C = 8.0
HD_C = 64
NH_C = 32
D_FF = 8192
LN_EPS = 1e-5
GN_EPS = 64e-5
DEPTH = 2
ALPHA = (2.0 * DEPTH) ** 0.25

LANES = 128
SUBLANES = 8
VMEM_LIMIT_BYTES = 56 * 1024 * 1024

MLSTM_CHUNK = 128
RWKV_TILE = 64
SHORT_SEQ_ROWS_CAP = 512
LORA_PAD = 128
NEG_BIG = -1e30

NT_DIMS = (((1,), (1,)), ((), ()))
TN_DIMS = (((0,), (0,)), ((), ()))


def _cparams(*sem):
    return pltpu.CompilerParams(dimension_semantics=sem, vmem_limit_bytes=VMEM_LIMIT_BYTES)


def _pick_tile(n, cap):
    best = None
    for d in range(SUBLANES, min(n, cap) + 1, SUBLANES):
        if n % d == 0:
            best = d
    assert best is not None, (n, cap)
    return best


def _dot_bf16(a, b):
    return jnp.dot(a.astype(bf16), b.astype(bf16), preferred_element_type=f32)


def _dot_hi(a, b, dims=None):
    if dims is None:
        return jnp.dot(a, b, precision=HI, preferred_element_type=f32)
    return lax.dot_general(a, b, dims, precision=HI, preferred_element_type=f32)


def _split_hi_lo(x):
    hi = x.astype(bf16)
    return hi, (x - hi.astype(f32)).astype(bf16)


def _layer_norm_rows(t, g, b):
    mu = jnp.mean(t, axis=-1, keepdims=True)
    d = t - mu
    var = jnp.mean(d * d, axis=-1, keepdims=True)
    return d * lax.rsqrt(var + LN_EPS) * g + b


def _inproj_kernel(x_ref, w_ref, wg_ref, bg_ref, o_ref, g_ref, xb_ref):
    @pl.when(pl.program_id(1) == 0)
    def _():
        x = x_ref[...]
        x_hi, x_lo = _split_hi_lo(x)
        xb_ref[...] = x_hi
        nt = lambda a, b: lax.dot_general(a, b, NT_DIMS, preferred_element_type=f32)
        pre = (nt(x_hi, wg_ref[0]) + (nt(x_hi, wg_ref[1]) + nt(x_lo, wg_ref[0]))) + bg_ref[...]
        lane = lax.broadcasted_iota(jnp.int32, pre.shape, 1)
        g_ref[...] = jnp.where(lane < NH_A, pre,
                               jnp.where(lane < 2 * NH_A, jax.nn.log_sigmoid(pre), 0.0))

    o_ref[...] = lax.dot_general(xb_ref[...], w_ref[...], NT_DIMS, preferred_element_type=f32)


def _inproj(x, w, w_g, b_g, tm_cap=1088, tn=1024):
    M, K = x.shape
    n_out = w.shape[0]
    tm = _pick_tile(M, tm_cap)
    return pl.pallas_call(
        _inproj_kernel,
        out_shape=[jax.ShapeDtypeStruct((M, n_out), f32), jax.ShapeDtypeStruct((M, LANES), f32)],
        grid=(M // tm, n_out // tn),
        in_specs=[pl.BlockSpec((tm, K), lambda i, j: (i, 0)),
                  pl.BlockSpec((tn, K), lambda i, j: (j, 0)),
                  pl.BlockSpec((2, LANES, K), lambda i, j: (0, 0, 0)),
                  pl.BlockSpec((1, LANES), lambda i, j: (0, 0))],
        out_specs=[pl.BlockSpec((tm, tn), lambda i, j: (i, j)),
                   pl.BlockSpec((tm, LANES), lambda i, j: (i, 0))],
        scratch_shapes=[pltpu.VMEM((tm, K), bf16)],
        compiler_params=_cparams("parallel", "arbitrary"),
        name="inproj",
    )(x, w, w_g, b_g)


def _proj_ln_kernel(*refs, n_x):
    x_refs = refs[:n_x]
    w_ref, res_ref, g_ref, b_ref, o_ref = refs[n_x:]
    y = None
    k0 = 0
    for x_ref in x_refs:
        kc = x_ref.shape[1]
        part = _dot_bf16(x_ref[...], w_ref[k0:k0 + kc, :])
        y = part if y is None else y + part
        k0 += kc
    t = ALPHA * res_ref[...] + y
    o_ref[...] = _layer_norm_rows(t, g_ref[...], b_ref[...])


def _proj_ln(xs, w, res, g, b, tm_cap=256):
    M, N = res.shape
    tm = _pick_tile(M, tm_cap)
    in_specs = []
    args = []
    for arr, width, blk in xs:
        in_specs.append(pl.BlockSpec((tm, width), functools.partial(lambda i, blk: (i, blk), blk=blk)))
        args.append(arr)
    K = w.shape[0]
    in_specs += [pl.BlockSpec((K, N), lambda i: (0, 0)),
                 pl.BlockSpec((tm, N), lambda i: (i, 0)),
                 pl.BlockSpec((1, N), lambda i: (0, 0)),
                 pl.BlockSpec((1, N), lambda i: (0, 0))]
    return pl.pallas_call(
        functools.partial(_proj_ln_kernel, n_x=len(xs)),
        out_shape=jax.ShapeDtypeStruct((M, N), f32),
        grid=(M // tm,),
        in_specs=in_specs,
        out_specs=pl.BlockSpec((tm, N), lambda i: (i, 0)),
        compiler_params=_cparams("parallel"),
        name="proj_ln",
    )(*args, w, res, g, b)


def _mlp_ln_kernel(x_ref, wu_ref, wd_ref, g_ref, b_ref, o_ref, xb_ref):
    f = pl.program_id(1)

    @pl.when(f == 0)
    def _():
        xb_ref[...] = x_ref[...].astype(bf16)
        o_ref[...] = jnp.zeros_like(o_ref)

    a = jnp.dot(xb_ref[...], wu_ref[...], preferred_element_type=f32)
    a = jnp.square(jnp.maximum(a, 0.0))
    o_ref[...] += jnp.dot(a.astype(bf16), wd_ref[...], preferred_element_type=f32)

    @pl.when(f == pl.num_programs(1) - 1)
    def _():
        t = ALPHA * x_ref[...] + o_ref[...]
        o_ref[...] = _layer_norm_rows(t, g_ref[...], b_ref[...])


def _mlp_ln(x, w_up, w_down, layer, g, b, tm_cap=1088, tf=512, window=None):
    M, K = x.shape
    F = w_up.shape[2]
    if window is None:
        tm = _pick_tile(M, tm_cap)
        x_spec = pl.BlockSpec((tm, K), lambda i, j: (i, 0), pipeline_mode=pl.Buffered(1))
    else:
        seq_rows, first, count = window
        assert first % SUBLANES == 0 and seq_rows % SUBLANES == 0
        tm = _pick_tile(count, tm_cap)
        per_seq = count // tm
        M = (M // seq_rows) * count
        x_spec = pl.BlockSpec(
            (pl.Element(tm), pl.Element(K)),
            lambda i, j: (pl.multiple_of((i // per_seq) * seq_rows + first + (i % per_seq) * tm, SUBLANES), 0),
            pipeline_mode=pl.Buffered(1))
    return pl.pallas_call(
        _mlp_ln_kernel,
        out_shape=jax.ShapeDtypeStruct((M, K), f32),
        grid=(M // tm, F // tf),
        in_specs=[x_spec,
                  pl.BlockSpec((None, K, tf), lambda i, j: (layer, 0, j)),
                  pl.BlockSpec((None, tf, K), lambda i, j: (layer, j, 0)),
                  pl.BlockSpec((1, K), lambda i, j: (0, 0)),
                  pl.BlockSpec((1, K), lambda i, j: (0, 0))],
        out_specs=pl.BlockSpec((tm, K), lambda i, j: (i, 0)),
        scratch_shapes=[pltpu.VMEM((tm, K), bf16)],
        compiler_params=_cparams("parallel", "arbitrary"),
        name="mlp_ln",
    )(x, w_up, w_down, g, b)


def _mlstm_kernel(q_ref, k_ref, v_ref, o_ref, g_ref, c0_ref, n0_ref, m0_ref, ng_ref,
                  h_ref, c_ref, n_ref, m_ref, *scratch, t_real, carry):
    nb, lb = q_ref.shape[0], q_ref.shape[1]
    R = nb * lb
    c = pl.program_id(1)
    if carry:
        cs, ns, ms = scratch

        @pl.when(c == 0)
        def _():
            cs[...] = c0_ref[...]
            ns[...] = n0_ref[...]
            ms[...] = m0_ref[...]
    else:
        cs, ns, ms = c0_ref, n0_ref, m0_ref

    heads = range(NH_A)
    seqs = range(nb)
    hs = lambda h: slice(h * HD_A, (h + 1) * HD_A)
    rows = lambda j: slice(j * lb, (j + 1) * lb)
    t_col = c * lb + lax.broadcasted_iota(jnp.int32, (nb, lb, 1), 1).reshape(R, 1)
    vrow = t_col < t_real
    rowi = lax.broadcasted_iota(jnp.int32, (R, R), 0)
    coli = lax.broadcasted_iota(jnp.int32, (R, R), 1)
    if nb == 1:
        causal = rowi >= coli
    else:
        shift = lb.bit_length() - 1
        causal = jnp.logical_and((rowi >> shift) == (coli >> shift), rowi >= coli)

    def per_seq(x):
        return jnp.broadcast_to(x, (nb, lb, x.shape[-1])).reshape(R, x.shape[-1])

    def last_row(x):
        return x.reshape(nb, lb, x.shape[-1])[:, lb - 1:lb, :]

    ld = lambda ref, h: jnp.where(vrow, ref[:, :, hs(h)].reshape(R, HD_A), 0.0)
    q = [ld(q_ref, h) for h in heads]
    k = [ld(k_ref, h) * (HD_A ** -0.5) for h in heads]
    vb = [ld(v_ref, h).astype(bf16) for h in heads]
    qb = [x.astype(bf16) for x in q]
    qk = [lax.dot_general(qb[h], k[h].astype(bf16), NT_DIMS, preferred_element_type=f32) for h in heads]

    gate = jnp.where(vrow, g_ref[...].reshape(R, LANES), 0.0)
    fcum = _dot_hi(causal.astype(f32), gate)
    gate_t = gate.T
    fcum_t = fcum.T
    vcol = (c * lb + (lax.broadcasted_iota(jnp.int32, (1, R), 1) & (lb - 1))) < t_real

    m_prev, fc, igc, fl, m_row, inter, s = [], [], [], [], [], [], []
    for h in heads:
        fc.append(fcum[:, NH_A + h:NH_A + h + 1])
        fr = fcum_t[NH_A + h:NH_A + h + 1, :]
        igc.append(jnp.where(vrow, gate[:, h:h + 1], NEG_BIG))
        igr = jnp.where(vcol, gate_t[h:h + 1, :], NEG_BIG)
        m_prev.append(ms[:, h, :, 0:1])
        logd = jnp.where(causal, fc[h] - fr + igr, NEG_BIG)
        b_in = fc[h] + per_seq(m_prev[h])
        m_row.append(jnp.maximum(b_in, jnp.max(logd, axis=1, keepdims=True)))
        inter.append(jnp.exp(b_in - m_row[h]))
        s.append(qk[h] * jnp.exp(logd - m_row[h]))
        fl.append(last_row(fc[h]))

    sv = [jnp.dot(s[h].astype(bf16), vb[h], preferred_element_type=f32) for h in heads]
    qc = [jnp.concatenate([jnp.dot(qb[h][rows(j)], cs[j, h].astype(bf16), preferred_element_type=f32)
                           for j in seqs], axis=0) for h in heads]
    for h in heads:
        num = sv[h] + inter[h] * qc[h]
        den = jnp.sum(s[h], axis=1, keepdims=True) \
            + inter[h] * jnp.sum(q[h] * per_seq(ns[:, h]), axis=1, keepdims=True)
        hh = num / jnp.maximum(jnp.abs(den), jnp.exp(-m_row[h]))
        mu = jnp.mean(hh, axis=1, keepdims=True)
        dv = hh - mu
        var = jnp.mean(dv * dv, axis=1, keepdims=True)
        hn = dv * lax.rsqrt(var + LN_EPS) * ng_ref[h]
        out = hn * jax.nn.sigmoid(ld(o_ref, h))
        h_ref[:, :, hs(h)] = out.reshape(nb, lb, HD_A)

    kw, m_new, dec = [], [], []
    for h in heads:
        wj = per_seq(fl[h]) - fc[h] + igc[h]
        wmax = jnp.max(wj.reshape(nb, lb, 1), axis=1, keepdims=True)
        m_new.append(jnp.maximum(fl[h] + m_prev[h], wmax))
        dec.append(jnp.exp(fl[h] + m_prev[h] - m_new[h]))
        kw.append(k[h] * jnp.exp(wj - per_seq(m_new[h])))
    kv = [[lax.dot_general(kw[h][rows(j)].astype(bf16), vb[h][rows(j)], TN_DIMS,
                           preferred_element_type=f32) for j in seqs] for h in heads]
    c_dst, n_dst, m_dst = (cs, ns, ms) if carry else (c_ref, n_ref, m_ref)
    for h in heads:
        n_new = dec[h] * ns[:, h] + jnp.sum(kw[h].reshape(nb, lb, HD_A), axis=1, keepdims=True)
        for j in seqs:
            c_dst[j, h] = dec[h][j] * cs[j, h] + kv[h][j]
        n_dst[:, h] = n_new
        m_dst[:, h] = jnp.broadcast_to(m_new[h], (nb, 1, LANES))

    if carry:
        @pl.when(c == pl.num_programs(1) - 1)
        def _():
            c_ref[...] = cs[...]
            n_ref[...] = ns[...]
            m_ref[...] = ms[...]


def _mlstm(z, gates, c0, n0, m0, norm_g, t_real, nb_short=8):
    B, T, _ = z.shape
    if T <= SUBLANES:
        nb, lb = nb_short, T
    else:
        nb, lb = 1, MLSTM_CHUNK
    nc = T // lb
    carry = nc > 1
    assert B % nb == 0 and T % lb == 0 and (carry or t_real == T)
    m0b = jnp.broadcast_to(m0[:, :, None, None], (B, NH_A, 1, LANES))
    zspec = lambda group: pl.BlockSpec((nb, lb, MIX_A), lambda b, c: (b, c, group))
    st_spec = lambda r, w: pl.BlockSpec((nb, NH_A, r, w), lambda b, c: (b, 0, 0, 0))
    st_shapes = [(HD_A, HD_A), (1, HD_A), (1, LANES)]
    outs = pl.pallas_call(
        functools.partial(_mlstm_kernel, t_real=t_real, carry=carry),
        out_shape=[jax.ShapeDtypeStruct((B, T, MIX_A), f32)]
        + [jax.ShapeDtypeStruct((B, NH_A) + s, f32) for s in st_shapes],
        grid=(B // nb, nc),
        in_specs=[zspec(0), zspec(1), zspec(2), zspec(3),
                  pl.BlockSpec((nb, lb, LANES), lambda b, c: (b, c, 0))]
        + [st_spec(*s) for s in st_shapes]
        + [pl.BlockSpec((NH_A, 1, HD_A), lambda b, c: (0, 0, 0))],
        out_specs=[pl.BlockSpec((nb, lb, MIX_A), lambda b, c: (b, c, 0))]
        + [st_spec(*s) for s in st_shapes],
        scratch_shapes=[pltpu.VMEM((nb, NH_A) + s, f32) for s in st_shapes] if carry else [],
        compiler_params=_cparams("parallel", "arbitrary"),
        name="mlstm",
    )(z, z, z, z, gates, c0, n0.reshape(B, NH_A, 1, HD_A), m0b, norm_g.reshape(NH_A, 1, HD_A))
    hm, c_new, n_new, m_new = outs
    return hm, c_new, n_new.reshape(B, NH_A, HD_A), m_new[:, :, 0, 0]


def _one_minus_exp(x):
    u = jnp.exp(x)
    um1 = u - 1.0
    safe = jnp.where(um1 == 0.0, -x, -(um1 * x) / jnp.log(jnp.where(um1 == 0.0, 0.5, u)))
    return jnp.where(x > -0.5, safe, 1.0 - u)


def _rglru_kernel(xr_ref, gr_ref, conv0_ref, h0_ref, cw_ref, cb_ref, wa_ref, ba_ref, wx_ref,
                  bx_ref, lam_ref, y_ref, hl_ref, xp, a_s, h_s, hc, *, t_real):
    nb, lr = xr_ref.shape[0], xr_ref.shape[1]
    rows = nb * lr
    ngroup = rows // SUBLANES
    c = pl.program_id(1)
    hist = SUBLANES - (CONV_W - 1)

    @pl.when(c == 0)
    def _():
        xp[:, 0:SUBLANES, :] = jnp.zeros((nb, SUBLANES, RG_W), f32)
        xp[:, hist:SUBLANES, :] = conv0_ref[...]
        hc[...] = h0_ref[...]

    xp[:, SUBLANES:SUBLANES + lr, :] = xr_ref[...]
    xc = cb_ref[...]
    for j in range(CONV_W):
        xc = xc + xp[:, hist + j:hist + j + lr, :] * cw_ref[j:j + 1, :]
    xp[:, hist:SUBLANES, :] = xp[:, hist + lr:SUBLANES + lr, :]
    xc = xc.reshape(rows, RG_W)

    xcb = xc.astype(bf16)
    ra = []
    rx = []
    for n in range(RG_BLOCKS):
        xn = xcb[:, n * RG_BW:(n + 1) * RG_BW]
        ra.append(jnp.dot(xn, wa_ref[n].astype(bf16), preferred_element_type=f32))
        rx.append(jnp.dot(xn, wx_ref[n].astype(bf16), preferred_element_type=f32))
    r = jax.nn.sigmoid(jnp.concatenate(ra, axis=1) + ba_ref[...])
    gi = jax.nn.sigmoid(jnp.concatenate(rx, axis=1) + bx_ref[...])
    log_a = -RG_C * r * jax.nn.softplus(-lam_ref[...])
    t_idx = c * lr + lax.broadcasted_iota(jnp.int32, (nb, lr, 1), 1).reshape(rows, 1)
    valid = t_idx < t_real
    a = jnp.where(valid, jnp.exp(log_a), 1.0)
    u = jnp.where(valid, jnp.sqrt(_one_minus_exp(2.0 * log_a)) * (gi * xc), 0.0)

    a = a.reshape(ngroup, SUBLANES, RG_W)
    u = u.reshape(ngroup, SUBLANES, RG_W)
    r8 = lax.broadcasted_iota(jnp.int32, (1, SUBLANES, 1), 1)
    d = 1
    while d < SUBLANES:
        inside = r8 >= d
        a_sh = jnp.where(inside, pltpu.roll(a, d, axis=1), 1.0)
        u_sh = jnp.where(inside, pltpu.roll(u, d, axis=1), 0.0)
        u = a * u_sh + u
        a = a * a_sh
        d *= 2

    if lr == SUBLANES:
        h = u + a * hc[...]
        h_last = h[:, SUBLANES - 1:SUBLANES, :]
        h = h.reshape(rows, RG_W)
    else:
        assert nb == 1
        a_s[...] = a.reshape(rows, RG_W)
        h_s[...] = u.reshape(rows, RG_W)

        def step(g, h_prev):
            i = pl.multiple_of(g * SUBLANES, SUBLANES)
            hg = h_s[pl.ds(i, SUBLANES), :] + a_s[pl.ds(i, SUBLANES), :] * h_prev
            h_s[pl.ds(i, SUBLANES), :] = hg
            return hg[SUBLANES - 1:SUBLANES, :]

        h_last = lax.fori_loop(0, ngroup, step, hc[0]).reshape(1, 1, RG_W)
        h = h_s[...]
    hc[...] = h_last
    y_ref[...] = (h * jax.nn.gelu(gr_ref[...].reshape(rows, RG_W))).reshape(nb, lr, RG_W)

    @pl.when(c == pl.num_programs(1) - 1)
    def _():
        hl_ref[...] = h_last


def _rglru(z, conv0, h0, conv_w, conv_b, wa, ba, wx, bx, lam, t_real, lr_cap=544, nb_short=16):
    B, T, _ = z.shape
    if T == SUBLANES:
        nb, lr = nb_short, T
    else:
        nb, lr = 1, _pick_tile(T, lr_cap)
    assert B % nb == 0
    row = lambda a: a.reshape(1, RG_W)
    full = lambda shp: pl.BlockSpec(shp, lambda b, c: (0,) * len(shp))
    y, h_last = pl.pallas_call(
        functools.partial(_rglru_kernel, t_real=t_real),
        out_shape=[jax.ShapeDtypeStruct((B, T, RG_W), f32),
                   jax.ShapeDtypeStruct((B, 1, RG_W), f32)],
        grid=(B // nb, T // lr),
        in_specs=[pl.BlockSpec((nb, lr, RG_W), lambda b, c: (b, c, 4)),
                  pl.BlockSpec((nb, lr, RG_W), lambda b, c: (b, c, 5)),
                  pl.BlockSpec((nb, CONV_W - 1, RG_W), lambda b, c: (b, 0, 0)),
                  pl.BlockSpec((nb, 1, RG_W), lambda b, c: (b, 0, 0)),
                  full((CONV_W, RG_W)), full((1, RG_W)),
                  full((RG_BLOCKS, RG_BW, RG_BW)), full((1, RG_W)),
                  full((RG_BLOCKS, RG_BW, RG_BW)), full((1, RG_W)), full((1, RG_W))],
        out_specs=[pl.BlockSpec((nb, lr, RG_W), lambda b, c: (b, c, 0)),
                   pl.BlockSpec((nb, 1, RG_W), lambda b, c: (b, 0, 0))],
        scratch_shapes=[pltpu.VMEM((nb, SUBLANES + lr, RG_W), f32), pltpu.VMEM((nb * lr, RG_W), f32),
                        pltpu.VMEM((nb * lr, RG_W), f32), pltpu.VMEM((nb, 1, RG_W), f32)],
        compiler_params=_cparams("parallel", "arbitrary"),
        name="rglru",
    )(z, z, conv0, h0.reshape(B, 1, RG_W), conv_w, row(conv_b), wa, row(ba), wx, row(bx), row(lam))
    return y, h_last.reshape(B, RG_W)


def _token_shift(x_ref, st_ref, xp_ref, c):
    tt = x_ref.shape[1]
    last = SUBLANES - 1

    @pl.when(c == 0)
    def _():
        xp_ref[:, last:SUBLANES, :] = st_ref[...]

    x = x_ref[...]
    xp_ref[:, SUBLANES:SUBLANES + tt, :] = x
    xprev = xp_ref[:, last:last + tt, :]
    xp_ref[:, last:SUBLANES, :] = xp_ref[:, last + tt:SUBLANES + tt, :]
    return x, xprev


def _seq_tile(B, T, rows_cap):
    if T <= SUBLANES:
        return _pick_tile(B * T, min(rows_cap, SHORT_SEQ_ROWS_CAP)) // T, T
    return 1, _pick_tile(T, rows_cap)


def _rwkv_rkv_kernel(x_ref, st_ref, mu_ref, w_ref, o_ref, xs_ref, xp_ref):
    c = pl.program_id(1)
    s = pl.program_id(2)

    @pl.when(jnp.logical_and(s == 0, pl.program_id(3) == 0))
    def _():
        x, xprev = _token_shift(x_ref, st_ref, xp_ref, c)
        dx = xprev - x
        for si in range(3):
            xs_ref[si] = (x + dx * mu_ref[si]).reshape(xs_ref.shape[1:]).astype(bf16)

    o_ref[...] = jnp.dot(xs_ref[s], w_ref[...], preferred_element_type=f32)


def _rwkv_rkv(x, shift0, mu3, w3, rows_cap=1088, tn=512):
    B, T, K = x.shape
    nb, tt = _seq_tile(B, T, rows_cap)
    rows = nb * tt
    nc = T // tt
    return pl.pallas_call(
        _rwkv_rkv_kernel,
        out_shape=jax.ShapeDtypeStruct((3, B * T, K), f32),
        grid=(B // nb, nc, 3, K // tn),
        in_specs=[pl.BlockSpec((nb, tt, K), lambda b, c, s, j: (b, c, 0), pipeline_mode=pl.Buffered(1)),
                  pl.BlockSpec((nb, 1, K), lambda b, c, s, j: (b, 0, 0)),
                  pl.BlockSpec((3, 1, K), lambda b, c, s, j: (0, 0, 0)),
                  pl.BlockSpec((None, K, tn), lambda b, c, s, j: (s, 0, j))],
        out_specs=pl.BlockSpec((None, rows, tn), lambda b, c, s, j: (s, b * nc + c, j)),
        scratch_shapes=[pltpu.VMEM((3, rows, K), bf16), pltpu.VMEM((nb, SUBLANES + tt, K), f32)],
        compiler_params=_cparams("parallel", "arbitrary", "arbitrary", "arbitrary"),
        name="rwkv_rkv",
    )(x, shift0, mu3, w3)


def _rwkv_lora_kernel(x_ref, st_ref, mu_ref, w1_ref, w2w_ref, w2a_ref, w2g_ref, w0_ref, a0_ref,
                      lw_ref, a_ref, g_ref, xp_ref):
    x, xprev = _token_shift(x_ref, st_ref, xp_ref, pl.program_id(1))
    dx = xprev - x
    mix = lambda i: (x + dx * mu_ref[i]).reshape(lw_ref.shape).astype(bf16)
    r = LORA_PAD
    hw = jnp.tanh(jnp.dot(mix(0), w1_ref[:, 0:r], preferred_element_type=f32))
    ha = jnp.dot(mix(1), w1_ref[:, r:2 * r], preferred_element_type=f32)
    hg = jax.nn.sigmoid(jnp.dot(mix(2), w1_ref[:, 2 * r:], preferred_element_type=f32))
    w = w0_ref[...] + jnp.dot(hw.astype(bf16), w2w_ref[...], preferred_element_type=f32)
    w = -jax.nn.softplus(-w) - 0.5
    lw_ref[...] = -jnp.exp(w)
    a = a0_ref[...] + jnp.dot(ha.astype(bf16), w2a_ref[...], preferred_element_type=f32)
    a_ref[...] = jax.nn.sigmoid(a)
    g_ref[...] = jnp.dot(hg.astype(bf16), w2g_ref[...], preferred_element_type=f32)


def _rwkv_lora(x, shift0, mu3, w1, w2w, w2a, w2g, w0, a0, rows_cap=272):
    B, T, K = x.shape
    nb, tt = _seq_tile(B, T, rows_cap)
    rows = nb * tt
    nc = T // tt
    full = lambda a: pl.BlockSpec(a.shape, lambda b, c: (0,) * a.ndim)
    out = pl.BlockSpec((rows, K), lambda b, c: (b * nc + c, 0))
    return pl.pallas_call(
        _rwkv_lora_kernel,
        out_shape=[jax.ShapeDtypeStruct((B * T, K), f32)] * 3,
        grid=(B // nb, nc),
        in_specs=[pl.BlockSpec((nb, tt, K), lambda b, c: (b, c, 0)),
                  pl.BlockSpec((nb, 1, K), lambda b, c: (b, 0, 0)),
                  full(mu3), full(w1), full(w2w), full(w2a), full(w2g), full(w0), full(a0)],
        out_specs=[out, out, out],
        scratch_shapes=[pltpu.VMEM((nb, SUBLANES + tt, K), f32)],
        compiler_params=_cparams("parallel", "arbitrary"),
        name="rwkv_lora",
    )(x, shift0, mu3, w1, w2w, w2a, w2g, w0, a0)


def _head_sums(x, ones_bd, npair):
    rows = x.shape[0]
    xs = jnp.concatenate([x[:, g * LANES:(g + 1) * LANES] for g in range(npair)], axis=0)
    s = jnp.dot(xs.astype(bf16), ones_bd, preferred_element_type=f32)
    return jnp.concatenate([s[g * rows:(g + 1) * rows] for g in range(npair)], axis=1)


def _rwkv_scan_kernel(r_ref, k_ref, v_ref, lw_ref, ag_ref, g_ref, s0_ref, kk_ref, ka_ref, rk_ref,
                      lg_ref, lb_ref, y_ref, s_ref, sbd, *, t_real, seg, npair):
    R = RWKV_TILE
    R2 = 2 * R
    nseg = R // seg
    W = npair * LANES
    shift = seg.bit_length() - 1
    c = pl.program_id(2)
    mm = lambda a, b: jnp.dot(a.astype(bf16), b.astype(bf16), preferred_element_type=f32)
    mm_nt = lambda a, b: lax.dot_general(a.astype(bf16), b.astype(bf16), NT_DIMS, preferred_element_type=f32)
    mm_tn = lambda a, b: lax.dot_general(a.astype(bf16), b.astype(bf16), TN_DIMS, preferred_element_type=f32)

    @pl.when(c == 0)
    def _():
        sbd[...] = jnp.zeros(sbd.shape, f32)
        for gi in range(npair):
            for j in range(nseg):
                sbd[gi, j, 0:HD_C, 0:HD_C] = s0_ref[j, 2 * gi]
                sbd[gi, j, HD_C:R2, HD_C:R2] = s0_ref[j, 2 * gi + 1]

    row1 = lax.broadcasted_iota(jnp.int32, (R, 1), 0)
    valid = c * seg + (row1 & (seg - 1)) < t_real
    ld = lambda ref: jnp.where(valid, ref[...].reshape(R, W), 0.0)
    r = ld(r_ref)
    k = ld(k_ref)
    v = ld(v_ref)
    lw = ld(lw_ref)
    ag = ld(ag_ref)

    lane = lax.broadcasted_iota(jnp.int32, (1, LANES), 1)
    m0 = (lane < HD_C).astype(f32)
    m1 = 1.0 - m0
    row2 = lax.broadcasted_iota(jnp.int32, (R2, R2), 0)
    col2 = lax.broadcasted_iota(jnp.int32, (R2, R2), 1)
    ones_bd = ((row2 < HD_C) == (col2 < HD_C)).astype(bf16)
    same_blk = (row2 >> shift) == (col2 >> shift)
    strict = jnp.logical_and(same_blk, row2 > col2)
    incl = jnp.logical_and(same_blk, row2 >= col2)
    eye = (row2 == col2).astype(f32)

    kk = k * kk_ref[...]
    kk = kk / jnp.maximum(jnp.sqrt(_head_sums(kk * kk, ones_bd, npair)), 1e-12)
    kp = k * (1.0 + (ag - 1.0) * ka_ref[...])
    bv = kk * ag

    rowR = lax.broadcasted_iota(jnp.int32, (R, R), 0)
    colR = lax.broadcasted_iota(jnp.int32, (R, R), 1)
    tril = jnp.logical_and((rowR >> shift) == (colR >> shift), rowR >= colR).astype(bf16)
    lw_hi, lw_lo = _split_hi_lo(lw)
    cw = (jnp.dot(tril, lw_hi, preferred_element_type=f32)
          + jnp.dot(tril, lw_lo, preferred_element_type=f32))
    cwl = jnp.concatenate(
        [jnp.broadcast_to(cw[j * seg + seg - 1:(j + 1) * seg, :], (seg, W)) for j in range(nseg)], axis=0)
    p_in = jnp.exp(cw)
    p_inv = jnp.exp(-cw)
    p_tail = jnp.exp(cwl - cw)
    at = -kk * jnp.exp(cw - lw)
    rt = r * p_in
    kt = kp * p_inv
    bt = bv * p_inv
    kh = kp * p_tail
    bh = bv * p_tail
    p_last = jnp.exp(cwl)

    def stack(x, gi):
        xg = x[:, gi * LANES:(gi + 1) * LANES]
        return jnp.concatenate([xg * m0, xg * m1], axis=0)

    def pick(x_st, j):
        if nseg == 1:
            return x_st
        return jnp.concatenate([x_st[j * seg:(j + 1) * seg], x_st[R + j * seg:R + (j + 1) * seg]], axis=0)

    def unpick(parts):
        if nseg == 1:
            return parts[0]
        return jnp.concatenate([q[0:seg] for q in parts] + [q[seg:2 * seg] for q in parts], axis=0)

    n_sq = shift - 1
    pairs = range(npair)
    s_prev = [[sbd[gi, j] for j in range(nseg)] for gi in pairs]
    a_st = [stack(at, gi) for gi in pairs]
    r_st = [stack(rt, gi) for gi in pairs]
    v_st = [stack(v, gi) for gi in pairs]
    prod = [mm_nt(jnp.concatenate([a_st[gi], r_st[gi]], axis=0),
                  jnp.concatenate([stack(kt, gi), stack(bt, gi)], axis=0)) for gi in pairs]
    m_ak = [jnp.where(strict, prod[gi][0:R2, 0:R2], 0.0) for gi in pairs]
    a_rk = [jnp.where(incl, prod[gi][R2:2 * R2, 0:R2], 0.0) for gi in pairs]
    a_rb = [jnp.where(incl, prod[gi][R2:2 * R2, R2:2 * R2], 0.0) for gi in pairs]

    x = [jnp.where(strict, prod[gi][0:R2, R2:2 * R2], 0.0) for gi in pairs]
    tinv = [eye + x[gi] for gi in pairs]
    for _ in range(n_sq):
        x = [mm(x[gi], x[gi]) for gi in pairs]
        tinv = [tinv[gi] + mm(tinv[gi], x[gi]) for gi in pairs]

    ar_s = [[mm_nt(jnp.concatenate([pick(a_st[gi], j), pick(r_st[gi], j)], axis=0), s_prev[gi][j])
             for j in range(nseg)] for gi in pairs]
    mv = [mm(m_ak[gi], v_st[gi]) for gi in pairs]
    u_st = [mm(tinv[gi], unpick([q[0:2 * seg] for q in ar_s[gi]]) + mv[gi]) for gi in pairs]
    o_st = [unpick([q[2 * seg:4 * seg] for q in ar_s[gi]])
            + mm(jnp.concatenate([a_rk[gi], a_rb[gi]], axis=1),
                 jnp.concatenate([v_st[gi], u_st[gi]], axis=0)) for gi in pairs]
    for gi in pairs:
        bh_st = stack(bh, gi)
        kh_st = stack(kh, gi)
        for j in range(nseg):
            pl_row = p_last[j * seg:j * seg + 1, gi * LANES:(gi + 1) * LANES]
            upd = mm_tn(jnp.concatenate([pick(u_st[gi], j), pick(v_st[gi], j)], axis=0),
                        jnp.concatenate([pick(bh_st, j), pick(kh_st, j)], axis=0))
            sbd[gi, j] = s_prev[gi][j] * pl_row + upd

    y = jnp.concatenate([o[0:R] + o[R:R2] for o in o_st], axis=1)
    inv_n = 1.0 / HD_C
    sums = _head_sums(jnp.concatenate([y, r * kp * rk_ref[...]], axis=0), ones_bd, npair)
    mu = sums[0:R] * inv_n
    dy = y - mu
    var = _head_sums(dy * dy, ones_bd, npair) * inv_n
    yn = dy * lax.rsqrt(var + GN_EPS) * lg_ref[...] + lb_ref[...]
    out = (yn + sums[R:R2] * v) * g_ref[...].reshape(R, W)
    y_ref[...] = out.reshape(y_ref.shape)

    @pl.when(c == pl.num_programs(2) - 1)
    def _():
        for gi in range(npair):
            for j in range(nseg):
                s_ref[j, 2 * gi] = sbd[gi, j, 0:HD_C, 0:HD_C]
                s_ref[j, 2 * gi + 1] = sbd[gi, j, HD_C:R2, HD_C:R2]


def _rwkv_scan(rkv, lw, ag, g, s0, kk_p, ka_p, rk_p, lnx_g, lnx_b, t_real, npair=16):
    _, B, T, D = rkv.shape
    seg = min(RWKV_TILE, T)
    nseg = RWKV_TILE // seg
    nc = T // seg
    assert nseg == 1 or nc == 1
    assert B % nseg == 0 and T % seg == 0
    W = npair * LANES
    tok = pl.BlockSpec((nseg, seg, W), lambda b, p, c: (b, c, p))
    rkv_spec = lambda s: pl.BlockSpec((None, nseg, seg, W), lambda b, p, c: (s, b, c, p))
    prm = pl.BlockSpec((1, W), lambda b, p, c: (0, p))
    st = pl.BlockSpec((nseg, 2 * npair, HD_C, HD_C), lambda b, p, c: (b, p, 0, 0))
    row = lambda a: a.reshape(1, D)
    return pl.pallas_call(
        functools.partial(_rwkv_scan_kernel, t_real=t_real, seg=seg, npair=npair),
        out_shape=[jax.ShapeDtypeStruct((B, T, D), f32),
                   jax.ShapeDtypeStruct((B, NH_C, HD_C, HD_C), f32)],
        grid=(B // nseg, D // W, nc),
        in_specs=[rkv_spec(0), rkv_spec(1), rkv_spec(2), tok, tok, tok, st, prm, prm, prm, prm, prm],
        out_specs=[tok, st],
        scratch_shapes=[pltpu.VMEM((npair, nseg, 2 * RWKV_TILE, 2 * RWKV_TILE), f32)],
        compiler_params=_cparams("parallel", "parallel", "arbitrary"),
        name="rwkv_scan",
    )(rkv, rkv, rkv, lw, ag, g, s0, row(kk_p), row(ka_p), row(rk_p), row(lnx_g), row(lnx_b))


def _pad_lora(w1, w2):
    rank = w1.shape[1]
    return (jnp.pad(w1, ((0, 0), (0, LORA_PAD - rank))), jnp.pad(w2, ((0, LORA_PAD - rank), (0, 0))))


def _prepare_params(p):
    w_in_t = jnp.swapaxes(p['w_in_ab'][0], 0, 1)
    n_qkvo = 4 * MIX_A
    n_g = 2 * NH_A
    q = {}
    q['w_in'] = jnp.concatenate([w_in_t[:n_qkvo], w_in_t[n_qkvo + n_g:]], axis=0).astype(bf16)
    q['w_gate'] = jnp.stack(_split_hi_lo(jnp.pad(w_in_t[n_qkvo:n_qkvo + n_g], ((0, LANES - n_g), (0, 0)))))
    q['b_gate'] = jnp.pad(p['b_if_ab'][0], (0, LANES - n_g)).reshape(1, LANES)
    q['w_out_ab'] = p['w_out_ab'][0].astype(bf16)
    q['w_up'] = p['w_up'].astype(bf16)
    q['w_down'] = p['w_down'].astype(bf16)
    q['w_rkv'] = jnp.stack([p['rw_wr'][0], p['rw_wk'][0], p['rw_wv'][0]]).astype(bf16)
    mu = p['rw_mu'][0]
    q['mu_rkv'] = jnp.stack([mu[0], mu[2], mu[3]])[:, None, :]
    q['mu_lora'] = jnp.stack([mu[1], mu[4], mu[5]])[:, None, :]
    w1, w2w = _pad_lora(p['rw_w1'][0], p['rw_w2'][0])
    a1, w2a = _pad_lora(p['rw_a1'][0], p['rw_a2'][0])
    q['lora_w1'] = jnp.concatenate([w1, a1, p['rw_g1'][0]], axis=1).astype(bf16)
    q['lora_w2w'] = w2w.astype(bf16)
    q['lora_w2a'] = w2a.astype(bf16)
    q['lora_w2g'] = p['rw_g2'][0].astype(bf16)
    q['w_o'] = p['rw_wo'][0].astype(bf16)
    return q


def _trunk(h, states, p, q, t_real, out_from=0):
    mC, mn, mm, rgh, rgc, rwS, rwx = states
    B, T, D = h.shape
    M = B * T
    row = lambda a: a.reshape(1, -1)
    hf = h.reshape(M, D)

    z, gates = _inproj(hf, q['w_in'], q['w_gate'], q['b_gate'])
    z3 = z.reshape(B, T, -1)
    hm, c_new, n_new, m_new = _mlstm(z3, gates.reshape(B, T, LANES), mC[0], mn[0], mm[0],
                                     p['mlstm_norm_g'][0], t_real)
    yr, h_last = _rglru(z3, rgc[0], rgh[0], p['rg_conv_w'][0], p['rg_conv_b'][0], p['rg_wa'][0],
                        p['rg_ba'][0], p['rg_wx'][0], p['rg_bx'][0], p['rg_lambda'][0], t_real)
    assert t_real >= CONV_W - 1
    conv_new = z3[:, t_real - (CONV_W - 1):t_real, 4 * MIX_A:4 * MIX_A + RG_W]
    h1 = _proj_ln([(hm.reshape(M, MIX_A), MIX_A, 0), (yr.reshape(M, RG_W), RG_W, 0)],
                  q['w_out_ab'], hf, row(p['ln1_g'][0]), row(p['ln1_b'][0]))
    h2 = _mlp_ln(h1, q['w_up'], q['w_down'], 0, row(p['ln2_g'][0]), row(p['ln2_b'][0]))

    h2_3 = h2.reshape(B, T, D)
    shift0 = rwx[0].astype(f32)[:, None]
    rkv = _rwkv_rkv(h2_3, shift0, q['mu_rkv'], q['w_rkv'])
    lw, ag, g = _rwkv_lora(h2_3, shift0, q['mu_lora'], q['lora_w1'], q['lora_w2w'], q['lora_w2a'],
                           q['lora_w2g'], row(p['rw_w0'][0]), row(p['rw_a0'][0]))
    r3 = lambda a: a.reshape(B, T, D)
    xo, s_new = _rwkv_scan(rkv.reshape(3, B, T, D), r3(lw), r3(ag), r3(g), rwS[0], p['rw_kk'][0],
                           p['rw_ka'][0], p['rw_rk'][0], p['rw_lnx_g'][0], p['rw_lnx_b'][0], t_real)
    xo = xo.reshape(M, D)
    h3 = _proj_ln([(xo, D, 0)], q['w_o'], h2, row(p['ln1_g'][1]), row(p['ln1_b'][1]))
    n_keep = t_real - out_from
    h4 = _mlp_ln(h3, q['w_up'], q['w_down'], 1, row(p['ln2_g'][1]), row(p['ln2_b'][1]),
                 window=None if (out_from == 0 and t_real == T) else (T, out_from, n_keep))
    shift_new = h2_3[:, t_real - 1]

    new_states = (c_new[None], n_new[None], m_new[None], h_last[None], conv_new[None],
                  s_new[None], shift_new[None])
    return h4.reshape(B, n_keep, D), new_states


def kernel(x_prompt, x_sample, state_mlstm_C, state_mlstm_n, state_mlstm_m, state_rglru_h, state_rglru_conv, state_rwkv_S, state_rwkv_shift, meta_tokens, w_in_ab, b_if_ab, mlstm_norm_g, rg_conv_w, rg_conv_b, rg_wa, rg_ba, rg_wx, rg_bx, rg_lambda, w_out_ab, rw_mu, rw_wr, rw_wk, rw_wv, rw_wo, rw_w0, rw_w1, rw_w2, rw_a0, rw_a1, rw_a2, rw_g1, rw_g2, rw_kk, rw_ka, rw_rk, rw_lnx_g, rw_lnx_b, ln1_g, ln1_b, ln2_g, ln2_b, w_up, w_down):
    p = dict(w_in_ab=w_in_ab, b_if_ab=b_if_ab, mlstm_norm_g=mlstm_norm_g, rg_conv_w=rg_conv_w,
             rg_conv_b=rg_conv_b, rg_wa=rg_wa, rg_ba=rg_ba, rg_wx=rg_wx, rg_bx=rg_bx,
             rg_lambda=rg_lambda, w_out_ab=w_out_ab, rw_mu=rw_mu, rw_wr=rw_wr, rw_wk=rw_wk,
             rw_wv=rw_wv, rw_wo=rw_wo, rw_w0=rw_w0, rw_w1=rw_w1, rw_w2=rw_w2, rw_a0=rw_a0,
             rw_a1=rw_a1, rw_a2=rw_a2, rw_g1=rw_g1, rw_g2=rw_g2, rw_kk=rw_kk, rw_ka=rw_ka,
             rw_rk=rw_rk, rw_lnx_g=rw_lnx_g, rw_lnx_b=rw_lnx_b, ln1_g=ln1_g, ln1_b=ln1_b,
             ln2_g=ln2_g, ln2_b=ln2_b, w_up=w_up, w_down=w_down)
    q = _prepare_params(p)
    B, S, D = x_prompt.shape
    dt = x_prompt.dtype
    t_prompt = N_META + S
    t_pad = -(-t_prompt // MLSTM_CHUNK) * MLSTM_CHUNK
    zero_states = (
        jnp.zeros((1, B, NH_A, HD_A, HD_A), f32),
        jnp.zeros((1, B, NH_A, HD_A), f32),
        jnp.zeros((1, B, NH_A), f32),
        jnp.zeros((1, B, RG_W), f32),
        jnp.zeros((1, B, CONV_W - 1, RG_W), dt),
        jnp.zeros((1, B, NH_C, HD_C, HD_C), f32),
        jnp.zeros((1, B, D), dt),
    )
    meta = jnp.broadcast_to(meta_tokens.astype(dt)[None], (B, N_META, D))
    tail = jnp.zeros((B, t_pad - t_prompt, D), dt)
    hp, ps = _trunk(jnp.concatenate([meta, x_prompt, tail], axis=1), zero_states, p, q, t_prompt,
                    out_from=N_META)
    sample_states = (state_mlstm_C, state_mlstm_n, state_mlstm_m, state_rglru_h, state_rglru_conv,
                     state_rwkv_S, state_rwkv_shift)
    hs, ss = _trunk(x_sample, sample_states, p, q, x_sample.shape[1])
    return (hp, hs, *ps, *ss)
```

```python
import functools

import jax
import jax.numpy as jnp
from jax import lax
from jax.experimental import pallas as pl
from jax.experimental.pallas import tpu as pltpu

f32 = jnp.float32
bf16 = jnp.bfloat16
HI = lax.Precision.HIGHEST

D_MODEL = 2048
N_META = 16
MIX_A = 1024
NH_A = 4
HD_A = 256
RG_W = 1024
RG_BLOCKS = 8
RG_BW = 128
CONV_W = 4
RG_C = 8.0
HD_C = 64
NH_C = 32
D_FF = 8192
LN_EPS = 1e-5
GN_EPS = 64e-5
DEPTH = 2
ALPHA = (2.0 * DEPTH) ** 0.25

LANES = 128
SUBLANES = 8
VMEM_LIMIT_BYTES = 56 * 1024 * 1024

MLSTM_CHUNK = 128
RWKV_TILE = 64
SHORT_SEQ_ROWS_CAP = 512
LORA_PAD = 128
NEG_BIG = -1e30

NT_DIMS = (((1,), (1,)), ((), ()))
TN_DIMS = (((0,), (0,)), ((), ()))


def _cparams(*sem):
    return pltpu.CompilerParams(dimension_semantics=sem, vmem_limit_bytes=VMEM_LIMIT_BYTES)


def _pick_tile(n, cap):
    best = None
    for d in range(SUBLANES, min(n, cap) + 1, SUBLANES):
        if n % d == 0:
            best = d
    assert best is not None, (n, cap)
    return best


def _dot_bf16(a, b):
    return jnp.dot(a.astype(bf16), b.astype(bf16), preferred_element_type=f32)


def _dot_hi(a, b, dims=None):
    if dims is None:
        return jnp.dot(a, b, precision=HI, preferred_element_type=f32)
    return lax.dot_general(a, b, dims, precision=HI, preferred_element_type=f32)


def _split_hi_lo(x):
    hi = x.astype(bf16)
    return hi, (x - hi.astype(f32)).astype(bf16)


def _layer_norm_rows(t, g, b):
    mu = jnp.mean(t, axis=-1, keepdims=True)
    d = t - mu
    var = jnp.mean(d * d, axis=-1, keepdims=True)
    return d * lax.rsqrt(var + LN_EPS) * g + b


def _inproj_kernel(x_ref, w_ref, wg_ref, bg_ref, o1_ref, o2_ref, g_ref, xb_ref):
    @pl.when(pl.program_id(1) == 0)
    def _():
        x = x_ref[...]
        x_hi, x_lo = _split_hi_lo(x)
        xb_ref[...] = x_hi
        nt = lambda a, b: lax.dot_general(a, b, NT_DIMS, preferred_element_type=f32)
        pre = (nt(x_hi, wg_ref[0]) + (nt(x_hi, wg_ref[1]) + nt(x_lo, wg_ref[0]))) + bg_ref[...]
        lane = lax.broadcasted_iota(jnp.int32, pre.shape, 1)
        g_ref[...] = jnp.where(lane < NH_A, pre,
                               jnp.where(lane < 2 * NH_A, jax.nn.log_sigmoid(pre), 0.0))

    y = lax.dot_general(xb_ref[...], w_ref[...], NT_DIMS, preferred_element_type=f32)
    j = pl.program_id(1)
    n_first = pl.num_programs(1) // 2

    @pl.when(j < n_first)
    def _():
        o1_ref[...] = y.astype(o1_ref.dtype)

    @pl.when(j >= n_first)
    def _():
        o2_ref[...] = y


def _inproj(x, w, w_g, b_g, qkv_dtype, tm_cap=1088, tn=1024):
    M, K = x.shape
    n_out = w.shape[0]
    n_half = n_out // 2
    nt = n_half // tn
    tm = _pick_tile(M, tm_cap)
    return pl.pallas_call(
        _inproj_kernel,
        out_shape=[jax.ShapeDtypeStruct((M, n_half), qkv_dtype), jax.ShapeDtypeStruct((M, n_half), f32),
                   jax.ShapeDtypeStruct((M, LANES), f32)],
        grid=(M // tm, n_out // tn),
        in_specs=[pl.BlockSpec((tm, K), lambda i, j: (i, 0)),
                  pl.BlockSpec((tn, K), lambda i, j: (j, 0)),
                  pl.BlockSpec((2, LANES, K), lambda i, j: (0, 0, 0)),
                  pl.BlockSpec((1, LANES), lambda i, j: (0, 0))],
        out_specs=[pl.BlockSpec((tm, tn), lambda i, j: (i, jnp.minimum(j, nt - 1))),
                   pl.BlockSpec((tm, tn), lambda i, j: (i, jnp.maximum(j - nt, 0))),
                   pl.BlockSpec((tm, LANES), lambda i, j: (i, 0))],
        scratch_shapes=[pltpu.VMEM((tm, K), bf16)],
        compiler_params=_cparams("parallel", "arbitrary"),
        name="inproj",
    )(x, w, w_g, b_g)


def _proj_ln_kernel(*refs, n_x):
    x_refs = refs[:n_x]
    w_ref, res_ref, g_ref, b_ref, o_ref = refs[n_x:]
    y = None
    k0 = 0
    for x_ref in x_refs:
        kc = x_ref.shape[1]
        part = _dot_bf16(x_ref[...], w_ref[k0:k0 + kc, :])
        y = part if y is None else y + part
        k0 += kc
    t = ALPHA * res_ref[...] + y
    o_ref[...] = _layer_norm_rows(t, g_ref[...], b_ref[...])


def _proj_ln(xs, w, res, g, b, tm_cap=256):
    M, N = res.shape
    tm = _pick_tile(M, tm_cap)
    in_specs = []
    args = []
    for arr, width, blk in xs:
        in_specs.append(pl.BlockSpec((tm, width), functools.partial(lambda i, blk: (i, blk), blk=blk)))
        args.append(arr)
    K = w.shape[0]
    in_specs += [pl.BlockSpec((K, N), lambda i: (0, 0)),
                 pl.BlockSpec((tm, N), lambda i: (i, 0)),
                 pl.BlockSpec((1, N), lambda i: (0, 0)),
                 pl.BlockSpec((1, N), lambda i: (0, 0))]
    return pl.pallas_call(
        functools.partial(_proj_ln_kernel, n_x=len(xs)),
        out_shape=jax.ShapeDtypeStruct((M, N), f32),
        grid=(M // tm,),
        in_specs=in_specs,
        out_specs=pl.BlockSpec((tm, N), lambda i: (i, 0)),
        compiler_params=_cparams("parallel"),
        name="proj_ln",
    )(*args, w, res, g, b)


def _mlp_ln_kernel(x_ref, wu_ref, wd_ref, g_ref, b_ref, o_ref, xb_ref):
    f = pl.program_id(1)

    @pl.when(f == 0)
    def _():
        xb_ref[...] = x_ref[...].astype(bf16)
        o_ref[...] = jnp.zeros_like(o_ref)

    a = jnp.dot(xb_ref[...], wu_ref[...], preferred_element_type=f32)
    a = jnp.square(jnp.maximum(a, 0.0))
    o_ref[...] += jnp.dot(a.astype(bf16), wd_ref[...], preferred_element_type=f32)

    @pl.when(f == pl.num_programs(1) - 1)
    def _():
        t = ALPHA * x_ref[...] + o_ref[...]
        o_ref[...] = _layer_norm_rows(t, g_ref[...], b_ref[...])


def _mlp_ln(x, w_up, w_down, layer, g, b, tm_cap=1088, tf=512, window=None):
    M, K = x.shape
    F = w_up.shape[2]
    if window is None:
        tm = _pick_tile(M, tm_cap)
        x_spec = pl.BlockSpec((tm, K), lambda i, j: (i, 0), pipeline_mode=pl.Buffered(1))
    else:
        seq_rows, first, count = window
        assert first % SUBLANES == 0 and seq_rows % SUBLANES == 0
        tm = _pick_tile(count, tm_cap)
        per_seq = count // tm
        M = (M // seq_rows) * count
        x_spec = pl.BlockSpec(
            (pl.Element(tm), pl.Element(K)),
            lambda i, j: (pl.multiple_of((i // per_seq) * seq_rows + first + (i % per_seq) * tm, SUBLANES), 0),
            pipeline_mode=pl.Buffered(1))
    return pl.pallas_call(
        _mlp_ln_kernel,
        out_shape=jax.ShapeDtypeStruct((M, K), f32),
        grid=(M // tm, F // tf),
        in_specs=[x_spec,
                  pl.BlockSpec((None, K, tf), lambda i, j: (layer, 0, j)),
                  pl.BlockSpec((None, tf, K), lambda i, j: (layer, j, 0)),
                  pl.BlockSpec((1, K), lambda i, j: (0, 0)),
                  pl.BlockSpec((1, K), lambda i, j: (0, 0))],
        out_specs=pl.BlockSpec((tm, K), lambda i, j: (i, 0)),
        scratch_shapes=[pltpu.VMEM((tm, K), bf16)],
        compiler_params=_cparams("parallel", "arbitrary"),
        name="mlp_ln",
    )(x, w_up, w_down, g, b)


def _mlstm_kernel(q_ref, k_ref, v_ref, o_ref, g_ref, c0_ref, n0_ref, m0_ref, ng_ref,
                  h_ref, c_ref, n_ref, m_ref, *scratch, t_real, carry):
    nb, lb = q_ref.shape[0], q_ref.shape[1]
    R = nb * lb
    c = pl.program_id(1)
    if carry:
        cs, ns, ms = scratch

        @pl.when(c == 0)
        def _():
            cs[...] = c0_ref[...]
            ns[...] = n0_ref[...]
            ms[...] = m0_ref[...]
    else:
        cs, ns, ms = c0_ref, n0_ref, m0_ref

    heads = range(NH_A)
    seqs = range(nb)
    hs = lambda h: slice(h * HD_A, (h + 1) * HD_A)
    rows = lambda j: slice(j * lb, (j + 1) * lb)
    t_col = c * lb + lax.broadcasted_iota(jnp.int32, (nb, lb, 1), 1).reshape(R, 1)
    vrow = t_col < t_real
    rowi = lax.broadcasted_iota(jnp.int32, (R, R), 0)
    coli = lax.broadcasted_iota(jnp.int32, (R, R), 1)
    if nb == 1:
        causal = rowi >= coli
    else:
        shift = lb.bit_length() - 1
        causal = jnp.logical_and((rowi >> shift) == (coli >> shift), rowi >= coli)

    def per_seq(x):
        return jnp.broadcast_to(x, (nb, lb, x.shape[-1])).reshape(R, x.shape[-1])

    def last_row(x):
        return x.reshape(nb, lb, x.shape[-1])[:, lb - 1:lb, :]

    ld = lambda ref, h: jnp.where(vrow, ref[:, :, hs(h)].astype(f32).reshape(R, HD_A), 0.0)
    q = [ld(q_ref, h) for h in heads]
    k = [ld(k_ref, h) * (HD_A ** -0.5) for h in heads]
    vb = [ld(v_ref, h).astype(bf16) for h in heads]
    qb = [x.astype(bf16) for x in q]
    qk = [lax.dot_general(qb[h], k[h].astype(bf16), NT_DIMS, preferred_element_type=f32) for h in heads]

    gate = jnp.where(vrow, g_ref[...].reshape(R, LANES), 0.0)
    fcum = _dot_hi(causal.astype(f32), gate)
    gate_t = gate.T
    fcum_t = fcum.T
    vcol = (c * lb + (lax.broadcasted_iota(jnp.int32, (1, R), 1) & (lb - 1))) < t_real

    m_prev, fc, igc, fl, m_row, inter, s = [], [], [], [], [], [], []
    for h in heads:
        fc.append(fcum[:, NH_A + h:NH_A + h + 1])
        fr = fcum_t[NH_A + h:NH_A + h + 1, :]
        igc.append(jnp.where(vrow, gate[:, h:h + 1], NEG_BIG))
        igr = jnp.where(vcol, gate_t[h:h + 1, :], NEG_BIG)
        m_prev.append(ms[:, h, :, 0:1])
        logd = jnp.where(causal, fc[h] - fr + igr, NEG_BIG)
        b_in = fc[h] + per_seq(m_prev[h])
        m_row.append(jnp.maximum(b_in, jnp.max(logd, axis=1, keepdims=True)))
        inter.append(jnp.exp(b_in - m_row[h]))
        s.append(qk[h] * jnp.exp(logd - m_row[h]))
        fl.append(last_row(fc[h]))

    sv = [jnp.dot(s[h].astype(bf16), vb[h], preferred_element_type=f32) for h in heads]
    qc = [jnp.concatenate([jnp.dot(qb[h][rows(j)], cs[j, h].astype(bf16), preferred_element_type=f32)
                           for j in seqs], axis=0) for h in heads]
    for h in heads:
        num = sv[h] + inter[h] * qc[h]
        den = jnp.sum(s[h], axis=1, keepdims=True) \
            + inter[h] * jnp.sum(q[h] * per_seq(ns[:, h]), axis=1, keepdims=True)
        hh = num / jnp.maximum(jnp.abs(den), jnp.exp(-m_row[h]))
        mu = jnp.mean(hh, axis=1, keepdims=True)
        dv = hh - mu
        var = jnp.mean(dv * dv, axis=1, keepdims=True)
        hn = dv * lax.rsqrt(var + LN_EPS) * ng_ref[h]
        out = hn * jax.nn.sigmoid(ld(o_ref, h))
        h_ref[:, :, hs(h)] = out.reshape(nb, lb, HD_A).astype(h_ref.dtype)

    kw, m_new, dec = [], [], []
    for h in heads:
        wj = per_seq(fl[h]) - fc[h] + igc[h]
        wmax = jnp.max(wj.reshape(nb, lb, 1), axis=1, keepdims=True)
        m_new.append(jnp.maximum(fl[h] + m_prev[h], wmax))
        dec.append(jnp.exp(fl[h] + m_prev[h] - m_new[h]))
        kw.append(k[h] * jnp.exp(wj - per_seq(m_new[h])))
    kv = [[lax.dot_general(kw[h][rows(j)].astype(bf16), vb[h][rows(j)], TN_DIMS,
                           preferred_element_type=f32) for j in seqs] for h in heads]
    c_dst, n_dst, m_dst = (cs, ns, ms) if carry else (c_ref, n_ref, m_ref)
    for h in heads:
        n_new = dec[h] * ns[:, h] + jnp.sum(kw[h].reshape(nb, lb, HD_A), axis=1, keepdims=True)
        for j in seqs:
            c_dst[j, h] = dec[h][j] * cs[j, h] + kv[h][j]
        n_dst[:, h] = n_new
        m_dst[:, h] = jnp.broadcast_to(m_new[h], (nb, 1, LANES))

    if carry:
        @pl.when(c == pl.num_programs(1) - 1)
        def _():
            c_ref[...] = cs[...]
            n_ref[...] = ns[...]
            m_ref[...] = ms[...]


def _mlstm(zq, zr, gates, c0, n0, m0, norm_g, t_real, out_dtype, nb_short=8):
    B, T, _ = zq.shape
    if T <= SUBLANES:
        nb, lb = nb_short, T
    else:
        nb, lb = 1, MLSTM_CHUNK
    nc = T // lb
    carry = nc > 1
    assert B % nb == 0 and T % lb == 0 and (carry or t_real == T)
    m0b = jnp.broadcast_to(m0[:, :, None, None], (B, NH_A, 1, LANES))
    zspec = lambda group: pl.BlockSpec((nb, lb, MIX_A), lambda b, c: (b, c, group))
    st_spec = lambda r, w: pl.BlockSpec((nb, NH_A, r, w), lambda b, c: (b, 0, 0, 0))
    st_shapes = [(HD_A, HD_A), (1, HD_A), (1, LANES)]
    outs = pl.pallas_call(
        functools.partial(_mlstm_kernel, t_real=t_real, carry=carry),
        out_shape=[jax.ShapeDtypeStruct((B, T, MIX_A), out_dtype)]
        + [jax.ShapeDtypeStruct((B, NH_A) + s, f32) for s in st_shapes],
        grid=(B // nb, nc),
        in_specs=[zspec(0), zspec(1), zspec(2), zspec(0),
                  pl.BlockSpec((nb, lb, LANES), lambda b, c: (b, c, 0))]
        + [st_spec(*s) for s in st_shapes]
        + [pl.BlockSpec((NH_A, 1, HD_A), lambda b, c: (0, 0, 0))],
        out_specs=[pl.BlockSpec((nb, lb, MIX_A), lambda b, c: (b, c, 0))]
        + [st_spec(*s) for s in st_shapes],
        scratch_shapes=[pltpu.VMEM((nb, NH_A) + s, f32) for s in st_shapes] if carry else [],
        compiler_params=_cparams("parallel", "arbitrary"),
        name="mlstm",
    )(zq, zq, zq, zr, gates, c0, n0.reshape(B, NH_A, 1, HD_A), m0b, norm_g.reshape(NH_A, 1, HD_A))
    hm, c_new, n_new, m_new = outs
    return hm, c_new, n_new.reshape(B, NH_A, HD_A), m_new[:, :, 0, 0]


def _one_minus_exp(x):
    u = jnp.exp(x)
    um1 = u - 1.0
    safe = jnp.where(um1 == 0.0, -x, -(um1 * x) / jnp.log(jnp.where(um1 == 0.0, 0.5, u)))
    return jnp.where(x > -0.5, safe, 1.0 - u)


def _rglru_kernel(xr_ref, gr_ref, conv0_ref, h0_ref, cw_ref, cb_ref, wa_ref, ba_ref, wx_ref,
                  bx_ref, lam_ref, y_ref, hl_ref, xp, a_s, h_s, hc, *, t_real):
    nb, lr = xr_ref.shape[0], xr_ref.shape[1]
    rows = nb * lr
    ngroup = rows // SUBLANES
    c = pl.program_id(1)
    hist = SUBLANES - (CONV_W - 1)

    @pl.when(c == 0)
    def _():
        xp[:, 0:SUBLANES, :] = jnp.zeros((nb, SUBLANES, RG_W), f32)
        xp[:, hist:SUBLANES, :] = conv0_ref[...]
        hc[...] = h0_ref[...]

    xp[:, SUBLANES:SUBLANES + lr, :] = xr_ref[...]
    xc = cb_ref[...]
    for j in range(CONV_W):
        xc = xc + xp[:, hist + j:hist + j + lr, :] * cw_ref[j:j + 1, :]
    xp[:, hist:SUBLANES, :] = xp[:, hist + lr:SUBLANES + lr, :]
    xc = xc.reshape(rows, RG_W)

    xcb = xc.astype(bf16)
    ra = []
    rx = []
    for n in range(RG_BLOCKS):
        xn = xcb[:, n * RG_BW:(n + 1) * RG_BW]
        ra.append(jnp.dot(xn, wa_ref[n].astype(bf16), preferred_element_type=f32))
        rx.append(jnp.dot(xn, wx_ref[n].astype(bf16), preferred_element_type=f32))
    r = jax.nn.sigmoid(jnp.concatenate(ra, axis=1) + ba_ref[...])
    gi = jax.nn.sigmoid(jnp.concatenate(rx, axis=1) + bx_ref[...])
    log_a = -RG_C * r * jax.nn.softplus(-lam_ref[...])
    t_idx = c * lr + lax.broadcasted_iota(jnp.int32, (nb, lr, 1), 1).reshape(rows, 1)
    valid = t_idx < t_real
    a = jnp.where(valid, jnp.exp(log_a), 1.0)
    u = jnp.where(valid, jnp.sqrt(_one_minus_exp(2.0 * log_a)) * (gi * xc), 0.0)

    a = a.reshape(ngroup, SUBLANES, RG_W)
    u = u.reshape(ngroup, SUBLANES, RG_W)
    r8 = lax.broadcasted_iota(jnp.int32, (1, SUBLANES, 1), 1)
    d = 1
    while d < SUBLANES:
        inside = r8 >= d
        a_sh = jnp.where(inside, pltpu.roll(a, d, axis=1), 1.0)
        u_sh = jnp.where(inside, pltpu.roll(u, d, axis=1), 0.0)
        u = a * u_sh + u
        a = a * a_sh
        d *= 2

    if lr == SUBLANES:
        h = u + a * hc[...]
        h_last = h[:, SUBLANES - 1:SUBLANES, :]
        h = h.reshape(rows, RG_W)
    else:
        assert nb == 1
        a_s[...] = a.reshape(rows, RG_W)
        h_s[...] = u.reshape(rows, RG_W)

        def step(g, h_prev):
            i = pl.multiple_of(g * SUBLANES, SUBLANES)
            hg = h_s[pl.ds(i, SUBLANES), :] + a_s[pl.ds(i, SUBLANES), :] * h_prev
            h_s[pl.ds(i, SUBLANES), :] = hg
            return hg[SUBLANES - 1:SUBLANES, :]

        h_last = lax.fori_loop(0, ngroup, step, hc[0]).reshape(1, 1, RG_W)
        h = h_s[...]
    hc[...] = h_last
    y_ref[...] = (h * jax.nn.gelu(gr_ref[...].reshape(rows, RG_W))).reshape(nb, lr, RG_W).astype(y_ref.dtype)

    @pl.when(c == pl.num_programs(1) - 1)
    def _():
        hl_ref[...] = h_last


def _rglru(zr, conv0, h0, conv_w, conv_b, wa, ba, wx, bx, lam, t_real, out_dtype, lr_cap=544, nb_short=16):
    B, T, _ = zr.shape
    if T == SUBLANES:
        nb, lr = nb_short, T
    else:
        nb, lr = 1, _pick_tile(T, lr_cap)
    assert B % nb == 0
    row = lambda a: a.reshape(1, RG_W)
    full = lambda shp: pl.BlockSpec(shp, lambda b, c: (0,) * len(shp))
    y, h_last = pl.pallas_call(
        functools.partial(_rglru_kernel, t_real=t_real),
        out_shape=[jax.ShapeDtypeStruct((B, T, RG_W), out_dtype),
                   jax.ShapeDtypeStruct((B, 1, RG_W), f32)],
        grid=(B // nb, T // lr),
        in_specs=[pl.BlockSpec((nb, lr, RG_W), lambda b, c: (b, c, 1)),
                  pl.BlockSpec((nb, lr, RG_W), lambda b, c: (b, c, 2)),
                  pl.BlockSpec((nb, CONV_W - 1, RG_W), lambda b, c: (b, 0, 0)),
                  pl.BlockSpec((nb, 1, RG_W), lambda b, c: (b, 0, 0)),
                  full((CONV_W, RG_W)), full((1, RG_W)),
                  full((RG_BLOCKS, RG_BW, RG_BW)), full((1, RG_W)),
                  full((RG_BLOCKS, RG_BW, RG_BW)), full((1, RG_W)), full((1, RG_W))],
        out_specs=[pl.BlockSpec((nb, lr, RG_W), lambda b, c: (b, c, 0)),
                   pl.BlockSpec((nb, 1, RG_W), lambda b, c: (b, 0, 0))],
        scratch_shapes=[pltpu.VMEM((nb, SUBLANES + lr, RG_W), f32), pltpu.VMEM((nb * lr, RG_W), f32),
                        pltpu.VMEM((nb * lr, RG_W), f32), pltpu.VMEM((nb, 1, RG_W), f32)],
        compiler_params=_cparams("parallel", "arbitrary"),
        name="rglru",
    )(zr, zr, conv0, h0.reshape(B, 1, RG_W), conv_w, row(conv_b), wa, row(ba), wx, row(bx), row(lam))
    return y, h_last.reshape(B, RG_W)


def _token_shift(x_ref, st_ref, xp_ref, c):
    tt = x_ref.shape[1]
    last = SUBLANES - 1

    @pl.when(c == 0)
    def _():
        xp_ref[:, last:SUBLANES, :] = st_ref[...]

    x = x_ref[...]
    xp_ref[:, SUBLANES:SUBLANES + tt, :] = x
    xprev = xp_ref[:, last:last + tt, :]
    xp_ref[:, last:SUBLANES, :] = xp_ref[:, last + tt:SUBLANES + tt, :]
    return x, xprev


def _seq_tile(B, T, rows_cap):
    if T <= SUBLANES:
        return _pick_tile(B * T, min(rows_cap, SHORT_SEQ_ROWS_CAP)) // T, T
    return 1, _pick_tile(T, rows_cap)


def _rwkv_rkv_kernel(x_ref, st_ref, mu_ref, w_ref, o_ref, xs_ref, xp_ref):
    c = pl.program_id(1)
    s = pl.program_id(2)

    @pl.when(jnp.logical_and(s == 0, pl.program_id(3) == 0))
    def _():
        x, xprev = _token_shift(x_ref, st_ref, xp_ref, c)
        dx = xprev - x
        for si in range(3):
            xs_ref[si] = (x + dx * mu_ref[si]).reshape(xs_ref.shape[1:]).astype(bf16)

    o_ref[...] = jnp.dot(xs_ref[s], w_ref[...], preferred_element_type=f32)


def _rwkv_rkv(x, shift0, mu3, w3, rows_cap=1088, tn=512):
    B, T, K = x.shape
    nb, tt = _seq_tile(B, T, rows_cap)
    rows = nb * tt
    nc = T // tt
    return pl.pallas_call(
        _rwkv_rkv_kernel,
        out_shape=jax.ShapeDtypeStruct((3, B * T, K), f32),
        grid=(B // nb, nc, 3, K // tn),
        in_specs=[pl.BlockSpec((nb, tt, K), lambda b, c, s, j: (b, c, 0), pipeline_mode=pl.Buffered(1)),
                  pl.BlockSpec((nb, 1, K), lambda b, c, s, j: (b, 0, 0)),
                  pl.BlockSpec((3, 1, K), lambda b, c, s, j: (0, 0, 0)),
                  pl.BlockSpec((None, K, tn), lambda b, c, s, j: (s, 0, j))],
        out_specs=pl.BlockSpec((None, rows, tn), lambda b, c, s, j: (s, b * nc + c, j)),
        scratch_shapes=[pltpu.VMEM((3, rows, K), bf16), pltpu.VMEM((nb, SUBLANES + tt, K), f32)],
        compiler_params=_cparams("parallel", "arbitrary", "arbitrary", "arbitrary"),
        name="rwkv_rkv",
    )(x, shift0, mu3, w3)


def _rwkv_lora_kernel(x_ref, st_ref, mu_ref, w1_ref, w2w_ref, w2a_ref, w2g_ref, w0_ref, a0_ref,
                      lw_ref, a_ref, g_ref, xp_ref):
    x, xprev = _token_shift(x_ref, st_ref, xp_ref, pl.program_id(1))
    dx = xprev - x
    mix = lambda i: (x + dx * mu_ref[i]).reshape(lw_ref.shape).astype(bf16)
    r = LORA_PAD
    hw = jnp.tanh(jnp.dot(mix(0), w1_ref[:, 0:r], preferred_element_type=f32))
    ha = jnp.dot(mix(1), w1_ref[:, r:2 * r], preferred_element_type=f32)
    hg = jax.nn.sigmoid(jnp.dot(mix(2), w1_ref[:, 2 * r:], preferred_element_type=f32))
    w = w0_ref[...] + jnp.dot(hw.astype(bf16), w2w_ref[...], preferred_element_type=f32)
    w = -jax.nn.softplus(-w) - 0.5
    lw_ref[...] = -jnp.exp(w)
    a = a0_ref[...] + jnp.dot(ha.astype(bf16), w2a_ref[...], preferred_element_type=f32)
    a_ref[...] = jax.nn.sigmoid(a)
    g_ref[...] = jnp.dot(hg.astype(bf16), w2g_ref[...], preferred_element_type=f32)


def _rwkv_lora(x, shift0, mu3, w1, w2w, w2a, w2g, w0, a0, rows_cap=272):
    B, T, K = x.shape
    nb, tt = _seq_tile(B, T, rows_cap)
    rows = nb * tt
    nc = T // tt
    full = lambda a: pl.BlockSpec(a.shape, lambda b, c: (0,) * a.ndim)
    out = pl.BlockSpec((rows, K), lambda b, c: (b * nc + c, 0))
    return pl.pallas_call(
        _rwkv_lora_kernel,
        out_shape=[jax.ShapeDtypeStruct((B * T, K), f32)] * 3,
        grid=(B // nb, nc),
        in_specs=[pl.BlockSpec((nb, tt, K), lambda b, c: (b, c, 0)),
                  pl.BlockSpec((nb, 1, K), lambda b, c: (b, 0, 0)),
                  full(mu3), full(w1), full(w2w), full(w2a), full(w2g), full(w0), full(a0)],
        out_specs=[out, out, out],
        scratch_shapes=[pltpu.VMEM((nb, SUBLANES + tt, K), f32)],
        compiler_params=_cparams("parallel", "arbitrary"),
        name="rwkv_lora",
    )(x, shift0, mu3, w1, w2w, w2a, w2g, w0, a0)


def _head_sums(x, ones_bd, npair):
    rows = x.shape[0]
    xs = jnp.concatenate([x[:, g * LANES:(g + 1) * LANES] for g in range(npair)], axis=0)
    s = jnp.dot(xs.astype(bf16), ones_bd, preferred_element_type=f32)
    return jnp.concatenate([s[g * rows:(g + 1) * rows] for g in range(npair)], axis=1)


def _rwkv_scan_kernel(r_ref, k_ref, v_ref, lw_ref, ag_ref, g_ref, s0_ref, kk_ref, ka_ref, rk_ref,
                      lg_ref, lb_ref, y_ref, s_ref, sbd, *, t_real, seg, npair):
    R = RWKV_TILE
    R2 = 2 * R
    nseg = R // seg
    W = npair * LANES
    shift = seg.bit_length() - 1
    c = pl.program_id(2)
    mm = lambda a, b: jnp.dot(a.astype(bf16), b.astype(bf16), preferred_element_type=f32)
    mm_nt = lambda a, b: lax.dot_general(a.astype(bf16), b.astype(bf16), NT_DIMS, preferred_element_type=f32)
    mm_tn = lambda a, b: lax.dot_general(a.astype(bf16), b.astype(bf16), TN_DIMS, preferred_element_type=f32)

    @pl.when(c == 0)
    def _():
        sbd[...] = jnp.zeros(sbd.shape, f32)
        for gi in range(npair):
            for j in range(nseg):
                sbd[gi, j, 0:HD_C, 0:HD_C] = s0_ref[j, 2 * gi]
                sbd[gi, j, HD_C:R2, HD_C:R2] = s0_ref[j, 2 * gi + 1]

    row1 = lax.broadcasted_iota(jnp.int32, (R, 1), 0)
    valid = c * seg + (row1 & (seg - 1)) < t_real
    ld = lambda ref: jnp.where(valid, ref[...].reshape(R, W), 0.0)
    r = ld(r_ref)
    k = ld(k_ref)
    v = ld(v_ref)
    lw = ld(lw_ref)
    ag = ld(ag_ref)

    lane = lax.broadcasted_iota(jnp.int32, (1, LANES), 1)
    m0 = (lane < HD_C).astype(f32)
    m1 = 1.0 - m0
    row2 = lax.broadcasted_iota(jnp.int32, (R2, R2), 0)
    col2 = lax.broadcasted_iota(jnp.int32, (R2, R2), 1)
    ones_bd = ((row2 < HD_C) == (col2 < HD_C)).astype(bf16)
    same_blk = (row2 >> shift) == (col2 >> shift)
    strict = jnp.logical_and(same_blk, row2 > col2)
    incl = jnp.logical_and(same_blk, row2 >= col2)
    eye = (row2 == col2).astype(f32)

    kk = k * kk_ref[...]
    kk = kk / jnp.maximum(jnp.sqrt(_head_sums(kk * kk, ones_bd, npair)), 1e-12)
    kp = k * (1.0 + (ag - 1.0) * ka_ref[...])
    bv = kk * ag

    rowR = lax.broadcasted_iota(jnp.int32, (R, R), 0)
    colR = lax.broadcasted_iota(jnp.int32, (R, R), 1)
    tril = jnp.logical_and((rowR >> shift) == (colR >> shift), rowR >= colR).astype(bf16)
    lw_hi, lw_lo = _split_hi_lo(lw)
    cw = (jnp.dot(tril, lw_hi, preferred_element_type=f32)
          + jnp.dot(tril, lw_lo, preferred_element_type=f32))
    cwl = jnp.concatenate(
        [jnp.broadcast_to(cw[j * seg + seg - 1:(j + 1) * seg, :], (seg, W)) for j in range(nseg)], axis=0)
    p_in = jnp.exp(cw)
    p_inv = jnp.exp(-cw)
    p_tail = jnp.exp(cwl - cw)
    at = -kk * jnp.exp(cw - lw)
    rt = r * p_in
    kt = kp * p_inv
    bt = bv * p_inv
    kh = kp * p_tail
    bh = bv * p_tail
    p_last = jnp.exp(cwl)

    def stack(x, gi):
        xg = x[:, gi * LANES:(gi + 1) * LANES]
        return jnp.concatenate([xg * m0, xg * m1], axis=0)

    def pick(x_st, j):
        if nseg == 1:
            return x_st
        return jnp.concatenate([x_st[j * seg:(j + 1) * seg], x_st[R + j * seg:R + (j + 1) * seg]], axis=0)

    def unpick(parts):
        if nseg == 1:
            return parts[0]
        return jnp.concatenate([q[0:seg] for q in parts] + [q[seg:2 * seg] for q in parts], axis=0)

    n_sq = shift - 1
    pairs = range(npair)
    s_prev = [[sbd[gi, j] for j in range(nseg)] for gi in pairs]
    a_st = [stack(at, gi) for gi in pairs]
    r_st = [stack(rt, gi) for gi in pairs]
    v_st = [stack(v, gi) for gi in pairs]
    prod = [mm_nt(jnp.concatenate([a_st[gi], r_st[gi]], axis=0),
                  jnp.concatenate([stack(kt, gi), stack(bt, gi)], axis=0)) for gi in pairs]
    m_ak = [jnp.where(strict, prod[gi][0:R2, 0:R2], 0.0) for gi in pairs]
    a_rk = [jnp.where(incl, prod[gi][R2:2 * R2, 0:R2], 0.0) for gi in pairs]
    a_rb = [jnp.where(incl, prod[gi][R2:2 * R2, R2:2 * R2], 0.0) for gi in pairs]

    x = [jnp.where(strict, prod[gi][0:R2, R2:2 * R2], 0.0) for gi in pairs]
    tinv = [eye + x[gi] for gi in pairs]
    for _ in range(n_sq):
        x = [mm(x[gi], x[gi]) for gi in pairs]
        tinv = [tinv[gi] + mm(tinv[gi], x[gi]) for gi in pairs]

    ar_s = [[mm_nt(jnp.concatenate([pick(a_st[gi], j), pick(r_st[gi], j)], axis=0), s_prev[gi][j])
             for j in range(nseg)] for gi in pairs]
    mv = [mm(m_ak[gi], v_st[gi]) for gi in pairs]
    u_st = [mm(tinv[gi], unpick([q[0:2 * seg] for q in ar_s[gi]]) + mv[gi]) for gi in pairs]
    o_st = [unpick([q[2 * seg:4 * seg] for q in ar_s[gi]])
            + mm(jnp.concatenate([a_rk[gi], a_rb[gi]], axis=1),
                 jnp.concatenate([v_st[gi], u_st[gi]], axis=0)) for gi in pairs]
    for gi in pairs:
        bh_st = stack(bh, gi)
        kh_st = stack(kh, gi)
        for j in range(nseg):
            pl_row = p_last[j * seg:j * seg + 1, gi * LANES:(gi + 1) * LANES]
            upd = mm_tn(jnp.concatenate([pick(u_st[gi], j), pick(v_st[gi], j)], axis=0),
                        jnp.concatenate([pick(bh_st, j), pick(kh_st, j)], axis=0))
            sbd[gi, j] = s_prev[gi][j] * pl_row + upd

    y = jnp.concatenate([o[0:R] + o[R:R2] for o in o_st], axis=1)
    inv_n = 1.0 / HD_C
    sums = _head_sums(jnp.concatenate([y, r * kp * rk_ref[...]], axis=0), ones_bd, npair)
    mu = sums[0:R] * inv_n
    dy = y - mu
    var = _head_sums(dy * dy, ones_bd, npair) * inv_n
    yn = dy * lax.rsqrt(var + GN_EPS) * lg_ref[...] + lb_ref[...]
    out = (yn + sums[R:R2] * v) * g_ref[...].reshape(R, W)
    y_ref[...] = out.reshape(y_ref.shape).astype(y_ref.dtype)

    @pl.when(c == pl.num_programs(2) - 1)
    def _():
        for gi in range(npair):
            for j in range(nseg):
                s_ref[j, 2 * gi] = sbd[gi, j, 0:HD_C, 0:HD_C]
                s_ref[j, 2 * gi + 1] = sbd[gi, j, HD_C:R2, HD_C:R2]


def _rwkv_scan(rkv, lw, ag, g, s0, kk_p, ka_p, rk_p, lnx_g, lnx_b, t_real, out_dtype, npair=16):
    _, B, T, D = rkv.shape
    seg = min(RWKV_TILE, T)
    nseg = RWKV_TILE // seg
    nc = T // seg
    assert nseg == 1 or nc == 1
    assert B % nseg == 0 and T % seg == 0
    W = npair * LANES
    tok = pl.BlockSpec((nseg, seg, W), lambda b, p, c: (b, c, p))
    rkv_spec = lambda s: pl.BlockSpec((None, nseg, seg, W), lambda b, p, c: (s, b, c, p))
    prm = pl.BlockSpec((1, W), lambda b, p, c: (0, p))
    st = pl.BlockSpec((nseg, 2 * npair, HD_C, HD_C), lambda b, p, c: (b, p, 0, 0))
    row = lambda a: a.reshape(1, D)
    return pl.pallas_call(
        functools.partial(_rwkv_scan_kernel, t_real=t_real, seg=seg, npair=npair),
        out_shape=[jax.ShapeDtypeStruct((B, T, D), out_dtype),
                   jax.ShapeDtypeStruct((B, NH_C, HD_C, HD_C), f32)],
        grid=(B // nseg, D // W, nc),
        in_specs=[rkv_spec(0), rkv_spec(1), rkv_spec(2), tok, tok, tok, st, prm, prm, prm, prm, prm],
        out_specs=[tok, st],
        scratch_shapes=[pltpu.VMEM((npair, nseg, 2 * RWKV_TILE, 2 * RWKV_TILE), f32)],
        compiler_params=_cparams("parallel", "parallel", "arbitrary"),
        name="rwkv_scan",
    )(rkv, rkv, rkv, lw, ag, g, s0, row(kk_p), row(ka_p), row(rk_p), row(lnx_g), row(lnx_b))


def _pad_lora(w1, w2):
    rank = w1.shape[1]
    return (jnp.pad(w1, ((0, 0), (0, LORA_PAD - rank))), jnp.pad(w2, ((0, LORA_PAD - rank), (0, 0))))


def _prepare_params(p):
    w_in_t = jnp.swapaxes(p['w_in_ab'][0], 0, 1)
    n_qkvo = 4 * MIX_A
    n_g = 2 * NH_A
    q = {}
    q['w_in'] = jnp.concatenate([w_in_t[:n_qkvo], w_in_t[n_qkvo + n_g:]], axis=0).astype(bf16)
    q['w_gate'] = jnp.stack(_split_hi_lo(jnp.pad(w_in_t[n_qkvo:n_qkvo + n_g], ((0, LANES - n_g), (0, 0)))))
    q['b_gate'] = jnp.pad(p['b_if_ab'][0], (0, LANES - n_g)).reshape(1, LANES)
    q['w_out_ab'] = p['w_out_ab'][0].astype(bf16)
    q['w_up'] = p['w_up'].astype(bf16)
    q['w_down'] = p['w_down'].astype(bf16)
    q['w_rkv'] = jnp.stack([p['rw_wr'][0], p['rw_wk'][0], p['rw_wv'][0]]).astype(bf16)
    mu = p['rw_mu'][0]
    q['mu_rkv'] = jnp.stack([mu[0], mu[2], mu[3]])[:, None, :]
    q['mu_lora'] = jnp.stack([mu[1], mu[4], mu[5]])[:, None, :]
    w1, w2w = _pad_lora(p['rw_w1'][0], p['rw_w2'][0])
    a1, w2a = _pad_lora(p['rw_a1'][0], p['rw_a2'][0])
    q['lora_w1'] = jnp.concatenate([w1, a1, p['rw_g1'][0]], axis=1).astype(bf16)
    q['lora_w2w'] = w2w.astype(bf16)
    q['lora_w2a'] = w2a.astype(bf16)
    q['lora_w2g'] = p['rw_g2'][0].astype(bf16)
    q['w_o'] = p['rw_wo'][0].astype(bf16)
    return q


def _trunk(h, states, p, q, t_real, out_from=0):
    mC, mn, mm, rgh, rgc, rwS, rwx = states
    B, T, D = h.shape
    M = B * T
    row = lambda a: a.reshape(1, -1)
    hf = h.reshape(M, D)

    act_dtype = bf16 if T % (2 * SUBLANES) == 0 else f32
    zq, zr, gates = _inproj(hf, q['w_in'], q['w_gate'], q['b_gate'], act_dtype)
    zq = zq.reshape(B, T, -1)
    zr = zr.reshape(B, T, -1)
    hm, c_new, n_new, m_new = _mlstm(zq, zr, gates.reshape(B, T, LANES), mC[0], mn[0], mm[0],
                                     p['mlstm_norm_g'][0], t_real, act_dtype)
    yr, h_last = _rglru(zr, rgc[0], rgh[0], p['rg_conv_w'][0], p['rg_conv_b'][0], p['rg_wa'][0],
                        p['rg_ba'][0], p['rg_wx'][0], p['rg_bx'][0], p['rg_lambda'][0], t_real, act_dtype)
    assert t_real >= CONV_W - 1
    conv_new = zr[:, t_real - (CONV_W - 1):t_real, MIX_A:MIX_A + RG_W]
    h1 = _proj_ln([(hm.reshape(M, MIX_A), MIX_A, 0), (yr.reshape(M, RG_W), RG_W, 0)],
                  q['w_out_ab'], hf, row(p['ln1_g'][0]), row(p['ln1_b'][0]))
    h2 = _mlp_ln(h1, q['w_up'], q['w_down'], 0, row(p['ln2_g'][0]), row(p['ln2_b'][0]))

    h2_3 = h2.reshape(B, T, D)
    shift0 = rwx[0].astype(f32)[:, None]
    rkv = _rwkv_rkv(h2_3, shift0, q['mu_rkv'], q['w_rkv'])
    lw, ag, g = _rwkv_lora(h2_3, shift0, q['mu_lora'], q['lora_w1'], q['lora_w2w'], q['lora_w2a'],
                           q['lora_w2g'], row(p['rw_w0'][0]), row(p['rw_a0'][0]))
    r3 = lambda a: a.reshape(B, T, D)
    xo, s_new = _rwkv_scan(rkv.reshape(3, B, T, D), r3(lw), r3(ag), r3(g), rwS[0], p['rw_kk'][0],
                           p['rw_ka'][0], p['rw_rk'][0], p['rw_lnx_g'][0], p['rw_lnx_b'][0], t_real,
                           act_dtype)
    xo = xo.reshape(M, D)
    h3 = _proj_ln([(xo, D, 0)], q['w_o'], h2, row(p['ln1_g'][1]), row(p['ln1_b'][1]))
    n_keep = t_real - out_from
    h4 = _mlp_ln(h3, q['w_up'], q['w_down'], 1, row(p['ln2_g'][1]), row(p['ln2_b'][1]),
                 window=None if (out_from == 0 and t_real == T) else (T, out_from, n_keep))
    shift_new = h2_3[:, t_real - 1]

    new_states = (c_new[None], n_new[None], m_new[None], h_last[None], conv_new[None],
                  s_new[None], shift_new[None])
    return h4.reshape(B, n_keep, D), new_states


def kernel(x_prompt, x_sample, state_mlstm_C, state_mlstm_n, state_mlstm_m, state_rglru_h, state_rglru_conv, state_rwkv_S, state_rwkv_shift, meta_tokens, w_in_ab, b_if_ab, mlstm_norm_g, rg_conv_w, rg_conv_b, rg_wa, rg_ba, rg_wx, rg_bx, rg_lambda, w_out_ab, rw_mu, rw_wr, rw_wk, rw_wv, rw_wo, rw_w0, rw_w1, rw_w2, rw_a0, rw_a1, rw_a2, rw_g1, rw_g2, rw_kk, rw_ka, rw_rk, rw_lnx_g, rw_lnx_b, ln1_g, ln1_b, ln2_g, ln2_b, w_up, w_down):
    p = dict(w_in_ab=w_in_ab, b_if_ab=b_if_ab, mlstm_norm_g=mlstm_norm_g, rg_conv_w=rg_conv_w,
             rg_conv_b=rg_conv_b, rg_wa=rg_wa, rg_ba=rg_ba, rg_wx=rg_wx, rg_bx=rg_bx,
             rg_lambda=rg_lambda, w_out_ab=w_out_ab, rw_mu=rw_mu, rw_wr=rw_wr, rw_wk=rw_wk,
             rw_wv=rw_wv, rw_wo=rw_wo, rw_w0=rw_w0, rw_w1=rw_w1, rw_w2=rw_w2, rw_a0=rw_a0,
             rw_a1=rw_a1, rw_a2=rw_a2, rw_g1=rw_g1, rw_g2=rw_g2, rw_kk=rw_kk, rw_ka=rw_ka,
             rw_rk=rw_rk, rw_lnx_g=rw_lnx_g, rw_lnx_b=rw_lnx_b, ln1_g=ln1_g, ln1_b=ln1_b,
             ln2_g=ln2_g, ln2_b=ln2_b, w_up=w_up, w_down=w_down)
    q = _prepare_params(p)
    B, S, D = x_prompt.shape
    dt = x_prompt.dtype
    t_prompt = N_META + S
    t_pad = -(-t_prompt // MLSTM_CHUNK) * MLSTM_CHUNK
    zero_states = (
        jnp.zeros((1, B, NH_A, HD_A, HD_A), f32),
        jnp.zeros((1, B, NH_A, HD_A), f32),
        jnp.zeros((1, B, NH_A), f32),
        jnp.zeros((1, B, RG_W), f32),
        jnp.zeros((1, B, CONV_W - 1, RG_W), dt),
        jnp.zeros((1, B, NH_C, HD_C, HD_C), f32),
        jnp.zeros((1, B, D), dt),
    )
    meta = jnp.broadcast_to(meta_tokens.astype(dt)[None], (B, N_META, D))
    tail = jnp.zeros((B, t_pad - t_prompt, D), dt)
    hp, ps = _trunk(jnp.concatenate([meta, x_prompt, tail], axis=1), zero_states, p, q, t_prompt,
                    out_from=N_META)
    sample_states = (state_mlstm_C, state_mlstm_n, state_mlstm_m, state_rglru_h, state_rglru_conv,
                     state_rwkv_S, state_rwkv_shift)
    hs, ss = _trunk(x_sample, sample_states, p, q, x_sample.shape[1])
    return (hp, hs, *ps, *ss)
```

```python
import functools
from typing import NamedTuple, Optional

import jax
import jax.numpy as jnp
from jax import lax
from jax.experimental import pallas as pl
from jax.experimental.pallas import tpu as pltpu

f32 = jnp.float32
bf16 = jnp.bfloat16
HI = lax.Precision.HIGHEST

D_MODEL = 2048
N_META = 16
MIX_A = 1024
NH_A = 4
HD_A = 256
RG_W = 1024
RG_BLOCKS = 8
RG_BW = 128
CONV_W = 4
RG_C = 8.0
HD_C = 64
NH_C = 32
D_FF = 8192
LN_EPS = 1e-5
GN_EPS = 64e-5
DEPTH = 2
ALPHA = (2.0 * DEPTH) ** 0.25

LANES = 128
SUBLANES = 8
VMEM_LIMIT_BYTES = 56 * 1024 * 1024

MLSTM_CHUNK = 128
RWKV_TILE = 64
SHORT_SEQ_ROWS_CAP = 512
LORA_PAD = 128
NEG_BIG = -1e30

NT_DIMS = (((1,), (1,)), ((), ()))
TN_DIMS = (((0,), (0,)), ((), ()))


class _Tokens(NamedTuple):
    x: jax.Array
    lead: Optional[jax.Array]
    t_pad: int

    @property
    def batch(self):
        return self.x.shape[0]


def _cparams(*sem):
    return pltpu.CompilerParams(dimension_semantics=sem, vmem_limit_bytes=VMEM_LIMIT_BYTES)


def _pick_tile(n, cap):
    best = None
    for d in range(SUBLANES, min(n, cap) + 1, SUBLANES):
        if n % d == 0:
            best = d
    assert best is not None, (n, cap)
    return best


def _dot_bf16(a, b):
    return jnp.dot(a.astype(bf16), b.astype(bf16), preferred_element_type=f32)


def _dot_hi(a, b, dims=None):
    if dims is None:
        return jnp.dot(a, b, precision=HI, preferred_element_type=f32)
    return lax.dot_general(a, b, dims, precision=HI, preferred_element_type=f32)


def _split_hi_lo(x):
    hi = x.astype(bf16)
    return hi, (x - hi.astype(f32)).astype(bf16)


def _layer_norm_rows(t, g, b):
    mu = jnp.mean(t, axis=-1, keepdims=True)
    d = t - mu
    var = jnp.mean(d * d, axis=-1, keepdims=True)
    return d * lax.rsqrt(var + LN_EPS) * g + b


def _window_spec(tm, K, seq_rows, t_pad, n_lead, grid_rank, **kw):
    tps = t_pad // tm
    assert tps >= 2 and tps * tm == t_pad and t_pad - n_lead - seq_rows <= tm

    def index(g, *_):
        start = jnp.clip(tm * (g % tps) - n_lead, 0, seq_rows - tm)
        return (pl.multiple_of((g // tps) * seq_rows + start, SUBLANES), 0)

    assert grid_rank in (1, 2)
    return pl.BlockSpec((pl.Element(tm), pl.Element(K)), index, **kw)


def _virtual_rows(blk, lead_ref, i_seq, tps, n_tail):
    tm = blk.shape[0]
    n_lead = lead_ref.shape[0]
    first = jnp.concatenate([lead_ref[...], blk[:tm - n_lead]], axis=0)
    last = jnp.concatenate([blk[n_tail:], jnp.zeros((n_tail, blk.shape[1]), blk.dtype)], axis=0)
    return jnp.where(i_seq == 0, first, jnp.where(i_seq == tps - 1, last, blk))


def _inproj_kernel(x_ref, w_ref, wg_ref, bg_ref, *rest, virtual):
    if virtual is None:
        o1_ref, o2_ref, g_ref, xb_ref = rest
    else:
        lead_ref, o1_ref, o2_ref, g_ref, xb_ref = rest

    @pl.when(pl.program_id(1) == 0)
    def _():
        x = x_ref[...]
        if virtual is not None:
            tps, n_tail = virtual
            x = _virtual_rows(x, lead_ref, pl.program_id(0) % tps, tps, n_tail)
        x_hi, x_lo = _split_hi_lo(x)
        xb_ref[...] = x_hi
        nt = lambda a, b: lax.dot_general(a, b, NT_DIMS, preferred_element_type=f32)
        pre = (nt(x_hi, wg_ref[0]) + (nt(x_hi, wg_ref[1]) + nt(x_lo, wg_ref[0]))) + bg_ref[...]
        lane = lax.broadcasted_iota(jnp.int32, pre.shape, 1)
        g_ref[...] = jnp.where(lane < NH_A, pre,
                               jnp.where(lane < 2 * NH_A, jax.nn.log_sigmoid(pre), 0.0))

    y = lax.dot_general(xb_ref[...], w_ref[...], NT_DIMS, preferred_element_type=f32)
    j = pl.program_id(1)
    n_first = pl.num_programs(1) // 2

    @pl.when(j < n_first)
    def _():
        o1_ref[...] = y.astype(o1_ref.dtype)

    @pl.when(j >= n_first)
    def _():
        o2_ref[...] = y


def _inproj(tok, w, w_g, b_g, qkv_dtype, tm_cap=1088, tn=1024):
    x = tok.x.reshape(-1, tok.x.shape[-1])
    K = x.shape[1]
    M = tok.batch * tok.t_pad
    n_out = w.shape[0]
    n_half = n_out // 2
    nt = n_half // tn
    if tok.lead is None:
        tm = _pick_tile(M, tm_cap)
        x_specs, x_args, virtual = [pl.BlockSpec((tm, K), lambda i, j: (i, 0))], [x], None
    else:
        tm = _pick_tile(tok.t_pad, tm_cap)
        n_lead = tok.lead.shape[0]
        x_specs = [_window_spec(tm, K, tok.x.shape[1], tok.t_pad, n_lead, 2)]
        x_args = [x]
        virtual = (tok.t_pad // tm, tok.t_pad - n_lead - tok.x.shape[1])
    lead_specs = [] if tok.lead is None else [pl.BlockSpec(tok.lead.shape, lambda i, j: (0, 0))]
    lead_args = [] if tok.lead is None else [tok.lead]
    return pl.pallas_call(
        functools.partial(_inproj_kernel, virtual=virtual),
        out_shape=[jax.ShapeDtypeStruct((M, n_half), qkv_dtype), jax.ShapeDtypeStruct((M, n_half), f32),
                   jax.ShapeDtypeStruct((M, LANES), f32)],
        grid=(M // tm, n_out // tn),
        in_specs=x_specs + [pl.BlockSpec((tn, K), lambda i, j: (j, 0)),
                            pl.BlockSpec((2, LANES, K), lambda i, j: (0, 0, 0)),
                            pl.BlockSpec((1, LANES), lambda i, j: (0, 0))] + lead_specs,
        out_specs=[pl.BlockSpec((tm, tn), lambda i, j: (i, jnp.minimum(j, nt - 1))),
                   pl.BlockSpec((tm, tn), lambda i, j: (i, jnp.maximum(j - nt, 0))),
                   pl.BlockSpec((tm, LANES), lambda i, j: (i, 0))],
        scratch_shapes=[pltpu.VMEM((tm, K), bf16)],
        compiler_params=_cparams("parallel", "arbitrary"),
        name="inproj",
    )(*x_args, w, w_g, b_g, *lead_args)


def _proj_ln_kernel(*refs, n_x, virtual):
    x_refs = refs[:n_x]
    if virtual is None:
        w_ref, res_ref, g_ref, b_ref, o_ref = refs[n_x:]
        res = res_ref[...]
    else:
        w_ref, res_ref, g_ref, b_ref, lead_ref, o_ref = refs[n_x:]
        tps, n_tail = virtual
        res = _virtual_rows(res_ref[...], lead_ref, pl.program_id(0) % tps, tps, n_tail)
    y = None
    k0 = 0
    for x_ref in x_refs:
        kc = x_ref.shape[1]
        part = _dot_bf16(x_ref[...], w_ref[k0:k0 + kc, :])
        y = part if y is None else y + part
        k0 += kc
    t = ALPHA * res + y
    o_ref[...] = _layer_norm_rows(t, g_ref[...], b_ref[...])


def _proj_ln(xs, w, res, g, b, tm_cap=272):
    if isinstance(res, _Tokens) and res.lead is not None:
        tok = res
        N = tok.x.shape[-1]
        M = tok.batch * tok.t_pad
        tm = _pick_tile(tok.t_pad, tm_cap)
        n_lead = tok.lead.shape[0]
        res = tok.x.reshape(-1, N)
        res_spec = _window_spec(tm, N, tok.x.shape[1], tok.t_pad, n_lead, 1)
        virtual = (tok.t_pad // tm, tok.t_pad - n_lead - tok.x.shape[1])
        lead_specs, lead_args = [pl.BlockSpec(tok.lead.shape, lambda i: (0, 0))], [tok.lead]
    else:
        if isinstance(res, _Tokens):
            res = res.x.reshape(-1, res.x.shape[-1])
        M, N = res.shape
        tm = _pick_tile(M, tm_cap)
        res_spec = pl.BlockSpec((tm, N), lambda i: (i, 0))
        virtual, lead_specs, lead_args = None, [], []
    in_specs = []
    args = []
    for arr, width, blk in xs:
        in_specs.append(pl.BlockSpec((tm, width), functools.partial(lambda i, blk: (i, blk), blk=blk)))
        args.append(arr)
    K = w.shape[0]
    in_specs += [pl.BlockSpec((K, N), lambda i: (0, 0)),
                 res_spec,
                 pl.BlockSpec((1, N), lambda i: (0, 0)),
                 pl.BlockSpec((1, N), lambda i: (0, 0))] + lead_specs
    return pl.pallas_call(
        functools.partial(_proj_ln_kernel, n_x=len(xs), virtual=virtual),
        out_shape=jax.ShapeDtypeStruct((M, N), f32),
        grid=(M // tm,),
        in_specs=in_specs,
        out_specs=pl.BlockSpec((tm, N), lambda i: (i, 0)),
        compiler_params=_cparams("parallel"),
        name="proj_ln",
    )(*args, w, res, g, b, *lead_args)


def _mlp_ln_kernel(x_ref, wu_ref, wd_ref, g_ref, b_ref, o_ref, xb_ref):
    f = pl.program_id(1)

    @pl.when(f == 0)
    def _():
        xb_ref[...] = x_ref[...].astype(bf16)
        o_ref[...] = jnp.zeros_like(o_ref)

    a = jnp.dot(xb_ref[...], wu_ref[...], preferred_element_type=f32)
    a = jnp.square(jnp.maximum(a, 0.0))
    o_ref[...] += jnp.dot(a.astype(bf16), wd_ref[...], preferred_element_type=f32)

    @pl.when(f == pl.num_programs(1) - 1)
    def _():
        t = ALPHA * x_ref[...] + o_ref[...]
        o_ref[...] = _layer_norm_rows(t, g_ref[...], b_ref[...])


def _mlp_ln(x, w_up, w_down, layer, g, b, tm_cap=1088, tf=512, window=None):
    M, K = x.shape
    F = w_up.shape[2]
    if window is None:
        tm = _pick_tile(M, tm_cap)
        x_spec = pl.BlockSpec((tm, K), lambda i, j: (i, 0), pipeline_mode=pl.Buffered(1))
    else:
        seq_rows, first, count = window
        assert first % SUBLANES == 0 and seq_rows % SUBLANES == 0
        tm = _pick_tile(count, tm_cap)
        per_seq = count // tm
        M = (M // seq_rows) * count
        x_spec = pl.BlockSpec(
            (pl.Element(tm), pl.Element(K)),
            lambda i, j: (pl.multiple_of((i // per_seq) * seq_rows + first + (i % per_seq) * tm, SUBLANES), 0),
            pipeline_mode=pl.Buffered(1))
    return pl.pallas_call(
        _mlp_ln_kernel,
        out_shape=jax.ShapeDtypeStruct((M, K), f32),
        grid=(M // tm, F // tf),
        in_specs=[x_spec,
                  pl.BlockSpec((None, K, tf), lambda i, j: (layer, 0, j)),
                  pl.BlockSpec((None, tf, K), lambda i, j: (layer, j, 0)),
                  pl.BlockSpec((1, K), lambda i, j: (0, 0)),
                  pl.BlockSpec((1, K), lambda i, j: (0, 0))],
        out_specs=pl.BlockSpec((tm, K), lambda i, j: (i, 0)),
        scratch_shapes=[pltpu.VMEM((tm, K), bf16)],
        compiler_params=_cparams("parallel", "arbitrary"),
        name="mlp_ln",
    )(x, w_up, w_down, g, b)


def _mlstm_kernel(q_ref, k_ref, v_ref, o_ref, g_ref, c0_ref, n0_ref, m0_ref, ng_ref,
                  h_ref, c_ref, n_ref, m_ref, *scratch, t_real, carry):
    nb, lb = q_ref.shape[0], q_ref.shape[1]
    R = nb * lb
    c = pl.program_id(1)
    if carry:
        cs, ns, ms = scratch

        @pl.when(c == 0)
        def _():
            cs[...] = c0_ref[...]
            ns[...] = n0_ref[...]
            ms[...] = m0_ref[...]
    else:
        cs, ns, ms = c0_ref, n0_ref, m0_ref

    heads = range(NH_A)
    seqs = range(nb)
    hs = lambda h: slice(h * HD_A, (h + 1) * HD_A)
    rows = lambda j: slice(j * lb, (j + 1) * lb)
    t_col = c * lb + lax.broadcasted_iota(jnp.int32, (nb, lb, 1), 1).reshape(R, 1)
    vrow = t_col < t_real
    rowi = lax.broadcasted_iota(jnp.int32, (R, R), 0)
    coli = lax.broadcasted_iota(jnp.int32, (R, R), 1)
    if nb == 1:
        causal = rowi >= coli
    else:
        shift = lb.bit_length() - 1
        causal = jnp.logical_and((rowi >> shift) == (coli >> shift), rowi >= coli)

    def per_seq(x):
        return jnp.broadcast_to(x, (nb, lb, x.shape[-1])).reshape(R, x.shape[-1])

    def last_row(x):
        return x.reshape(nb, lb, x.shape[-1])[:, lb - 1:lb, :]

    ld = lambda ref, h: jnp.where(vrow, ref[:, :, hs(h)].astype(f32).reshape(R, HD_A), 0.0)
    q = [ld(q_ref, h) for h in heads]
    k = [ld(k_ref, h) * (HD_A ** -0.5) for h in heads]
    vb = [ld(v_ref, h).astype(bf16) for h in heads]
    qb = [x.astype(bf16) for x in q]
    qk = [lax.dot_general(qb[h], k[h].astype(bf16), NT_DIMS, preferred_element_type=f32) for h in heads]

    gate = jnp.where(vrow, g_ref[...].reshape(R, LANES), 0.0)
    fcum = _dot_hi(causal.astype(f32), gate)
    gate_t = gate.T
    fcum_t = fcum.T
    vcol = (c * lb + (lax.broadcasted_iota(jnp.int32, (1, R), 1) & (lb - 1))) < t_real

    m_prev, fc, igc, fl, m_row, inter, s = [], [], [], [], [], [], []
    for h in heads:
        fc.append(fcum[:, NH_A + h:NH_A + h + 1])
        fr = fcum_t[NH_A + h:NH_A + h + 1, :]
        igc.append(jnp.where(vrow, gate[:, h:h + 1], NEG_BIG))
        igr = jnp.where(vcol, gate_t[h:h + 1, :], NEG_BIG)
        m_prev.append(ms[:, h, :, 0:1])
        logd = jnp.where(causal, fc[h] - fr + igr, NEG_BIG)
        b_in = fc[h] + per_seq(m_prev[h])
        m_row.append(jnp.maximum(b_in, jnp.max(logd, axis=1, keepdims=True)))
        inter.append(jnp.exp(b_in - m_row[h]))
        s.append(qk[h] * jnp.exp(logd - m_row[h]))
        fl.append(last_row(fc[h]))

    sv = [jnp.dot(s[h].astype(bf16), vb[h], preferred_element_type=f32) for h in heads]
    qc = [jnp.concatenate([jnp.dot(qb[h][rows(j)], cs[j, h].astype(bf16), preferred_element_type=f32)
                           for j in seqs], axis=0) for h in heads]
    for h in heads:
        num = sv[h] + inter[h] * qc[h]
        den = jnp.sum(s[h], axis=1, keepdims=True) \
            + inter[h] * jnp.sum(q[h] * per_seq(ns[:, h]), axis=1, keepdims=True)
        hh = num / jnp.maximum(jnp.abs(den), jnp.exp(-m_row[h]))
        mu = jnp.mean(hh, axis=1, keepdims=True)
        dv = hh - mu
        var = jnp.mean(dv * dv, axis=1, keepdims=True)
        hn = dv * lax.rsqrt(var + LN_EPS) * ng_ref[h]
        out = hn * jax.nn.sigmoid(ld(o_ref, h))
        h_ref[:, :, hs(h)] = out.reshape(nb, lb, HD_A).astype(h_ref.dtype)

    kw, m_new, dec = [], [], []
    for h in heads:
        wj = per_seq(fl[h]) - fc[h] + igc[h]
        wmax = jnp.max(wj.reshape(nb, lb, 1), axis=1, keepdims=True)
        m_new.append(jnp.maximum(fl[h] + m_prev[h], wmax))
        dec.append(jnp.exp(fl[h] + m_prev[h] - m_new[h]))
        kw.append(k[h] * jnp.exp(wj - per_seq(m_new[h])))
    kv = [[lax.dot_general(kw[h][rows(j)].astype(bf16), vb[h][rows(j)], TN_DIMS,
                           preferred_element_type=f32) for j in seqs] for h in heads]
    c_dst, n_dst, m_dst = (cs, ns, ms) if carry else (c_ref, n_ref, m_ref)
    for h in heads:
        n_new = dec[h] * ns[:, h] + jnp.sum(kw[h].reshape(nb, lb, HD_A), axis=1, keepdims=True)
        for j in seqs:
            c_dst[j, h] = dec[h][j] * cs[j, h] + kv[h][j]
        n_dst[:, h] = n_new
        m_dst[:, h] = jnp.broadcast_to(m_new[h], (nb, 1, LANES))

    if carry:
        @pl.when(c == pl.num_programs(1) - 1)
        def _():
            c_ref[...] = cs[...]
            n_ref[...] = ns[...]
            m_ref[...] = ms[...]


def _mlstm(zq, zr, gates, c0, n0, m0, norm_g, t_real, out_dtype, nb_short=8):
    B, T, _ = zq.shape
    if T <= SUBLANES:
        nb, lb = nb_short, T
    else:
        nb, lb = 1, MLSTM_CHUNK
    nc = T // lb
    carry = nc > 1
    assert B % nb == 0 and T % lb == 0 and (carry or t_real == T)
    m0b = jnp.broadcast_to(m0[:, :, None, None], (B, NH_A, 1, LANES))
    zspec = lambda group: pl.BlockSpec((nb, lb, MIX_A), lambda b, c: (b, c, group))
    st_spec = lambda r, w: pl.BlockSpec((nb, NH_A, r, w), lambda b, c: (b, 0, 0, 0))
    st_shapes = [(HD_A, HD_A), (1, HD_A), (1, LANES)]
    outs = pl.pallas_call(
        functools.partial(_mlstm_kernel, t_real=t_real, carry=carry),
        out_shape=[jax.ShapeDtypeStruct((B, T, MIX_A), out_dtype)]
        + [jax.ShapeDtypeStruct((B, NH_A) + s, f32) for s in st_shapes],
        grid=(B // nb, nc),
        in_specs=[zspec(0), zspec(1), zspec(2), zspec(0),
                  pl.BlockSpec((nb, lb, LANES), lambda b, c: (b, c, 0))]
        + [st_spec(*s) for s in st_shapes]
        + [pl.BlockSpec((NH_A, 1, HD_A), lambda b, c: (0, 0, 0))],
        out_specs=[pl.BlockSpec((nb, lb, MIX_A), lambda b, c: (b, c, 0))]
        + [st_spec(*s) for s in st_shapes],
        scratch_shapes=[pltpu.VMEM((nb, NH_A) + s, f32) for s in st_shapes] if carry else [],
        compiler_params=_cparams("parallel", "arbitrary"),
        name="mlstm",
    )(zq, zq, zq, zr, gates, c0, n0.reshape(B, NH_A, 1, HD_A), m0b, norm_g.reshape(NH_A, 1, HD_A))
    hm, c_new, n_new, m_new = outs
    return hm, c_new, n_new.reshape(B, NH_A, HD_A), m_new[:, :, 0, 0]


def _one_minus_exp(x):
    u = jnp.exp(x)
    um1 = u - 1.0
    safe = jnp.where(um1 == 0.0, -x, -(um1 * x) / jnp.log(jnp.where(um1 == 0.0, 0.5, u)))
    return jnp.where(x > -0.5, safe, 1.0 - u)


def _rglru_kernel(xr_ref, gr_ref, conv0_ref, h0_ref, cw_ref, cb_ref, wa_ref, ba_ref, wx_ref,
                  bx_ref, lam_ref, y_ref, hl_ref, xp, a_s, h_s, hc, *, t_real):
    nb, lr = xr_ref.shape[0], xr_ref.shape[1]
    rows = nb * lr
    ngroup = rows // SUBLANES
    c = pl.program_id(1)
    hist = SUBLANES - (CONV_W - 1)

    @pl.when(c == 0)
    def _():
        xp[:, 0:SUBLANES, :] = jnp.zeros((nb, SUBLANES, RG_W), f32)
        xp[:, hist:SUBLANES, :] = conv0_ref[...]
        hc[...] = h0_ref[...]

    xp[:, SUBLANES:SUBLANES + lr, :] = xr_ref[...]
    xc = cb_ref[...]
    for j in range(CONV_W):
        xc = xc + xp[:, hist + j:hist + j + lr, :] * cw_ref[j:j + 1, :]
    xp[:, hist:SUBLANES, :] = xp[:, hist + lr:SUBLANES + lr, :]
    xc = xc.reshape(rows, RG_W)

    xcb = xc.astype(bf16)
    ra = []
    rx = []
    for n in range(RG_BLOCKS):
        xn = xcb[:, n * RG_BW:(n + 1) * RG_BW]
        ra.append(jnp.dot(xn, wa_ref[n].astype(bf16), preferred_element_type=f32))
        rx.append(jnp.dot(xn, wx_ref[n].astype(bf16), preferred_element_type=f32))
    r = jax.nn.sigmoid(jnp.concatenate(ra, axis=1) + ba_ref[...])
    gi = jax.nn.sigmoid(jnp.concatenate(rx, axis=1) + bx_ref[...])
    log_a = -RG_C * r * jax.nn.softplus(-lam_ref[...])
    t_idx = c * lr + lax.broadcasted_iota(jnp.int32, (nb, lr, 1), 1).reshape(rows, 1)
    valid = t_idx < t_real
    a = jnp.where(valid, jnp.exp(log_a), 1.0)
    u = jnp.where(valid, jnp.sqrt(_one_minus_exp(2.0 * log_a)) * (gi * xc), 0.0)

    a = a.reshape(ngroup, SUBLANES, RG_W)
    u = u.reshape(ngroup, SUBLANES, RG_W)
    r8 = lax.broadcasted_iota(jnp.int32, (1, SUBLANES, 1), 1)
    d = 1
    while d < SUBLANES:
        inside = r8 >= d
        a_sh = jnp.where(inside, pltpu.roll(a, d, axis=1), 1.0)
        u_sh = jnp.where(inside, pltpu.roll(u, d, axis=1), 0.0)
        u = a * u_sh + u
        a = a * a_sh
        d *= 2

    if lr == SUBLANES:
        h = u + a * hc[...]
        h_last = h[:, SUBLANES - 1:SUBLANES, :]
        h = h.reshape(rows, RG_W)
    else:
        assert nb == 1
        a_s[...] = a.reshape(rows, RG_W)
        h_s[...] = u.reshape(rows, RG_W)

        def step(g, h_prev):
            i = pl.multiple_of(g * SUBLANES, SUBLANES)
            hg = h_s[pl.ds(i, SUBLANES), :] + a_s[pl.ds(i, SUBLANES), :] * h_prev
            h_s[pl.ds(i, SUBLANES), :] = hg
            return hg[SUBLANES - 1:SUBLANES, :]

        h_last = lax.fori_loop(0, ngroup, step, hc[0]).reshape(1, 1, RG_W)
        h = h_s[...]
    hc[...] = h_last
    y_ref[...] = (h * jax.nn.gelu(gr_ref[...].reshape(rows, RG_W))).reshape(nb, lr, RG_W).astype(y_ref.dtype)

    @pl.when(c == pl.num_programs(1) - 1)
    def _():
        hl_ref[...] = h_last


def _rglru(zr, conv0, h0, conv_w, conv_b, wa, ba, wx, bx, lam, t_real, out_dtype, lr_cap=544, nb_short=16):
    B, T, _ = zr.shape
    if T == SUBLANES:
        nb, lr = nb_short, T
    else:
        nb, lr = 1, _pick_tile(T, lr_cap)
    assert B % nb == 0
    row = lambda a: a.reshape(1, RG_W)
    full = lambda shp: pl.BlockSpec(shp, lambda b, c: (0,) * len(shp))
    y, h_last = pl.pallas_call(
        functools.partial(_rglru_kernel, t_real=t_real),
        out_shape=[jax.ShapeDtypeStruct((B, T, RG_W), out_dtype),
                   jax.ShapeDtypeStruct((B, 1, RG_W), f32)],
        grid=(B // nb, T // lr),
        in_specs=[pl.BlockSpec((nb, lr, RG_W), lambda b, c: (b, c, 1)),
                  pl.BlockSpec((nb, lr, RG_W), lambda b, c: (b, c, 2)),
                  pl.BlockSpec((nb, CONV_W - 1, RG_W), lambda b, c: (b, 0, 0)),
                  pl.BlockSpec((nb, 1, RG_W), lambda b, c: (b, 0, 0)),
                  full((CONV_W, RG_W)), full((1, RG_W)),
                  full((RG_BLOCKS, RG_BW, RG_BW)), full((1, RG_W)),
                  full((RG_BLOCKS, RG_BW, RG_BW)), full((1, RG_W)), full((1, RG_W))],
        out_specs=[pl.BlockSpec((nb, lr, RG_W), lambda b, c: (b, c, 0)),
                   pl.BlockSpec((nb, 1, RG_W), lambda b, c: (b, 0, 0))],
        scratch_shapes=[pltpu.VMEM((nb, SUBLANES + lr, RG_W), f32), pltpu.VMEM((nb * lr, RG_W), f32),
                        pltpu.VMEM((nb * lr, RG_W), f32), pltpu.VMEM((nb, 1, RG_W), f32)],
        compiler_params=_cparams("parallel", "arbitrary"),
        name="rglru",
    )(zr, zr, conv0, h0.reshape(B, 1, RG_W), conv_w, row(conv_b), wa, row(ba), wx, row(bx), row(lam))
    return y, h_last.reshape(B, RG_W)


def _token_shift(x_ref, st_ref, xp_ref, c):
    tt = x_ref.shape[1]
    last = SUBLANES - 1

    @pl.when(c == 0)
    def _():
        xp_ref[:, last:SUBLANES, :] = st_ref[...]

    x = x_ref[...]
    xp_ref[:, SUBLANES:SUBLANES + tt, :] = x
    xprev = xp_ref[:, last:last + tt, :]
    xp_ref[:, last:SUBLANES, :] = xp_ref[:, last + tt:SUBLANES + tt, :]
    return x, xprev


def _seq_tile(B, T, rows_cap):
    if T <= SUBLANES:
        return _pick_tile(B * T, min(rows_cap, SHORT_SEQ_ROWS_CAP)) // T, T
    return 1, _pick_tile(T, rows_cap)


def _rwkv_rkv_kernel(x_ref, st_ref, mu_ref, w_ref, o_ref, xs_ref, xp_ref):
    c = pl.program_id(1)
    s = pl.program_id(2)

    @pl.when(jnp.logical_and(s == 0, pl.program_id(3) == 0))
    def _():
        x, xprev = _token_shift(x_ref, st_ref, xp_ref, c)
        dx = xprev - x
        for si in range(3):
            xs_ref[si] = (x + dx * mu_ref[si]).reshape(xs_ref.shape[1:]).astype(bf16)

    o_ref[...] = jnp.dot(xs_ref[s], w_ref[...], preferred_element_type=f32)


def _rwkv_rkv(x, shift0, mu3, w3, rows_cap=1088, tn=512):
    B, T, K = x.shape
    nb, tt = _seq_tile(B, T, rows_cap)
    rows = nb * tt
    nc = T // tt
    return pl.pallas_call(
        _rwkv_rkv_kernel,
        out_shape=jax.ShapeDtypeStruct((3, B * T, K), f32),
        grid=(B // nb, nc, 3, K // tn),
        in_specs=[pl.BlockSpec((nb, tt, K), lambda b, c, s, j: (b, c, 0), pipeline_mode=pl.Buffered(1)),
                  pl.BlockSpec((nb, 1, K), lambda b, c, s, j: (b, 0, 0)),
                  pl.BlockSpec((3, 1, K), lambda b, c, s, j: (0, 0, 0)),
                  pl.BlockSpec((None, K, tn), lambda b, c, s, j: (s, 0, j))],
        out_specs=pl.BlockSpec((None, rows, tn), lambda b, c, s, j: (s, b * nc + c, j)),
        scratch_shapes=[pltpu.VMEM((3, rows, K), bf16), pltpu.VMEM((nb, SUBLANES + tt, K), f32)],
        compiler_params=_cparams("parallel", "arbitrary", "arbitrary", "arbitrary"),
        name="rwkv_rkv",
    )(x, shift0, mu3, w3)


def _rwkv_lora_kernel(x_ref, st_ref, mu_ref, w1_ref, w2w_ref, w2a_ref, w2g_ref, w0_ref, a0_ref,
                      lw_ref, a_ref, g_ref, xp_ref):
    x, xprev = _token_shift(x_ref, st_ref, xp_ref, pl.program_id(1))
    dx = xprev - x
    mix = lambda i: (x + dx * mu_ref[i]).reshape(lw_ref.shape).astype(bf16)
    r = LORA_PAD
    hw = jnp.tanh(jnp.dot(mix(0), w1_ref[:, 0:r], preferred_element_type=f32))
    ha = jnp.dot(mix(1), w1_ref[:, r:2 * r], preferred_element_type=f32)
    hg = jax.nn.sigmoid(jnp.dot(mix(2), w1_ref[:, 2 * r:], preferred_element_type=f32))
    w = w0_ref[...] + jnp.dot(hw.astype(bf16), w2w_ref[...], preferred_element_type=f32)
    w = -jax.nn.softplus(-w) - 0.5
    lw_ref[...] = -jnp.exp(w)
    a = a0_ref[...] + jnp.dot(ha.astype(bf16), w2a_ref[...], preferred_element_type=f32)
    a_ref[...] = jax.nn.sigmoid(a)
    g_ref[...] = jnp.dot(hg.astype(bf16), w2g_ref[...], preferred_element_type=f32)


def _rwkv_lora(x, shift0, mu3, w1, w2w, w2a, w2g, w0, a0, rows_cap=272):
    B, T, K = x.shape
    nb, tt = _seq_tile(B, T, rows_cap)
    rows = nb * tt
    nc = T // tt
    full = lambda a: pl.BlockSpec(a.shape, lambda b, c: (0,) * a.ndim)
    out = pl.BlockSpec((rows, K), lambda b, c: (b * nc + c, 0))
    return pl.pallas_call(
        _rwkv_lora_kernel,
        out_shape=[jax.ShapeDtypeStruct((B * T, K), f32)] * 3,
        grid=(B // nb, nc),
        in_specs=[pl.BlockSpec((nb, tt, K), lambda b, c: (b, c, 0)),
                  pl.BlockSpec((nb, 1, K), lambda b, c: (b, 0, 0)),
                  full(mu3), full(w1), full(w2w), full(w2a), full(w2g), full(w0), full(a0)],
        out_specs=[out, out, out],
        scratch_shapes=[pltpu.VMEM((nb, SUBLANES + tt, K), f32)],
        compiler_params=_cparams("parallel", "arbitrary"),
        name="rwkv_lora",
    )(x, shift0, mu3, w1, w2w, w2a, w2g, w0, a0)


def _head_sums(x, ones_bd, npair):
    rows = x.shape[0]
    xs = jnp.concatenate([x[:, g * LANES:(g + 1) * LANES] for g in range(npair)], axis=0)
    s = jnp.dot(xs.astype(bf16), ones_bd, preferred_element_type=f32)
    return jnp.concatenate([s[g * rows:(g + 1) * rows] for g in range(npair)], axis=1)


def _rwkv_scan_kernel(r_ref, k_ref, v_ref, lw_ref, ag_ref, g_ref, s0_ref, kk_ref, ka_ref, rk_ref,
                      lg_ref, lb_ref, y_ref, s_ref, sbd, *, t_real, seg, npair):
    R = RWKV_TILE
    R2 = 2 * R
    nseg = R // seg
    W = npair * LANES
    shift = seg.bit_length() - 1
    c = pl.program_id(2)
    mm = lambda a, b: jnp.dot(a.astype(bf16), b.astype(bf16), preferred_element_type=f32)
    mm_nt = lambda a, b: lax.dot_general(a.astype(bf16), b.astype(bf16), NT_DIMS, preferred_element_type=f32)
    mm_tn = lambda a, b: lax.dot_general(a.astype(bf16), b.astype(bf16), TN_DIMS, preferred_element_type=f32)

    @pl.when(c == 0)
    def _():
        sbd[...] = jnp.zeros(sbd.shape, f32)
        for gi in range(npair):
            for j in range(nseg):
                sbd[gi, j, 0:HD_C, 0:HD_C] = s0_ref[j, 2 * gi]
                sbd[gi, j, HD_C:R2, HD_C:R2] = s0_ref[j, 2 * gi + 1]

    row1 = lax.broadcasted_iota(jnp.int32, (R, 1), 0)
    valid = c * seg + (row1 & (seg - 1)) < t_real
    ld = lambda ref: jnp.where(valid, ref[...].reshape(R, W), 0.0)
    r = ld(r_ref)
    k = ld(k_ref)
    v = ld(v_ref)
    lw = ld(lw_ref)
    ag = ld(ag_ref)

    lane = lax.broadcasted_iota(jnp.int32, (1, LANES), 1)
    m0 = (lane < HD_C).astype(f32)
    m1 = 1.0 - m0
    row2 = lax.broadcasted_iota(jnp.int32, (R2, R2), 0)
    col2 = lax.broadcasted_iota(jnp.int32, (R2, R2), 1)
    ones_bd = ((row2 < HD_C) == (col2 < HD_C)).astype(bf16)
    same_blk = (row2 >> shift) == (col2 >> shift)
    strict = jnp.logical_and(same_blk, row2 > col2)
    incl = jnp.logical_and(same_blk, row2 >= col2)
    eye = (row2 == col2).astype(f32)

    kk = k * kk_ref[...]
    kk = kk / jnp.maximum(jnp.sqrt(_head_sums(kk * kk, ones_bd, npair)), 1e-12)
    kp = k * (1.0 + (ag - 1.0) * ka_ref[...])
    bv = kk * ag

    rowR = lax.broadcasted_iota(jnp.int32, (R, R), 0)
    colR = lax.broadcasted_iota(jnp.int32, (R, R), 1)
    tril = jnp.logical_and((rowR >> shift) == (colR >> shift), rowR >= colR).astype(bf16)
    lw_hi, lw_lo = _split_hi_lo(lw)
    cw = (jnp.dot(tril, lw_hi, preferred_element_type=f32)
          + jnp.dot(tril, lw_lo, preferred_element_type=f32))
    cwl = jnp.concatenate(
        [jnp.broadcast_to(cw[j * seg + seg - 1:(j + 1) * seg, :], (seg, W)) for j in range(nseg)], axis=0)
    p_in = jnp.exp(cw)
    p_inv = jnp.exp(-cw)
    p_tail = jnp.exp(cwl - cw)
    at = -kk * jnp.exp(cw - lw)
    rt = r * p_in
    kt = kp * p_inv
    bt = bv * p_inv
    kh = kp * p_tail
    bh = bv * p_tail
    p_last = jnp.exp(cwl)

    def stack(x, gi):
        xg = x[:, gi * LANES:(gi + 1) * LANES]
        return jnp.concatenate([xg * m0, xg * m1], axis=0)

    def pick(x_st, j):
        if nseg == 1:
            return x_st
        return jnp.concatenate([x_st[j * seg:(j + 1) * seg], x_st[R + j * seg:R + (j + 1) * seg]], axis=0)

    def unpick(parts):
        if nseg == 1:
            return parts[0]
        return jnp.concatenate([q[0:seg] for q in parts] + [q[seg:2 * seg] for q in parts], axis=0)

    n_sq = shift - 1
    pairs = range(npair)
    s_prev = [[sbd[gi, j] for j in range(nseg)] for gi in pairs]
    a_st = [stack(at, gi) for gi in pairs]
    r_st = [stack(rt, gi) for gi in pairs]
    v_st = [stack(v, gi) for gi in pairs]
    prod = [mm_nt(jnp.concatenate([a_st[gi], r_st[gi]], axis=0),
                  jnp.concatenate([stack(kt, gi), stack(bt, gi)], axis=0)) for gi in pairs]
    m_ak = [jnp.where(strict, prod[gi][0:R2, 0:R2], 0.0) for gi in pairs]
    a_rk = [jnp.where(incl, prod[gi][R2:2 * R2, 0:R2], 0.0) for gi in pairs]
    a_rb = [jnp.where(incl, prod[gi][R2:2 * R2, R2:2 * R2], 0.0) for gi in pairs]

    x = [jnp.where(strict, prod[gi][0:R2, R2:2 * R2], 0.0) for gi in pairs]
    tinv = [eye + x[gi] for gi in pairs]
    for _ in range(n_sq):
        x = [mm(x[gi], x[gi]) for gi in pairs]
        tinv = [tinv[gi] + mm(tinv[gi], x[gi]) for gi in pairs]

    ar_s = [[mm_nt(jnp.concatenate([pick(a_st[gi], j), pick(r_st[gi], j)], axis=0), s_prev[gi][j])
             for j in range(nseg)] for gi in pairs]
    mv = [mm(m_ak[gi], v_st[gi]) for gi in pairs]
    u_st = [mm(tinv[gi], unpick([q[0:2 * seg] for q in ar_s[gi]]) + mv[gi]) for gi in pairs]
    o_st = [unpick([q[2 * seg:4 * seg] for q in ar_s[gi]])
            + mm(jnp.concatenate([a_rk[gi], a_rb[gi]], axis=1),
                 jnp.concatenate([v_st[gi], u_st[gi]], axis=0)) for gi in pairs]
    for gi in pairs:
        bh_st = stack(bh, gi)
        kh_st = stack(kh, gi)
        for j in range(nseg):
            pl_row = p_last[j * seg:j * seg + 1, gi * LANES:(gi + 1) * LANES]
            upd = mm_tn(jnp.concatenate([pick(u_st[gi], j), pick(v_st[gi], j)], axis=0),
                        jnp.concatenate([pick(bh_st, j), pick(kh_st, j)], axis=0))
            sbd[gi, j] = s_prev[gi][j] * pl_row + upd

    y = jnp.concatenate([o[0:R] + o[R:R2] for o in o_st], axis=1)
    inv_n = 1.0 / HD_C
    sums = _head_sums(jnp.concatenate([y, r * kp * rk_ref[...]], axis=0), ones_bd, npair)
    mu = sums[0:R] * inv_n
    dy = y - mu
    var = _head_sums(dy * dy, ones_bd, npair) * inv_n
    yn = dy * lax.rsqrt(var + GN_EPS) * lg_ref[...] + lb_ref[...]
    out = (yn + sums[R:R2] * v) * g_ref[...].reshape(R, W)
    y_ref[...] = out.reshape(y_ref.shape).astype(y_ref.dtype)

    @pl.when(c == pl.num_programs(2) - 1)
    def _():
        for gi in range(npair):
            for j in range(nseg):
                s_ref[j, 2 * gi] = sbd[gi, j, 0:HD_C, 0:HD_C]
                s_ref[j, 2 * gi + 1] = sbd[gi, j, HD_C:R2, HD_C:R2]


def _rwkv_scan(rkv, lw, ag, g, s0, kk_p, ka_p, rk_p, lnx_g, lnx_b, t_real, out_dtype, npair=16):
    _, B, T, D = rkv.shape
    seg = min(RWKV_TILE, T)
    nseg = RWKV_TILE // seg
    nc = T // seg
    assert nseg == 1 or nc == 1
    assert B % nseg == 0 and T % seg == 0
    W = npair * LANES
    tok = pl.BlockSpec((nseg, seg, W), lambda b, p, c: (b, c, p))
    rkv_spec = lambda s: pl.BlockSpec((None, nseg, seg, W), lambda b, p, c: (s, b, c, p))
    prm = pl.BlockSpec((1, W), lambda b, p, c: (0, p))
    st = pl.BlockSpec((nseg, 2 * npair, HD_C, HD_C), lambda b, p, c: (b, p, 0, 0))
    row = lambda a: a.reshape(1, D)
    return pl.pallas_call(
        functools.partial(_rwkv_scan_kernel, t_real=t_real, seg=seg, npair=npair),
        out_shape=[jax.ShapeDtypeStruct((B, T, D), out_dtype),
                   jax.ShapeDtypeStruct((B, NH_C, HD_C, HD_C), f32)],
        grid=(B // nseg, D // W, nc),
        in_specs=[rkv_spec(0), rkv_spec(1), rkv_spec(2), tok, tok, tok, st, prm, prm, prm, prm, prm],
        out_specs=[tok, st],
        scratch_shapes=[pltpu.VMEM((npair, nseg, 2 * RWKV_TILE, 2 * RWKV_TILE), f32)],
        compiler_params=_cparams("parallel", "parallel", "arbitrary"),
        name="rwkv_scan",
    )(rkv, rkv, rkv, lw, ag, g, s0, row(kk_p), row(ka_p), row(rk_p), row(lnx_g), row(lnx_b))


def _pad_lora(w1, w2):
    rank = w1.shape[1]
    return (jnp.pad(w1, ((0, 0), (0, LORA_PAD - rank))), jnp.pad(w2, ((0, LORA_PAD - rank), (0, 0))))


def _prepare_params(p):
    w_in_t = jnp.swapaxes(p['w_in_ab'][0], 0, 1)
    n_qkvo = 4 * MIX_A
    n_g = 2 * NH_A
    q = {}
    q['w_in'] = jnp.concatenate([w_in_t[:n_qkvo], w_in_t[n_qkvo + n_g:]], axis=0).astype(bf16)
    q['w_gate'] = jnp.stack(_split_hi_lo(jnp.pad(w_in_t[n_qkvo:n_qkvo + n_g], ((0, LANES - n_g), (0, 0)))))
    q['b_gate'] = jnp.pad(p['b_if_ab'][0], (0, LANES - n_g)).reshape(1, LANES)
    q['w_out_ab'] = p['w_out_ab'][0].astype(bf16)
    q['w_up'] = p['w_up'].astype(bf16)
    q['w_down'] = p['w_down'].astype(bf16)
    q['w_rkv'] = jnp.stack([p['rw_wr'][0], p['rw_wk'][0], p['rw_wv'][0]]).astype(bf16)
    mu = p['rw_mu'][0]
    q['mu_rkv'] = jnp.stack([mu[0], mu[2], mu[3]])[:, None, :]
    q['mu_lora'] = jnp.stack([mu[1], mu[4], mu[5]])[:, None, :]
    w1, w2w = _pad_lora(p['rw_w1'][0], p['rw_w2'][0])
    a1, w2a = _pad_lora(p['rw_a1'][0], p['rw_a2'][0])
    q['lora_w1'] = jnp.concatenate([w1, a1, p['rw_g1'][0]], axis=1).astype(bf16)
    q['lora_w2w'] = w2w.astype(bf16)
    q['lora_w2a'] = w2a.astype(bf16)
    q['lora_w2g'] = p['rw_g2'][0].astype(bf16)
    q['w_o'] = p['rw_wo'][0].astype(bf16)
    return q


def _trunk(tok, states, p, q, t_real, out_from=0):
    mC, mn, mm, rgh, rgc, rwS, rwx = states
    B, T, D = tok.batch, tok.t_pad, tok.x.shape[-1]
    M = B * T
    row = lambda a: a.reshape(1, -1)

    act_dtype = bf16 if T % (2 * SUBLANES) == 0 else f32
    zq, zr, gates = _inproj(tok, q['w_in'], q['w_gate'], q['b_gate'], act_dtype)
    zq = zq.reshape(B, T, -1)
    zr = zr.reshape(B, T, -1)
    hm, c_new, n_new, m_new = _mlstm(zq, zr, gates.reshape(B, T, LANES), mC[0], mn[0], mm[0],
                                     p['mlstm_norm_g'][0], t_real, act_dtype)
    yr, h_last = _rglru(zr, rgc[0], rgh[0], p['rg_conv_w'][0], p['rg_conv_b'][0], p['rg_wa'][0],
                        p['rg_ba'][0], p['rg_wx'][0], p['rg_bx'][0], p['rg_lambda'][0], t_real, act_dtype)
    assert t_real >= CONV_W - 1
    conv_new = zr[:, t_real - (CONV_W - 1):t_real, MIX_A:MIX_A + RG_W]
    h1 = _proj_ln([(hm.reshape(M, MIX_A), MIX_A, 0), (yr.reshape(M, RG_W), RG_W, 0)],
                  q['w_out_ab'], tok, row(p['ln1_g'][0]), row(p['ln1_b'][0]))
    h2 = _mlp_ln(h1, q['w_up'], q['w_down'], 0, row(p['ln2_g'][0]), row(p['ln2_b'][0]))

    h2_3 = h2.reshape(B, T, D)
    shift0 = rwx[0].astype(f32)[:, None]
    rkv = _rwkv_rkv(h2_3, shift0, q['mu_rkv'], q['w_rkv'])
    lw, ag, g = _rwkv_lora(h2_3, shift0, q['mu_lora'], q['lora_w1'], q['lora_w2w'], q['lora_w2a'],
                           q['lora_w2g'], row(p['rw_w0'][0]), row(p['rw_a0'][0]))
    r3 = lambda a: a.reshape(B, T, D)
    xo, s_new = _rwkv_scan(rkv.reshape(3, B, T, D), r3(lw), r3(ag), r3(g), rwS[0], p['rw_kk'][0],
                           p['rw_ka'][0], p['rw_rk'][0], p['rw_lnx_g'][0], p['rw_lnx_b'][0], t_real,
                           act_dtype)
    xo = xo.reshape(M, D)
    h3 = _proj_ln([(xo, D, 0)], q['w_o'], h2, row(p['ln1_g'][1]), row(p['ln1_b'][1]))
    n_keep = t_real - out_from
    h4 = _mlp_ln(h3, q['w_up'], q['w_down'], 1, row(p['ln2_g'][1]), row(p['ln2_b'][1]),
                 window=None if (out_from == 0 and t_real == T) else (T, out_from, n_keep))
    shift_new = h2_3[:, t_real - 1]

    new_states = (c_new[None], n_new[None], m_new[None], h_last[None], conv_new[None],
                  s_new[None], shift_new[None])
    return h4.reshape(B, n_keep, D), new_states


def kernel(x_prompt, x_sample, state_mlstm_C, state_mlstm_n, state_mlstm_m, state_rglru_h, state_rglru_conv, state_rwkv_S, state_rwkv_shift, meta_tokens, w_in_ab, b_if_ab, mlstm_norm_g, rg_conv_w, rg_conv_b, rg_wa, rg_ba, rg_wx, rg_bx, rg_lambda, w_out_ab, rw_mu, rw_wr, rw_wk, rw_wv, rw_wo, rw_w0, rw_w1, rw_w2, rw_a0, rw_a1, rw_a2, rw_g1, rw_g2, rw_kk, rw_ka, rw_rk, rw_lnx_g, rw_lnx_b, ln1_g, ln1_b, ln2_g, ln2_b, w_up, w_down):
    p = dict(w_in_ab=w_in_ab, b_if_ab=b_if_ab, mlstm_norm_g=mlstm_norm_g, rg_conv_w=rg_conv_w,
             rg_conv_b=rg_conv_b, rg_wa=rg_wa, rg_ba=rg_ba, rg_wx=rg_wx, rg_bx=rg_bx,
             rg_lambda=rg_lambda, w_out_ab=w_out_ab, rw_mu=rw_mu, rw_wr=rw_wr, rw_wk=rw_wk,
             rw_wv=rw_wv, rw_wo=rw_wo, rw_w0=rw_w0, rw_w1=rw_w1, rw_w2=rw_w2, rw_a0=rw_a0,
             rw_a1=rw_a1, rw_a2=rw_a2, rw_g1=rw_g1, rw_g2=rw_g2, rw_kk=rw_kk, rw_ka=rw_ka,
             rw_rk=rw_rk, rw_lnx_g=rw_lnx_g, rw_lnx_b=rw_lnx_b, ln1_g=ln1_g, ln1_b=ln1_b,
             ln2_g=ln2_g, ln2_b=ln2_b, w_up=w_up, w_down=w_down)
    q = _prepare_params(p)
    B, S, D = x_prompt.shape
    dt = x_prompt.dtype
    t_prompt = N_META + S
    t_pad = -(-t_prompt // MLSTM_CHUNK) * MLSTM_CHUNK
    zero_states = (
        jnp.zeros((1, B, NH_A, HD_A, HD_A), f32),
        jnp.zeros((1, B, NH_A, HD_A), f32),
        jnp.zeros((1, B, NH_A), f32),
        jnp.zeros((1, B, RG_W), f32),
        jnp.zeros((1, B, CONV_W - 1, RG_W), dt),
        jnp.zeros((1, B, NH_C, HD_C, HD_C), f32),
        jnp.zeros((1, B, D), dt),
    )
    hp, ps = _trunk(_Tokens(x_prompt, meta_tokens.astype(dt), t_pad), zero_states, p, q, t_prompt,
                    out_from=N_META)
    sample_states = (state_mlstm_C, state_mlstm_n, state_mlstm_m, state_rglru_h, state_rglru_conv,
                     state_rwkv_S, state_rwkv_shift)
    hs, ss = _trunk(_Tokens(x_sample, None, x_sample.shape[1]), sample_states, p, q, x_sample.shape[1])
    return (hp, hs, *ps, *ss)
```

```python
import functools
from typing import NamedTuple, Optional

import jax
import jax.numpy as jnp
from jax import lax
from jax.experimental import pallas as pl
from jax.experimental.pallas import tpu as pltpu

f32 = jnp.float32
bf16 = jnp.bfloat16
HI = lax.Precision.HIGHEST

D_MODEL = 2048
N_META = 16
MIX_A = 1024
NH_A = 4
HD_A = 256
RG_W = 1024
RG_BLOCKS = 8
RG_BW = 128
CONV_W = 4
RG_C = 8.0
HD_C = 64
NH_C = 32
D_FF = 8192
LN_EPS = 1e-5
GN_EPS = 64e-5
DEPTH = 2
ALPHA = (2.0 * DEPTH) ** 0.25

LANES = 128
SUBLANES = 8
VMEM_LIMIT_BYTES = 56 * 1024 * 1024

MLSTM_CHUNK = 128
RWKV_TILE = 64
SHORT_SEQ_ROWS_CAP = 512
LORA_PAD = 128
NEG_BIG = -1e30

NT_DIMS = (((1,), (1,)), ((), ()))
TN_DIMS = (((0,), (0,)), ((), ()))


class _Tokens(NamedTuple):
    x: jax.Array
    lead: Optional[jax.Array]
    t_pad: int

    @property
    def batch(self):
        return self.x.shape[0]


def _cparams(*sem):
    return pltpu.CompilerParams(dimension_semantics=sem, vmem_limit_bytes=VMEM_LIMIT_BYTES)


def _pick_tile(n, cap):
    best = None
    for d in range(SUBLANES, min(n, cap) + 1, SUBLANES):
        if n % d == 0:
            best = d
    assert best is not None, (n, cap)
    return best


def _dot_bf16(a, b):
    return jnp.dot(a.astype(bf16), b.astype(bf16), preferred_element_type=f32)


def _dot_hi(a, b, dims=None):
    if dims is None:
        return jnp.dot(a, b, precision=HI, preferred_element_type=f32)
    return lax.dot_general(a, b, dims, precision=HI, preferred_element_type=f32)


def _split_hi_lo(x):
    hi = x.astype(bf16)
    return hi, (x - hi.astype(f32)).astype(bf16)


def _layer_norm_rows(t, g, b):
    mu = jnp.mean(t, axis=-1, keepdims=True)
    d = t - mu
    var = jnp.mean(d * d, axis=-1, keepdims=True)
    return d * lax.rsqrt(var + LN_EPS) * g + b


def _window_spec(tm, K, seq_rows, t_pad, n_lead, grid_rank, **kw):
    tps = t_pad // tm
    assert tps >= 2 and tps * tm == t_pad and t_pad - n_lead - seq_rows <= tm

    def index(g, *_):
        start = jnp.clip(tm * (g % tps) - n_lead, 0, seq_rows - tm)
        return (pl.multiple_of((g // tps) * seq_rows + start, SUBLANES), 0)

    assert grid_rank in (1, 2)
    return pl.BlockSpec((pl.Element(tm), pl.Element(K)), index, **kw)


def _virtual_rows(blk, lead_ref, i_seq, tps, n_tail):
    tm = blk.shape[0]
    n_lead = lead_ref.shape[0]
    first = jnp.concatenate([lead_ref[...], blk[:tm - n_lead]], axis=0)
    last = jnp.concatenate([blk[n_tail:], jnp.zeros((n_tail, blk.shape[1]), blk.dtype)], axis=0)
    return jnp.where(i_seq == 0, first, jnp.where(i_seq == tps - 1, last, blk))


def _inproj_kernel(x_ref, w_ref, wg_ref, bg_ref, *rest, virtual):
    if virtual is None:
        o1_ref, o2_ref, g_ref, xb_ref = rest
    else:
        lead_ref, o1_ref, o2_ref, g_ref, xb_ref = rest

    @pl.when(pl.program_id(1) == 0)
    def _():
        x = x_ref[...]
        if virtual is not None:
            tps, n_tail = virtual
            x = _virtual_rows(x, lead_ref, pl.program_id(0) % tps, tps, n_tail)
        x_hi, x_lo = _split_hi_lo(x)
        xb_ref[...] = x_hi
        nt = lambda a, b: lax.dot_general(a, b, NT_DIMS, preferred_element_type=f32)
        pre = (nt(x_hi, wg_ref[0]) + (nt(x_hi, wg_ref[1]) + nt(x_lo, wg_ref[0]))) + bg_ref[...]
        lane = lax.broadcasted_iota(jnp.int32, pre.shape, 1)
        g_ref[...] = jnp.where(lane < NH_A, pre,
                               jnp.where(lane < 2 * NH_A, jax.nn.log_sigmoid(pre), 0.0))

    y = lax.dot_general(xb_ref[...], w_ref[...], NT_DIMS, preferred_element_type=f32)
    j = pl.program_id(1)
    n_first = pl.num_programs(1) // 2

    @pl.when(j < n_first)
    def _():
        o1_ref[...] = y.astype(o1_ref.dtype)

    @pl.when(j >= n_first)
    def _():
        o2_ref[...] = y


def _inproj(tok, w, w_g, b_g, qkv_dtype, tm_cap=1088, tn=1024):
    x = tok.x.reshape(-1, tok.x.shape[-1])
    K = x.shape[1]
    M = tok.batch * tok.t_pad
    n_out = w.shape[0]
    n_half = n_out // 2
    nt = n_half // tn
    if tok.lead is None:
        tm = _pick_tile(M, tm_cap)
        x_specs, x_args, virtual = [pl.BlockSpec((tm, K), lambda i, j: (i, 0))], [x], None
    else:
        tm = _pick_tile(tok.t_pad, tm_cap)
        n_lead = tok.lead.shape[0]
        x_specs = [_window_spec(tm, K, tok.x.shape[1], tok.t_pad, n_lead, 2)]
        x_args = [x]
        virtual = (tok.t_pad // tm, tok.t_pad - n_lead - tok.x.shape[1])
    lead_specs = [] if tok.lead is None else [pl.BlockSpec(tok.lead.shape, lambda i, j: (0, 0))]
    lead_args = [] if tok.lead is None else [tok.lead]
    return pl.pallas_call(
        functools.partial(_inproj_kernel, virtual=virtual),
        out_shape=[jax.ShapeDtypeStruct((M, n_half), qkv_dtype), jax.ShapeDtypeStruct((M, n_half), f32),
                   jax.ShapeDtypeStruct((M, LANES), f32)],
        grid=(M // tm, n_out // tn),
        in_specs=x_specs + [pl.BlockSpec((tn, K), lambda i, j: (j, 0)),
                            pl.BlockSpec((2, LANES, K), lambda i, j: (0, 0, 0)),
                            pl.BlockSpec((1, LANES), lambda i, j: (0, 0))] + lead_specs,
        out_specs=[pl.BlockSpec((tm, tn), lambda i, j: (i, jnp.minimum(j, nt - 1))),
                   pl.BlockSpec((tm, tn), lambda i, j: (i, jnp.maximum(j - nt, 0))),
                   pl.BlockSpec((tm, LANES), lambda i, j: (i, 0))],
        scratch_shapes=[pltpu.VMEM((tm, K), bf16)],
        compiler_params=_cparams("parallel", "arbitrary"),
        name="inproj",
    )(*x_args, w, w_g, b_g, *lead_args)


def _proj_ln_kernel(*refs, n_x, virtual):
    x_refs = refs[:n_x]
    if virtual is None:
        w_ref, res_ref, g_ref, b_ref, o_ref = refs[n_x:]
        res = res_ref[...]
    else:
        w_ref, res_ref, g_ref, b_ref, lead_ref, o_ref = refs[n_x:]
        tps, n_tail = virtual
        res = _virtual_rows(res_ref[...], lead_ref, pl.program_id(0) % tps, tps, n_tail)
    y = None
    k0 = 0
    for x_ref in x_refs:
        kc = x_ref.shape[1]
        part = _dot_bf16(x_ref[...], w_ref[k0:k0 + kc, :])
        y = part if y is None else y + part
        k0 += kc
    t = ALPHA * res + y
    o_ref[...] = _layer_norm_rows(t, g_ref[...], b_ref[...])


def _proj_ln(xs, w, res, g, b, tm_cap=272):
    if isinstance(res, _Tokens) and res.lead is not None:
        tok = res
        N = tok.x.shape[-1]
        M = tok.batch * tok.t_pad
        tm = _pick_tile(tok.t_pad, tm_cap)
        n_lead = tok.lead.shape[0]
        res = tok.x.reshape(-1, N)
        res_spec = _window_spec(tm, N, tok.x.shape[1], tok.t_pad, n_lead, 1)
        virtual = (tok.t_pad // tm, tok.t_pad - n_lead - tok.x.shape[1])
        lead_specs, lead_args = [pl.BlockSpec(tok.lead.shape, lambda i: (0, 0))], [tok.lead]
    else:
        if isinstance(res, _Tokens):
            res = res.x.reshape(-1, res.x.shape[-1])
        M, N = res.shape
        tm = _pick_tile(M, tm_cap)
        res_spec = pl.BlockSpec((tm, N), lambda i: (i, 0))
        virtual, lead_specs, lead_args = None, [], []
    in_specs = []
    args = []
    for arr, width, blk in xs:
        in_specs.append(pl.BlockSpec((tm, width), functools.partial(lambda i, blk: (i, blk), blk=blk)))
        args.append(arr)
    K = w.shape[0]
    in_specs += [pl.BlockSpec((K, N), lambda i: (0, 0)),
                 res_spec,
                 pl.BlockSpec((1, N), lambda i: (0, 0)),
                 pl.BlockSpec((1, N), lambda i: (0, 0))] + lead_specs
    return pl.pallas_call(
        functools.partial(_proj_ln_kernel, n_x=len(xs), virtual=virtual),
        out_shape=jax.ShapeDtypeStruct((M, N), f32),
        grid=(M // tm,),
        in_specs=in_specs,
        out_specs=pl.BlockSpec((tm, N), lambda i: (i, 0)),
        compiler_params=_cparams("parallel"),
        name="proj_ln",
    )(*args, w, res, g, b, *lead_args)


def _mlp_ln_kernel(x_ref, wu_ref, wd_ref, g_ref, b_ref, o_ref, xb_ref):
    f = pl.program_id(1)

    @pl.when(f == 0)
    def _():
        xb_ref[...] = x_ref[...].astype(bf16)
        o_ref[...] = jnp.zeros_like(o_ref)

    a = jnp.dot(xb_ref[...], wu_ref[...], preferred_element_type=f32)
    a = jnp.square(jnp.maximum(a, 0.0))
    o_ref[...] += jnp.dot(a.astype(bf16), wd_ref[...], preferred_element_type=f32)

    @pl.when(f == pl.num_programs(1) - 1)
    def _():
        t = ALPHA * x_ref[...] + o_ref[...]
        o_ref[...] = _layer_norm_rows(t, g_ref[...], b_ref[...])


def _mlp_ln(x, w_up, w_down, layer, g, b, tm_cap=1088, tf=512, window=None):
    M, K = x.shape
    F = w_up.shape[2]
    if window is None:
        tm = _pick_tile(M, tm_cap)
        x_spec = pl.BlockSpec((tm, K), lambda i, j: (i, 0), pipeline_mode=pl.Buffered(1))
    else:
        seq_rows, first, count = window
        assert first % SUBLANES == 0 and seq_rows % SUBLANES == 0
        tm = _pick_tile(count, tm_cap)
        per_seq = count // tm
        M = (M // seq_rows) * count
        x_spec = pl.BlockSpec(
            (pl.Element(tm), pl.Element(K)),
            lambda i, j: (pl.multiple_of((i // per_seq) * seq_rows + first + (i % per_seq) * tm, SUBLANES), 0),
            pipeline_mode=pl.Buffered(1))
    return pl.pallas_call(
        _mlp_ln_kernel,
        out_shape=jax.ShapeDtypeStruct((M, K), f32),
        grid=(M // tm, F // tf),
        in_specs=[x_spec,
                  pl.BlockSpec((None, K, tf), lambda i, j: (layer, 0, j)),
                  pl.BlockSpec((None, tf, K), lambda i, j: (layer, j, 0)),
                  pl.BlockSpec((1, K), lambda i, j: (0, 0)),
                  pl.BlockSpec((1, K), lambda i, j: (0, 0))],
        out_specs=pl.BlockSpec((tm, K), lambda i, j: (i, 0)),
        scratch_shapes=[pltpu.VMEM((tm, K), bf16)],
        compiler_params=_cparams("parallel", "arbitrary"),
        name="mlp_ln",
    )(x, w_up, w_down, g, b)


def _mlstm_kernel(q_ref, k_ref, v_ref, o_ref, g_ref, c0_ref, n0_ref, m0_ref, ng_ref,
                  h_ref, c_ref, n_ref, m_ref, *scratch, t_real, carry):
    nb, lb = q_ref.shape[0], q_ref.shape[1]
    R = nb * lb
    c = pl.program_id(1)
    if carry:
        cs, ns, ms = scratch

        @pl.when(c == 0)
        def _():
            cs[...] = c0_ref[...]
            ns[...] = n0_ref[...]
            ms[...] = m0_ref[...]
    else:
        cs, ns, ms = c0_ref, n0_ref, m0_ref

    heads = range(NH_A)
    seqs = range(nb)
    hs = lambda h: slice(h * HD_A, (h + 1) * HD_A)
    rows = lambda j: slice(j * lb, (j + 1) * lb)
    t_col = c * lb + lax.broadcasted_iota(jnp.int32, (nb, lb, 1), 1).reshape(R, 1)
    vrow = t_col < t_real
    rowi = lax.broadcasted_iota(jnp.int32, (R, R), 0)
    coli = lax.broadcasted_iota(jnp.int32, (R, R), 1)
    if nb == 1:
        causal = rowi >= coli
    else:
        shift = lb.bit_length() - 1
        causal = jnp.logical_and((rowi >> shift) == (coli >> shift), rowi >= coli)

    def per_seq(x):
        return jnp.broadcast_to(x, (nb, lb, x.shape[-1])).reshape(R, x.shape[-1])

    def last_row(x):
        return x.reshape(nb, lb, x.shape[-1])[:, lb - 1:lb, :]

    ld = lambda ref, h: jnp.where(vrow, ref[:, :, hs(h)].astype(f32).reshape(R, HD_A), 0.0)
    q = [ld(q_ref, h) for h in heads]
    k = [ld(k_ref, h) * (HD_A ** -0.5) for h in heads]
    vb = [ld(v_ref, h).astype(bf16) for h in heads]
    qb = [x.astype(bf16) for x in q]
    qk = [lax.dot_general(qb[h], k[h].astype(bf16), NT_DIMS, preferred_element_type=f32) for h in heads]

    gate = jnp.where(vrow, g_ref[...].reshape(R, LANES), 0.0)
    fcum = _dot_hi(causal.astype(f32), gate)
    gate_t = gate.T
    fcum_t = fcum.T
    vcol = (c * lb + (lax.broadcasted_iota(jnp.int32, (1, R), 1) & (lb - 1))) < t_real

    m_prev, fc, igc, fl, m_row, inter, s = [], [], [], [], [], [], []
    for h in heads:
        fc.append(fcum[:, NH_A + h:NH_A + h + 1])
        fr = fcum_t[NH_A + h:NH_A + h + 1, :]
        igc.append(jnp.where(vrow, gate[:, h:h + 1], NEG_BIG))
        igr = jnp.where(vcol, gate_t[h:h + 1, :], NEG_BIG)
        m_prev.append(ms[:, h, :, 0:1])
        logd = jnp.where(causal, fc[h] - fr + igr, NEG_BIG)
        b_in = fc[h] + per_seq(m_prev[h])
        m_row.append(jnp.maximum(b_in, jnp.max(logd, axis=1, keepdims=True)))
        inter.append(jnp.exp(b_in - m_row[h]))
        s.append(qk[h] * jnp.exp(logd - m_row[h]))
        fl.append(last_row(fc[h]))

    sv = [jnp.dot(s[h].astype(bf16), vb[h], preferred_element_type=f32) for h in heads]
    qc = [jnp.concatenate([jnp.dot(qb[h][rows(j)], cs[j, h].astype(bf16), preferred_element_type=f32)
                           for j in seqs], axis=0) for h in heads]
    for h in heads:
        num = sv[h] + inter[h] * qc[h]
        den = jnp.sum(s[h], axis=1, keepdims=True) \
            + inter[h] * jnp.sum(q[h] * per_seq(ns[:, h]), axis=1, keepdims=True)
        hh = num / jnp.maximum(jnp.abs(den), jnp.exp(-m_row[h]))
        mu = jnp.mean(hh, axis=1, keepdims=True)
        dv = hh - mu
        var = jnp.mean(dv * dv, axis=1, keepdims=True)
        hn = dv * lax.rsqrt(var + LN_EPS) * ng_ref[h]
        out = hn * jax.nn.sigmoid(ld(o_ref, h))
        h_ref[:, :, hs(h)] = out.reshape(nb, lb, HD_A).astype(h_ref.dtype)

    kw, m_new, dec = [], [], []
    for h in heads:
        wj = per_seq(fl[h]) - fc[h] + igc[h]
        wmax = jnp.max(wj.reshape(nb, lb, 1), axis=1, keepdims=True)
        m_new.append(jnp.maximum(fl[h] + m_prev[h], wmax))
        dec.append(jnp.exp(fl[h] + m_prev[h] - m_new[h]))
        kw.append(k[h] * jnp.exp(wj - per_seq(m_new[h])))
    kv = [[lax.dot_general(kw[h][rows(j)].astype(bf16), vb[h][rows(j)], TN_DIMS,
                           preferred_element_type=f32) for j in seqs] for h in heads]
    c_dst, n_dst, m_dst = (cs, ns, ms) if carry else (c_ref, n_ref, m_ref)
    for h in heads:
        n_new = dec[h] * ns[:, h] + jnp.sum(kw[h].reshape(nb, lb, HD_A), axis=1, keepdims=True)
        for j in seqs:
            c_dst[j, h] = dec[h][j] * cs[j, h] + kv[h][j]
        n_dst[:, h] = n_new
        m_dst[:, h] = jnp.broadcast_to(m_new[h], (nb, 1, LANES))

    if carry:
        @pl.when(c == pl.num_programs(1) - 1)
        def _():
            c_ref[...] = cs[...]
            n_ref[...] = ns[...]
            m_ref[...] = ms[...]


def _mlstm(zq, zr, gates, c0, n0, m0, norm_g, t_real, out_dtype, nb_short=8):
    B, T, _ = zq.shape
    if T <= SUBLANES:
        nb, lb = nb_short, T
    else:
        nb, lb = 1, MLSTM_CHUNK
    nc = T // lb
    carry = nc > 1
    assert B % nb == 0 and T % lb == 0 and (carry or t_real == T)
    m0b = jnp.broadcast_to(m0[:, :, None, None], (B, NH_A, 1, LANES))
    zspec = lambda group: pl.BlockSpec((nb, lb, MIX_A), lambda b, c: (b, c, group))
    st_spec = lambda r, w: pl.BlockSpec((nb, NH_A, r, w), lambda b, c: (b, 0, 0, 0))
    st_shapes = [(HD_A, HD_A), (1, HD_A), (1, LANES)]
    outs = pl.pallas_call(
        functools.partial(_mlstm_kernel, t_real=t_real, carry=carry),
        out_shape=[jax.ShapeDtypeStruct((B, T, MIX_A), out_dtype)]
        + [jax.ShapeDtypeStruct((B, NH_A) + s, f32) for s in st_shapes],
        grid=(B // nb, nc),
        in_specs=[zspec(0), zspec(1), zspec(2), zspec(0),
                  pl.BlockSpec((nb, lb, LANES), lambda b, c: (b, c, 0))]
        + [st_spec(*s) for s in st_shapes]
        + [pl.BlockSpec((NH_A, 1, HD_A), lambda b, c: (0, 0, 0))],
        out_specs=[pl.BlockSpec((nb, lb, MIX_A), lambda b, c: (b, c, 0))]
        + [st_spec(*s) for s in st_shapes],
        scratch_shapes=[pltpu.VMEM((nb, NH_A) + s, f32) for s in st_shapes] if carry else [],
        compiler_params=_cparams("parallel", "arbitrary"),
        name="mlstm",
    )(zq, zq, zq, zr, gates, c0, n0.reshape(B, NH_A, 1, HD_A), m0b, norm_g.reshape(NH_A, 1, HD_A))
    hm, c_new, n_new, m_new = outs
    return hm, c_new, n_new.reshape(B, NH_A, HD_A), m_new[:, :, 0, 0]


def _one_minus_exp(x):
    u = jnp.exp(x)
    um1 = u - 1.0
    safe = jnp.where(um1 == 0.0, -x, -(um1 * x) / jnp.log(jnp.where(um1 == 0.0, 0.5, u)))
    return jnp.where(x > -0.5, safe, 1.0 - u)


def _rglru_kernel(xr_ref, gr_ref, conv0_ref, h0_ref, cw_ref, cb_ref, wa_ref, ba_ref, wx_ref,
                  bx_ref, lam_ref, y_ref, hl_ref, xp, a_s, h_s, hc, *, t_real):
    nb, lr = xr_ref.shape[0], xr_ref.shape[1]
    rows = nb * lr
    ngroup = rows // SUBLANES
    c = pl.program_id(1)
    hist = SUBLANES - (CONV_W - 1)

    @pl.when(c == 0)
    def _():
        xp[:, 0:SUBLANES, :] = jnp.zeros((nb, SUBLANES, RG_W), f32)
        xp[:, hist:SUBLANES, :] = conv0_ref[...]
        hc[...] = h0_ref[...]

    xp[:, SUBLANES:SUBLANES + lr, :] = xr_ref[...]
    xc = cb_ref[...]
    for j in range(CONV_W):
        xc = xc + xp[:, hist + j:hist + j + lr, :] * cw_ref[j:j + 1, :]
    xp[:, hist:SUBLANES, :] = xp[:, hist + lr:SUBLANES + lr, :]
    xc = xc.reshape(rows, RG_W)

    xcb = xc.astype(bf16)
    ra = []
    rx = []
    for n in range(RG_BLOCKS):
        xn = xcb[:, n * RG_BW:(n + 1) * RG_BW]
        ra.append(jnp.dot(xn, wa_ref[n].astype(bf16), preferred_element_type=f32))
        rx.append(jnp.dot(xn, wx_ref[n].astype(bf16), preferred_element_type=f32))
    r = jax.nn.sigmoid(jnp.concatenate(ra, axis=1) + ba_ref[...])
    gi = jax.nn.sigmoid(jnp.concatenate(rx, axis=1) + bx_ref[...])
    log_a = -RG_C * r * jax.nn.softplus(-lam_ref[...])
    t_idx = c * lr + lax.broadcasted_iota(jnp.int32, (nb, lr, 1), 1).reshape(rows, 1)
    valid = t_idx < t_real
    a = jnp.where(valid, jnp.exp(log_a), 1.0)
    u = jnp.where(valid, jnp.sqrt(_one_minus_exp(2.0 * log_a)) * (gi * xc), 0.0)

    a = a.reshape(ngroup, SUBLANES, RG_W)
    u = u.reshape(ngroup, SUBLANES, RG_W)
    r8 = lax.broadcasted_iota(jnp.int32, (1, SUBLANES, 1), 1)
    d = 1
    while d < SUBLANES:
        inside = r8 >= d
        a_sh = jnp.where(inside, pltpu.roll(a, d, axis=1), 1.0)
        u_sh = jnp.where(inside, pltpu.roll(u, d, axis=1), 0.0)
        u = a * u_sh + u
        a = a * a_sh
        d *= 2

    if lr == SUBLANES:
        h = u + a * hc[...]
        h_last = h[:, SUBLANES - 1:SUBLANES, :]
        h = h.reshape(rows, RG_W)
    else:
        assert nb == 1
        a_s[...] = a.reshape(rows, RG_W)
        h_s[...] = u.reshape(rows, RG_W)

        def step(g, h_prev):
            i = pl.multiple_of(g * SUBLANES, SUBLANES)
            hg = h_s[pl.ds(i, SUBLANES), :] + a_s[pl.ds(i, SUBLANES), :] * h_prev
            h_s[pl.ds(i, SUBLANES), :] = hg
            return hg[SUBLANES - 1:SUBLANES, :]

        h_last = lax.fori_loop(0, ngroup, step, hc[0]).reshape(1, 1, RG_W)
        h = h_s[...]
    hc[...] = h_last
    y_ref[...] = (h * jax.nn.gelu(gr_ref[...].reshape(rows, RG_W))).reshape(nb, lr, RG_W).astype(y_ref.dtype)

    @pl.when(c == pl.num_programs(1) - 1)
    def _():
        hl_ref[...] = h_last


def _rglru(zr, conv0, h0, conv_w, conv_b, wa, ba, wx, bx, lam, t_real, out_dtype, lr_cap=544, nb_short=16):
    B, T, _ = zr.shape
    if T == SUBLANES:
        nb, lr = nb_short, T
    else:
        nb, lr = 1, _pick_tile(T, lr_cap)
    assert B % nb == 0
    row = lambda a: a.reshape(1, RG_W)
    full = lambda shp: pl.BlockSpec(shp, lambda b, c: (0,) * len(shp))
    y, h_last = pl.pallas_call(
        functools.partial(_rglru_kernel, t_real=t_real),
        out_shape=[jax.ShapeDtypeStruct((B, T, RG_W), out_dtype),
                   jax.ShapeDtypeStruct((B, 1, RG_W), f32)],
        grid=(B // nb, T // lr),
        in_specs=[pl.BlockSpec((nb, lr, RG_W), lambda b, c: (b, c, 1)),
                  pl.BlockSpec((nb, lr, RG_W), lambda b, c: (b, c, 2)),
                  pl.BlockSpec((nb, CONV_W - 1, RG_W), lambda b, c: (b, 0, 0)),
                  pl.BlockSpec((nb, 1, RG_W), lambda b, c: (b, 0, 0)),
                  full((CONV_W, RG_W)), full((1, RG_W)),
                  full((RG_BLOCKS, RG_BW, RG_BW)), full((1, RG_W)),
                  full((RG_BLOCKS, RG_BW, RG_BW)), full((1, RG_W)), full((1, RG_W))],
        out_specs=[pl.BlockSpec((nb, lr, RG_W), lambda b, c: (b, c, 0)),
                   pl.BlockSpec((nb, 1, RG_W), lambda b, c: (b, 0, 0))],
        scratch_shapes=[pltpu.VMEM((nb, SUBLANES + lr, RG_W), f32), pltpu.VMEM((nb * lr, RG_W), f32),
                        pltpu.VMEM((nb * lr, RG_W), f32), pltpu.VMEM((nb, 1, RG_W), f32)],
        compiler_params=_cparams("parallel", "arbitrary"),
        name="rglru",
    )(zr, zr, conv0, h0.reshape(B, 1, RG_W), conv_w, row(conv_b), wa, row(ba), wx, row(bx), row(lam))
    return y, h_last.reshape(B, RG_W)


def _token_shift(x_ref, st_ref, xp_ref, c):
    tt = x_ref.shape[1]
    last = SUBLANES - 1

    @pl.when(c == 0)
    def _():
        xp_ref[:, last:SUBLANES, :] = st_ref[...]

    x = x_ref[...]
    xp_ref[:, SUBLANES:SUBLANES + tt, :] = x
    xprev = xp_ref[:, last:last + tt, :]
    xp_ref[:, last:SUBLANES, :] = xp_ref[:, last + tt:SUBLANES + tt, :]
    return x, xprev


def _seq_tile(B, T, rows_cap):
    if T <= SUBLANES:
        return _pick_tile(B * T, min(rows_cap, SHORT_SEQ_ROWS_CAP)) // T, T
    return 1, _pick_tile(T, rows_cap)


def _rwkv_rkv_kernel(x_ref, st_ref, mu_ref, w_ref, o_ref, xs_ref, xp_ref):
    c = pl.program_id(1)
    s = pl.program_id(2)

    @pl.when(jnp.logical_and(s == 0, pl.program_id(3) == 0))
    def _():
        x, xprev = _token_shift(x_ref, st_ref, xp_ref, c)
        dx = xprev - x
        for si in range(3):
            xs_ref[si] = (x + dx * mu_ref[si]).reshape(xs_ref.shape[1:]).astype(bf16)

    o_ref[...] = jnp.dot(xs_ref[s], w_ref[...], preferred_element_type=f32)


def _rwkv_rkv(x, shift0, mu3, w3, rows_cap=1088, tn=512):
    B, T, K = x.shape
    nb, tt = _seq_tile(B, T, rows_cap)
    rows = nb * tt
    nc = T // tt
    return pl.pallas_call(
        _rwkv_rkv_kernel,
        out_shape=jax.ShapeDtypeStruct((3, B * T, K), f32),
        grid=(B // nb, nc, 3, K // tn),
        in_specs=[pl.BlockSpec((nb, tt, K), lambda b, c, s, j: (b, c, 0), pipeline_mode=pl.Buffered(1)),
                  pl.BlockSpec((nb, 1, K), lambda b, c, s, j: (b, 0, 0)),
                  pl.BlockSpec((3, 1, K), lambda b, c, s, j: (0, 0, 0)),
                  pl.BlockSpec((None, K, tn), lambda b, c, s, j: (s, 0, j))],
        out_specs=pl.BlockSpec((None, rows, tn), lambda b, c, s, j: (s, b * nc + c, j)),
        scratch_shapes=[pltpu.VMEM((3, rows, K), bf16), pltpu.VMEM((nb, SUBLANES + tt, K), f32)],
        compiler_params=_cparams("parallel", "arbitrary", "arbitrary", "arbitrary"),
        name="rwkv_rkv",
    )(x, shift0, mu3, w3)


def _rwkv_lora_kernel(x_ref, st_ref, mu_ref, w1_ref, w2w_ref, w2a_ref, w2g_ref, w0_ref, a0_ref,
                      lw_ref, a_ref, g_ref, xp_ref):
    x, xprev = _token_shift(x_ref, st_ref, xp_ref, pl.program_id(1))
    dx = xprev - x
    mix = lambda i: (x + dx * mu_ref[i]).reshape(lw_ref.shape).astype(bf16)
    r = LORA_PAD
    hw = jnp.tanh(jnp.dot(mix(0), w1_ref[:, 0:r], preferred_element_type=f32))
    ha = jnp.dot(mix(1), w1_ref[:, r:2 * r], preferred_element_type=f32)
    hg = jax.nn.sigmoid(jnp.dot(mix(2), w1_ref[:, 2 * r:], preferred_element_type=f32))
    w = w0_ref[...] + jnp.dot(hw.astype(bf16), w2w_ref[...], preferred_element_type=f32)
    w = -jax.nn.softplus(-w) - 0.5
    lw_ref[...] = -jnp.exp(w)
    a = a0_ref[...] + jnp.dot(ha.astype(bf16), w2a_ref[...], preferred_element_type=f32)
    a_ref[...] = jax.nn.sigmoid(a)
    g_ref[...] = jnp.dot(hg.astype(bf16), w2g_ref[...], preferred_element_type=f32)


def _rwkv_lora(x, shift0, mu3, w1, w2w, w2a, w2g, w0, a0, rows_cap=272):
    B, T, K = x.shape
    nb, tt = _seq_tile(B, T, rows_cap)
    rows = nb * tt
    nc = T // tt
    full = lambda a: pl.BlockSpec(a.shape, lambda b, c: (0,) * a.ndim)
    out = pl.BlockSpec((rows, K), lambda b, c: (b * nc + c, 0))
    return pl.pallas_call(
        _rwkv_lora_kernel,
        out_shape=[jax.ShapeDtypeStruct((B * T, K), f32)] * 3,
        grid=(B // nb, nc),
        in_specs=[pl.BlockSpec((nb, tt, K), lambda b, c: (b, c, 0)),
                  pl.BlockSpec((nb, 1, K), lambda b, c: (b, 0, 0)),
                  full(mu3), full(w1), full(w2w), full(w2a), full(w2g), full(w0), full(a0)],
        out_specs=[out, out, out],
        scratch_shapes=[pltpu.VMEM((nb, SUBLANES + tt, K), f32)],
        compiler_params=_cparams("parallel", "arbitrary"),
        name="rwkv_lora",
    )(x, shift0, mu3, w1, w2w, w2a, w2g, w0, a0)


def _head_sums(x, ones_bd, npair):
    rows = x.shape[0]
    xs = jnp.concatenate([x[:, g * LANES:(g + 1) * LANES] for g in range(npair)], axis=0)
    s = jnp.dot(xs.astype(bf16), ones_bd, preferred_element_type=f32)
    return jnp.concatenate([s[g * rows:(g + 1) * rows] for g in range(npair)], axis=1)


def _rwkv_scan_kernel(r_ref, k_ref, v_ref, lw_ref, ag_ref, g_ref, s0_ref, kk_ref, ka_ref, rk_ref,
                      lg_ref, lb_ref, y_ref, s_ref, sbd, *, t_real, seg, npair):
    R = RWKV_TILE
    R2 = 2 * R
    nseg = R // seg
    W = npair * LANES
    shift = seg.bit_length() - 1
    c = pl.program_id(2)
    mm = lambda a, b: jnp.dot(a.astype(bf16), b.astype(bf16), preferred_element_type=f32)
    mm_nt = lambda a, b: lax.dot_general(a.astype(bf16), b.astype(bf16), NT_DIMS, preferred_element_type=f32)
    mm_tn = lambda a, b: lax.dot_general(a.astype(bf16), b.astype(bf16), TN_DIMS, preferred_element_type=f32)

    @pl.when(c == 0)
    def _():
        sbd[...] = jnp.zeros(sbd.shape, f32)
        for gi in range(npair):
            for j in range(nseg):
                sbd[gi, j, 0:HD_C, 0:HD_C] = s0_ref[j, 2 * gi]
                sbd[gi, j, HD_C:R2, HD_C:R2] = s0_ref[j, 2 * gi + 1]

    row1 = lax.broadcasted_iota(jnp.int32, (R, 1), 0)
    valid = c * seg + (row1 & (seg - 1)) < t_real
    ld = lambda ref: jnp.where(valid, ref[...].reshape(R, W), 0.0)
    r = ld(r_ref)
    k = ld(k_ref)
    v = ld(v_ref)
    lw = ld(lw_ref)
    ag = ld(ag_ref)

    lane = lax.broadcasted_iota(jnp.int32, (1, LANES), 1)
    m0 = (lane < HD_C).astype(f32)
    m1 = 1.0 - m0
    row2 = lax.broadcasted_iota(jnp.int32, (R2, R2), 0)
    col2 = lax.broadcasted_iota(jnp.int32, (R2, R2), 1)
    ones_bd = ((row2 < HD_C) == (col2 < HD_C)).astype(bf16)
    same_blk = (row2 >> shift) == (col2 >> shift)
    strict = jnp.logical_and(same_blk, row2 > col2)
    incl = jnp.logical_and(same_blk, row2 >= col2)
    eye = (row2 == col2).astype(f32)

    kk = k * kk_ref[...]
    kk = kk / jnp.maximum(jnp.sqrt(_head_sums(kk * kk, ones_bd, npair)), 1e-12)
    kp = k * (1.0 + (ag - 1.0) * ka_ref[...])
    bv = kk * ag

    rowR = lax.broadcasted_iota(jnp.int32, (R, R), 0)
    colR = lax.broadcasted_iota(jnp.int32, (R, R), 1)
    tril = jnp.logical_and((rowR >> shift) == (colR >> shift), rowR >= colR).astype(bf16)
    lw_hi, lw_lo = _split_hi_lo(lw)
    cw = (jnp.dot(tril, lw_hi, preferred_element_type=f32)
          + jnp.dot(tril, lw_lo, preferred_element_type=f32))
    cwl = jnp.concatenate(
        [jnp.broadcast_to(cw[j * seg + seg - 1:(j + 1) * seg, :], (seg, W)) for j in range(nseg)], axis=0)
    p_in = jnp.exp(cw)
    p_inv = jnp.exp(-cw)
    p_tail = jnp.exp(cwl - cw)
    at = -kk * jnp.exp(cw - lw)
    rt = r * p_in
    kt = kp * p_inv
    bt = bv * p_inv
    kh = kp * p_tail
    bh = bv * p_tail
    p_last = jnp.exp(cwl)

    def stack(x, gi):
        xg = x[:, gi * LANES:(gi + 1) * LANES]
        return jnp.concatenate([xg * m0, xg * m1], axis=0)

    def pick(x_st, j):
        if nseg == 1:
            return x_st
        return jnp.concatenate([x_st[j * seg:(j + 1) * seg], x_st[R + j * seg:R + (j + 1) * seg]], axis=0)

    def unpick(parts):
        if nseg == 1:
            return parts[0]
        return jnp.concatenate([q[0:seg] for q in parts] + [q[seg:2 * seg] for q in parts], axis=0)

    n_sq = shift - 1
    pairs = range(npair)
    s_prev = [[sbd[gi, j] for j in range(nseg)] for gi in pairs]
    a_st = [stack(at, gi) for gi in pairs]
    r_st = [stack(rt, gi) for gi in pairs]
    v_st = [stack(v, gi) for gi in pairs]
    prod = [mm_nt(jnp.concatenate([a_st[gi], r_st[gi]], axis=0),
                  jnp.concatenate([stack(kt, gi), stack(bt, gi)], axis=0)) for gi in pairs]
    m_ak = [jnp.where(strict, prod[gi][0:R2, 0:R2], 0.0) for gi in pairs]
    a_rk = [jnp.where(incl, prod[gi][R2:2 * R2, 0:R2], 0.0) for gi in pairs]
    a_rb = [jnp.where(incl, prod[gi][R2:2 * R2, R2:2 * R2], 0.0) for gi in pairs]

    x = [jnp.where(strict, prod[gi][0:R2, R2:2 * R2], 0.0) for gi in pairs]
    tinv = [eye + x[gi] for gi in pairs]
    for _ in range(n_sq):
        x = [mm(x[gi], x[gi]) for gi in pairs]
        tinv = [tinv[gi] + mm(tinv[gi], x[gi]) for gi in pairs]

    ar_s = [[mm_nt(jnp.concatenate([pick(a_st[gi], j), pick(r_st[gi], j)], axis=0), s_prev[gi][j])
             for j in range(nseg)] for gi in pairs]
    mv = [mm(m_ak[gi], v_st[gi]) for gi in pairs]
    u_st = [mm(tinv[gi], unpick([q[0:2 * seg] for q in ar_s[gi]]) + mv[gi]) for gi in pairs]
    o_st = [unpick([q[2 * seg:4 * seg] for q in ar_s[gi]])
            + mm(jnp.concatenate([a_rk[gi], a_rb[gi]], axis=1),
                 jnp.concatenate([v_st[gi], u_st[gi]], axis=0)) for gi in pairs]
    for gi in pairs:
        bh_st = stack(bh, gi)
        kh_st = stack(kh, gi)
        for j in range(nseg):
            pl_row = p_last[j * seg:j * seg + 1, gi * LANES:(gi + 1) * LANES]
            upd = mm_tn(jnp.concatenate([pick(u_st[gi], j), pick(v_st[gi], j)], axis=0),
                        jnp.concatenate([pick(bh_st, j), pick(kh_st, j)], axis=0))
            sbd[gi, j] = s_prev[gi][j] * pl_row + upd

    y = jnp.concatenate([o[0:R] + o[R:R2] for o in o_st], axis=1)
    inv_n = 1.0 / HD_C
    sums = _head_sums(jnp.concatenate([y, r * kp * rk_ref[...]], axis=0), ones_bd, npair)
    mu = sums[0:R] * inv_n
    dy = y - mu
    var = _head_sums(dy * dy, ones_bd, npair) * inv_n
    yn = dy * lax.rsqrt(var + GN_EPS) * lg_ref[...] + lb_ref[...]
    out = (yn + sums[R:R2] * v) * g_ref[...].reshape(R, W)
    y_ref[...] = out.reshape(y_ref.shape).astype(y_ref.dtype)

    @pl.when(c == pl.num_programs(2) - 1)
    def _():
        for gi in range(npair):
            for j in range(nseg):
                s_ref[j, 2 * gi] = sbd[gi, j, 0:HD_C, 0:HD_C]
                s_ref[j, 2 * gi + 1] = sbd[gi, j, HD_C:R2, HD_C:R2]


def _rwkv_scan(rkv, lw, ag, g, s0, kk_p, ka_p, rk_p, lnx_g, lnx_b, t_real, out_dtype, npair=16):
    _, B, T, D = rkv.shape
    seg = min(RWKV_TILE, T)
    nseg = RWKV_TILE // seg
    nc = T // seg
    assert nseg == 1 or nc == 1
    assert B % nseg == 0 and T % seg == 0
    W = npair * LANES
    tok = pl.BlockSpec((nseg, seg, W), lambda b, p, c: (b, c, p))
    rkv_spec = lambda s: pl.BlockSpec((None, nseg, seg, W), lambda b, p, c: (s, b, c, p))
    prm = pl.BlockSpec((1, W), lambda b, p, c: (0, p))
    st = pl.BlockSpec((nseg, 2 * npair, HD_C, HD_C), lambda b, p, c: (b, p, 0, 0))
    row = lambda a: a.reshape(1, D)
    return pl.pallas_call(
        functools.partial(_rwkv_scan_kernel, t_real=t_real, seg=seg, npair=npair),
        out_shape=[jax.ShapeDtypeStruct((B, T, D), out_dtype),
                   jax.ShapeDtypeStruct((B, NH_C, HD_C, HD_C), f32)],
        grid=(B // nseg, D // W, nc),
        in_specs=[rkv_spec(0), rkv_spec(1), rkv_spec(2), tok, tok, tok, st, prm, prm, prm, prm, prm],
        out_specs=[tok, st],
        scratch_shapes=[pltpu.VMEM((npair, nseg, 2 * RWKV_TILE, 2 * RWKV_TILE), f32)],
        compiler_params=_cparams("parallel", "parallel", "arbitrary"),
        name="rwkv_scan",
    )(rkv, rkv, rkv, lw, ag, g, s0, row(kk_p), row(ka_p), row(rk_p), row(lnx_g), row(lnx_b))


def _cast_rows_kernel(x_ref, o_ref):
    o_ref[...] = x_ref[...].astype(o_ref.dtype)


def _compact_rows_bf16(w, rows, n_blocks, skip_after, skip):
    K = w.shape[1]
    assert skip % SUBLANES == 0
    return pl.pallas_call(
        _cast_rows_kernel,
        out_shape=jax.ShapeDtypeStruct((n_blocks * rows, K), bf16),
        grid=(n_blocks,),
        in_specs=[pl.BlockSpec(
            (pl.Element(rows), pl.Element(K)),
            lambda i: (pl.multiple_of(i * rows + jnp.where(i >= skip_after, skip, 0), SUBLANES), 0))],
        out_specs=pl.BlockSpec((rows, K), lambda i: (i, 0)),
        compiler_params=_cparams("parallel"),
        name="compact_rows_bf16",
    )(w)


def _stack3_kernel(a_ref, b_ref, c_ref, o_ref):
    s = pl.program_id(0)
    for i, ref in enumerate((a_ref, b_ref, c_ref)):
        @pl.when(s == i)
        def _():
            o_ref[...] = ref[...].astype(o_ref.dtype)


def _stack3_bf16(a, b, c, rows=512):
    R, K = a.shape
    spec = lambda which: pl.BlockSpec((rows, K), lambda s, r: (jnp.where(s == which, r, 0), 0))
    return pl.pallas_call(
        _stack3_kernel,
        out_shape=jax.ShapeDtypeStruct((3, R, K), bf16),
        grid=(3, R // rows),
        in_specs=[spec(0), spec(1), spec(2)],
        out_specs=pl.BlockSpec((None, rows, K), lambda s, r: (s, r, 0)),
        compiler_params=_cparams("arbitrary", "arbitrary"),
        name="stack3_bf16",
    )(a, b, c)


def _pad_lora(w1, w2):
    rank = w1.shape[1]
    return (jnp.pad(w1, ((0, 0), (0, LORA_PAD - rank))), jnp.pad(w2, ((0, LORA_PAD - rank), (0, 0))))


def _prepare_params(p):
    w_in_t = jnp.swapaxes(p['w_in_ab'][0], 0, 1)
    n_qkvo = 4 * MIX_A
    n_g = 2 * NH_A
    q = {}
    q['w_in'] = _compact_rows_bf16(w_in_t, MIX_A, 6, n_qkvo // MIX_A, n_g)
    q['w_gate'] = jnp.stack(_split_hi_lo(jnp.pad(w_in_t[n_qkvo:n_qkvo + n_g], ((0, LANES - n_g), (0, 0)))))
    q['b_gate'] = jnp.pad(p['b_if_ab'][0], (0, LANES - n_g)).reshape(1, LANES)
    q['w_out_ab'] = p['w_out_ab'][0].astype(bf16)
    q['w_up'] = p['w_up'].astype(bf16)
    q['w_down'] = p['w_down'].astype(bf16)
    q['w_rkv'] = _stack3_bf16(p['rw_wr'][0], p['rw_wk'][0], p['rw_wv'][0])
    mu = p['rw_mu'][0]
    q['mu_rkv'] = jnp.stack([mu[0], mu[2], mu[3]])[:, None, :]
    q['mu_lora'] = jnp.stack([mu[1], mu[4], mu[5]])[:, None, :]
    w1, w2w = _pad_lora(p['rw_w1'][0], p['rw_w2'][0])
    a1, w2a = _pad_lora(p['rw_a1'][0], p['rw_a2'][0])
    q['lora_w1'] = jnp.concatenate([w1, a1, p['rw_g1'][0]], axis=1).astype(bf16)
    q['lora_w2w'] = w2w.astype(bf16)
    q['lora_w2a'] = w2a.astype(bf16)
    q['lora_w2g'] = p['rw_g2'][0].astype(bf16)
    q['w_o'] = p['rw_wo'][0].astype(bf16)
    return q


def _trunk(tok, states, p, q, t_real, out_from=0):
    mC, mn, mm, rgh, rgc, rwS, rwx = states
    B, T, D = tok.batch, tok.t_pad, tok.x.shape[-1]
    M = B * T
    row = lambda a: a.reshape(1, -1)

    act_dtype = bf16 if T % (2 * SUBLANES) == 0 else f32
    zq, zr, gates = _inproj(tok, q['w_in'], q['w_gate'], q['b_gate'], act_dtype)
    zq = zq.reshape(B, T, -1)
    zr = zr.reshape(B, T, -1)
    hm, c_new, n_new, m_new = _mlstm(zq, zr, gates.reshape(B, T, LANES), mC[0], mn[0], mm[0],
                                     p['mlstm_norm_g'][0], t_real, act_dtype)
    yr, h_last = _rglru(zr, rgc[0], rgh[0], p['rg_conv_w'][0], p['rg_conv_b'][0], p['rg_wa'][0],
                        p['rg_ba'][0], p['rg_wx'][0], p['rg_bx'][0], p['rg_lambda'][0], t_real, act_dtype)
    assert t_real >= CONV_W - 1
    conv_new = zr[:, t_real - (CONV_W - 1):t_real, MIX_A:MIX_A + RG_W]
    h1 = _proj_ln([(hm.reshape(M, MIX_A), MIX_A, 0), (yr.reshape(M, RG_W), RG_W, 0)],
                  q['w_out_ab'], tok, row(p['ln1_g'][0]), row(p['ln1_b'][0]))
    h2 = _mlp_ln(h1, q['w_up'], q['w_down'], 0, row(p['ln2_g'][0]), row(p['ln2_b'][0]))

    h2_3 = h2.reshape(B, T, D)
    shift0 = rwx[0].astype(f32)[:, None]
    rkv = _rwkv_rkv(h2_3, shift0, q['mu_rkv'], q['w_rkv'])
    lw, ag, g = _rwkv_lora(h2_3, shift0, q['mu_lora'], q['lora_w1'], q['lora_w2w'], q['lora_w2a'],
                           q['lora_w2g'], row(p['rw_w0'][0]), row(p['rw_a0'][0]))
    r3 = lambda a: a.reshape(B, T, D)
    xo, s_new = _rwkv_scan(rkv.reshape(3, B, T, D), r3(lw), r3(ag), r3(g), rwS[0], p['rw_kk'][0],
                           p['rw_ka'][0], p['rw_rk'][0], p['rw_lnx_g'][0], p['rw_lnx_b'][0], t_real,
                           act_dtype)
    xo = xo.reshape(M, D)
    h3 = _proj_ln([(xo, D, 0)], q['w_o'], h2, row(p['ln1_g'][1]), row(p['ln1_b'][1]))
    n_keep = t_real - out_from
    h4 = _mlp_ln(h3, q['w_up'], q['w_down'], 1, row(p['ln2_g'][1]), row(p['ln2_b'][1]),
                 window=None if (out_from == 0 and t_real == T) else (T, out_from, n_keep))
    shift_new = h2_3[:, t_real - 1]

    new_states = (c_new[None], n_new[None], m_new[None], h_last[None], conv_new[None],
                  s_new[None], shift_new[None])
    return h4.reshape(B, n_keep, D), new_states


def kernel(x_prompt, x_sample, state_mlstm_C, state_mlstm_n, state_mlstm_m, state_rglru_h, state_rglru_conv, state_rwkv_S, state_rwkv_shift, meta_tokens, w_in_ab, b_if_ab, mlstm_norm_g, rg_conv_w, rg_conv_b, rg_wa, rg_ba, rg_wx, rg_bx, rg_lambda, w_out_ab, rw_mu, rw_wr, rw_wk, rw_wv, rw_wo, rw_w0, rw_w1, rw_w2, rw_a0, rw_a1, rw_a2, rw_g1, rw_g2, rw_kk, rw_ka, rw_rk, rw_lnx_g, rw_lnx_b, ln1_g, ln1_b, ln2_g, ln2_b, w_up, w_down):
    p = dict(w_in_ab=w_in_ab, b_if_ab=b_if_ab, mlstm_norm_g=mlstm_norm_g, rg_conv_w=rg_conv_w,
             rg_conv_b=rg_conv_b, rg_wa=rg_wa, rg_ba=rg_ba, rg_wx=rg_wx, rg_bx=rg_bx,
             rg_lambda=rg_lambda, w_out_ab=w_out_ab, rw_mu=rw_mu, rw_wr=rw_wr, rw_wk=rw_wk,
             rw_wv=rw_wv, rw_wo=rw_wo, rw_w0=rw_w0, rw_w1=rw_w1, rw_w2=rw_w2, rw_a0=rw_a0,
             rw_a1=rw_a1, rw_a2=rw_a2, rw_g1=rw_g1, rw_g2=rw_g2, rw_kk=rw_kk, rw_ka=rw_ka,
             rw_rk=rw_rk, rw_lnx_g=rw_lnx_g, rw_lnx_b=rw_lnx_b, ln1_g=ln1_g, ln1_b=ln1_b,
             ln2_g=ln2_g, ln2_b=ln2_b, w_up=w_up, w_down=w_down)
    q = _prepare_params(p)
    B, S, D = x_prompt.shape
    dt = x_prompt.dtype
    t_prompt = N_META + S
    t_pad = -(-t_prompt // MLSTM_CHUNK) * MLSTM_CHUNK
    zero_states = (
        jnp.zeros((1, B, NH_A, HD_A, HD_A), f32),
        jnp.zeros((1, B, NH_A, HD_A), f32),
        jnp.zeros((1, B, NH_A), f32),
        jnp.zeros((1, B, RG_W), f32),
        jnp.zeros((1, B, CONV_W - 1, RG_W), dt),
        jnp.zeros((1, B, NH_C, HD_C, HD_C), f32),
        jnp.zeros((1, B, D), dt),
    )
    hp, ps = _trunk(_Tokens(x_prompt, meta_tokens.astype(dt), t_pad), zero_states, p, q, t_prompt,
                    out_from=N_META)
    sample_states = (state_mlstm_C, state_mlstm_n, state_mlstm_m, state_rglru_h, state_rglru_conv,
                     state_rwkv_S, state_rwkv_shift)
    hs, ss = _trunk(_Tokens(x_sample, None, x_sample.shape[1]), sample_states, p, q, x_sample.shape[1])
    return (hp, hs, *ps, *ss)
```

```python
import functools
from typing import NamedTuple, Optional

import jax
import jax.numpy as jnp
from jax import lax
from jax.experimental import pallas as pl
from jax.experimental.pallas import tpu as pltpu

f32 = jnp.float32
bf16 = jnp.bfloat16
HI = lax.Precision.HIGHEST

D_MODEL = 2048
N_META = 16
MIX_A = 1024
NH_A = 4
HD_A = 256
RG_W = 1024
RG_BLOCKS = 8
RG_BW = 128
CONV_W = 4
RG_C = 8.0
HD_C = 64
NH_C = 32
D_FF = 8192
LN_EPS = 1e-5
GN_EPS = 64e-5
DEPTH = 2
ALPHA = (2.0 * DEPTH) ** 0.25

LANES = 128
SUBLANES = 8
VMEM_LIMIT_BYTES = 56 * 1024 * 1024

MLSTM_CHUNK = 128
RWKV_TILE = 64
SHORT_SEQ_ROWS_CAP = 512
LORA_PAD = 128
NEG_BIG = -1e30

NT_DIMS = (((1,), (1,)), ((), ()))
TN_DIMS = (((0,), (0,)), ((), ()))


class _Tokens(NamedTuple):
    x: jax.Array
    lead: Optional[jax.Array]
    t_pad: int

    @property
    def batch(self):
        return self.x.shape[0]


def _cparams(*sem):
    return pltpu.CompilerParams(dimension_semantics=sem, vmem_limit_bytes=VMEM_LIMIT_BYTES)


def _pick_tile(n, cap):
    best = None
    for d in range(SUBLANES, min(n, cap) + 1, SUBLANES):
        if n % d == 0:
            best = d
    assert best is not None, (n, cap)
    return best


def _dot_bf16(a, b):
    return jnp.dot(a.astype(bf16), b.astype(bf16), preferred_element_type=f32)


def _dot_hi(a, b, dims=None):
    if dims is None:
        return jnp.dot(a, b, precision=HI, preferred_element_type=f32)
    return lax.dot_general(a, b, dims, precision=HI, preferred_element_type=f32)


def _split_hi_lo(x):
    hi = x.astype(bf16)
    return hi, (x - hi.astype(f32)).astype(bf16)


def _layer_norm_rows(t, g, b):
    mu = jnp.mean(t, axis=-1, keepdims=True)
    d = t - mu
    var = jnp.mean(d * d, axis=-1, keepdims=True)
    return d * lax.rsqrt(var + LN_EPS) * g + b


def _window_spec(tm, K, seq_rows, t_pad, n_lead, grid_rank, **kw):
    tps = t_pad // tm
    assert tps >= 2 and tps * tm == t_pad and t_pad - n_lead - seq_rows <= tm

    def index(g, *_):
        start = jnp.clip(tm * (g % tps) - n_lead, 0, seq_rows - tm)
        return (pl.multiple_of((g // tps) * seq_rows + start, SUBLANES), 0)

    assert grid_rank in (1, 2)
    return pl.BlockSpec((pl.Element(tm), pl.Element(K)), index, **kw)


def _virtual_rows(blk, lead_ref, i_seq, tps, n_tail):
    tm = blk.shape[0]
    n_lead = lead_ref.shape[0]
    first = jnp.concatenate([lead_ref[...], blk[:tm - n_lead]], axis=0)
    last = jnp.concatenate([blk[n_tail:], jnp.zeros((n_tail, blk.shape[1]), blk.dtype)], axis=0)
    return jnp.where(i_seq == 0, first, jnp.where(i_seq == tps - 1, last, blk))


def _inproj_kernel(x_ref, w_ref, wg_ref, bg_ref, *rest, virtual):
    if virtual is None:
        o1_ref, o2_ref, g_ref, xb_ref = rest
    else:
        lead_ref, o1_ref, o2_ref, g_ref, xb_ref = rest

    @pl.when(pl.program_id(1) == 0)
    def _():
        x = x_ref[...]
        if virtual is not None:
            tps, n_tail = virtual
            x = _virtual_rows(x, lead_ref, pl.program_id(0) % tps, tps, n_tail)
        x_hi, x_lo = _split_hi_lo(x)
        xb_ref[...] = x_hi
        nt = lambda a, b: lax.dot_general(a, b, NT_DIMS, preferred_element_type=f32)
        pre = (nt(x_hi, wg_ref[0]) + (nt(x_hi, wg_ref[1]) + nt(x_lo, wg_ref[0]))) + bg_ref[...]
        lane = lax.broadcasted_iota(jnp.int32, pre.shape, 1)
        g_ref[...] = jnp.where(lane < NH_A, pre,
                               jnp.where(lane < 2 * NH_A, jax.nn.log_sigmoid(pre), 0.0))

    y = lax.dot_general(xb_ref[...], w_ref[...], NT_DIMS, preferred_element_type=f32)
    j = pl.program_id(1)
    n_first = pl.num_programs(1) // 2

    @pl.when(j < n_first)
    def _():
        o1_ref[...] = y.astype(o1_ref.dtype)

    @pl.when(j >= n_first)
    def _():
        o2_ref[...] = y


def _inproj(tok, w, w_g, b_g, qkv_dtype, tm_cap=1088, tn=1024):
    x = tok.x.reshape(-1, tok.x.shape[-1])
    K = x.shape[1]
    M = tok.batch * tok.t_pad
    n_out = w.shape[0]
    n_half = n_out // 2
    nt = n_half // tn
    if tok.lead is None:
        tm = _pick_tile(M, tm_cap)
        x_specs, x_args, virtual = [pl.BlockSpec((tm, K), lambda i, j: (i, 0))], [x], None
    else:
        tm = _pick_tile(tok.t_pad, tm_cap)
        n_lead = tok.lead.shape[0]
        x_specs = [_window_spec(tm, K, tok.x.shape[1], tok.t_pad, n_lead, 2)]
        x_args = [x]
        virtual = (tok.t_pad // tm, tok.t_pad - n_lead - tok.x.shape[1])
    lead_specs = [] if tok.lead is None else [pl.BlockSpec(tok.lead.shape, lambda i, j: (0, 0))]
    lead_args = [] if tok.lead is None else [tok.lead]
    return pl.pallas_call(
        functools.partial(_inproj_kernel, virtual=virtual),
        out_shape=[jax.ShapeDtypeStruct((M, n_half), qkv_dtype), jax.ShapeDtypeStruct((M, n_half), f32),
                   jax.ShapeDtypeStruct((M, LANES), f32)],
        grid=(M // tm, n_out // tn),
        in_specs=x_specs + [pl.BlockSpec((tn, K), lambda i, j: (j, 0)),
                            pl.BlockSpec((2, LANES, K), lambda i, j: (0, 0, 0)),
                            pl.BlockSpec((1, LANES), lambda i, j: (0, 0))] + lead_specs,
        out_specs=[pl.BlockSpec((tm, tn), lambda i, j: (i, jnp.minimum(j, nt - 1))),
                   pl.BlockSpec((tm, tn), lambda i, j: (i, jnp.maximum(j - nt, 0))),
                   pl.BlockSpec((tm, LANES), lambda i, j: (i, 0))],
        scratch_shapes=[pltpu.VMEM((tm, K), bf16)],
        compiler_params=_cparams("parallel", "arbitrary"),
        name="inproj",
    )(*x_args, w, w_g, b_g, *lead_args)


def _proj_ln_kernel(*refs, n_x, virtual):
    x_refs = refs[:n_x]
    if virtual is None:
        w_ref, res_ref, g_ref, b_ref, o_ref = refs[n_x:]
        res = res_ref[...]
    else:
        w_ref, res_ref, g_ref, b_ref, lead_ref, o_ref = refs[n_x:]
        tps, n_tail = virtual
        res = _virtual_rows(res_ref[...], lead_ref, pl.program_id(0) % tps, tps, n_tail)
    y = None
    k0 = 0
    for x_ref in x_refs:
        kc = x_ref.shape[1]
        part = _dot_bf16(x_ref[...], w_ref[k0:k0 + kc, :])
        y = part if y is None else y + part
        k0 += kc
    t = ALPHA * res + y
    o_ref[...] = _layer_norm_rows(t, g_ref[...], b_ref[...])


def _proj_ln(xs, w, res, g, b, tm_cap=272):
    if isinstance(res, _Tokens) and res.lead is not None:
        tok = res
        N = tok.x.shape[-1]
        M = tok.batch * tok.t_pad
        tm = _pick_tile(tok.t_pad, tm_cap)
        n_lead = tok.lead.shape[0]
        res = tok.x.reshape(-1, N)
        res_spec = _window_spec(tm, N, tok.x.shape[1], tok.t_pad, n_lead, 1)
        virtual = (tok.t_pad // tm, tok.t_pad - n_lead - tok.x.shape[1])
        lead_specs, lead_args = [pl.BlockSpec(tok.lead.shape, lambda i: (0, 0))], [tok.lead]
    else:
        if isinstance(res, _Tokens):
            res = res.x.reshape(-1, res.x.shape[-1])
        M, N = res.shape
        tm = _pick_tile(M, tm_cap)
        res_spec = pl.BlockSpec((tm, N), lambda i: (i, 0))
        virtual, lead_specs, lead_args = None, [], []
    in_specs = []
    args = []
    for arr, width, blk in xs:
        in_specs.append(pl.BlockSpec((tm, width), functools.partial(lambda i, blk: (i, blk), blk=blk)))
        args.append(arr)
    K = w.shape[0]
    in_specs += [pl.BlockSpec((K, N), lambda i: (0, 0)),
                 res_spec,
                 pl.BlockSpec((1, N), lambda i: (0, 0)),
                 pl.BlockSpec((1, N), lambda i: (0, 0))] + lead_specs
    return pl.pallas_call(
        functools.partial(_proj_ln_kernel, n_x=len(xs), virtual=virtual),
        out_shape=jax.ShapeDtypeStruct((M, N), f32),
        grid=(M // tm,),
        in_specs=in_specs,
        out_specs=pl.BlockSpec((tm, N), lambda i: (i, 0)),
        compiler_params=_cparams("parallel"),
        name="proj_ln",
    )(*args, w, res, g, b, *lead_args)


def _mlp_ln_kernel(x_ref, wu_ref, wd_ref, g_ref, b_ref, o_ref, xb_ref):
    f = pl.program_id(1)

    @pl.when(f == 0)
    def _():
        xb_ref[...] = x_ref[...].astype(bf16)
        o_ref[...] = jnp.zeros_like(o_ref)

    a = jnp.dot(xb_ref[...], wu_ref[...], preferred_element_type=f32)
    a = jnp.square(jnp.maximum(a, 0.0))
    o_ref[...] += jnp.dot(a.astype(bf16), wd_ref[...], preferred_element_type=f32)

    @pl.when(f == pl.num_programs(1) - 1)
    def _():
        t = ALPHA * x_ref[...] + o_ref[...]
        o_ref[...] = _layer_norm_rows(t, g_ref[...], b_ref[...])


def _mlp_ln(x, w_up, w_down, layer, g, b, tm_cap=1088, tf=1024, window=None):
    M, K = x.shape
    F = w_up.shape[2]
    if window is None:
        tm = _pick_tile(M, tm_cap)
        x_spec = pl.BlockSpec((tm, K), lambda i, j: (i, 0), pipeline_mode=pl.Buffered(1))
    else:
        seq_rows, first, count = window
        assert first % SUBLANES == 0 and seq_rows % SUBLANES == 0
        tm = _pick_tile(count, tm_cap)
        per_seq = count // tm
        M = (M // seq_rows) * count
        x_spec = pl.BlockSpec(
            (pl.Element(tm), pl.Element(K)),
            lambda i, j: (pl.multiple_of((i // per_seq) * seq_rows + first + (i % per_seq) * tm, SUBLANES), 0),
            pipeline_mode=pl.Buffered(1))
    return pl.pallas_call(
        _mlp_ln_kernel,
        out_shape=jax.ShapeDtypeStruct((M, K), f32),
        grid=(M // tm, F // tf),
        in_specs=[x_spec,
                  pl.BlockSpec((None, K, tf), lambda i, j: (layer, 0, j)),
                  pl.BlockSpec((None, tf, K), lambda i, j: (layer, j, 0)),
                  pl.BlockSpec((1, K), lambda i, j: (0, 0)),
                  pl.BlockSpec((1, K), lambda i, j: (0, 0))],
        out_specs=pl.BlockSpec((tm, K), lambda i, j: (i, 0), pipeline_mode=pl.Buffered(1)),
        scratch_shapes=[pltpu.VMEM((tm, K), bf16)],
        compiler_params=_cparams("parallel", "arbitrary"),
        name="mlp_ln",
    )(x, w_up, w_down, g, b)


def _mlstm_kernel(q_ref, k_ref, v_ref, o_ref, g_ref, c0_ref, n0_ref, m0_ref, ng_ref,
                  h_ref, c_ref, n_ref, m_ref, *scratch, t_real, carry):
    nb, lb = q_ref.shape[0], q_ref.shape[1]
    R = nb * lb
    c = pl.program_id(1)
    if carry:
        cs, ns, ms = scratch

        @pl.when(c == 0)
        def _():
            cs[...] = c0_ref[...]
            ns[...] = n0_ref[...]
            ms[...] = m0_ref[...]
    else:
        cs, ns, ms = c0_ref, n0_ref, m0_ref

    heads = range(NH_A)
    seqs = range(nb)
    hs = lambda h: slice(h * HD_A, (h + 1) * HD_A)
    rows = lambda j: slice(j * lb, (j + 1) * lb)
    t_col = c * lb + lax.broadcasted_iota(jnp.int32, (nb, lb, 1), 1).reshape(R, 1)
    vrow = t_col < t_real
    rowi = lax.broadcasted_iota(jnp.int32, (R, R), 0)
    coli = lax.broadcasted_iota(jnp.int32, (R, R), 1)
    if nb == 1:
        causal = rowi >= coli
    else:
        shift = lb.bit_length() - 1
        causal = jnp.logical_and((rowi >> shift) == (coli >> shift), rowi >= coli)

    def per_seq(x):
        return jnp.broadcast_to(x, (nb, lb, x.shape[-1])).reshape(R, x.shape[-1])

    def last_row(x):
        return x.reshape(nb, lb, x.shape[-1])[:, lb - 1:lb, :]

    ld = lambda ref, h: jnp.where(vrow, ref[:, :, hs(h)].astype(f32).reshape(R, HD_A), 0.0)
    q = [ld(q_ref, h) for h in heads]
    k = [ld(k_ref, h) * (HD_A ** -0.5) for h in heads]
    vb = [ld(v_ref, h).astype(bf16) for h in heads]
    qb = [x.astype(bf16) for x in q]
    qk = [lax.dot_general(qb[h], k[h].astype(bf16), NT_DIMS, preferred_element_type=f32) for h in heads]

    gate = jnp.where(vrow, g_ref[...].reshape(R, LANES), 0.0)
    fcum = _dot_hi(causal.astype(f32), gate)
    gate_t = gate.T
    fcum_t = fcum.T
    vcol = (c * lb + (lax.broadcasted_iota(jnp.int32, (1, R), 1) & (lb - 1))) < t_real

    m_prev, fc, igc, fl, m_row, inter, s = [], [], [], [], [], [], []
    for h in heads:
        fc.append(fcum[:, NH_A + h:NH_A + h + 1])
        fr = fcum_t[NH_A + h:NH_A + h + 1, :]
        igc.append(jnp.where(vrow, gate[:, h:h + 1], NEG_BIG))
        igr = jnp.where(vcol, gate_t[h:h + 1, :], NEG_BIG)
        m_prev.append(ms[:, h, :, 0:1])
        logd = jnp.where(causal, fc[h] - fr + igr, NEG_BIG)
        b_in = fc[h] + per_seq(m_prev[h])
        m_row.append(jnp.maximum(b_in, jnp.max(logd, axis=1, keepdims=True)))
        inter.append(jnp.exp(b_in - m_row[h]))
        s.append(qk[h] * jnp.exp(logd - m_row[h]))
        fl.append(last_row(fc[h]))

    sv = [jnp.dot(s[h].astype(bf16), vb[h], preferred_element_type=f32) for h in heads]
    qc = [jnp.concatenate([jnp.dot(qb[h][rows(j)], cs[j, h].astype(bf16), preferred_element_type=f32)
                           for j in seqs], axis=0) for h in heads]
    for h in heads:
        num = sv[h] + inter[h] * qc[h]
        den = jnp.sum(s[h], axis=1, keepdims=True) \
            + inter[h] * jnp.sum(q[h] * per_seq(ns[:, h]), axis=1, keepdims=True)
        hh = num / jnp.maximum(jnp.abs(den), jnp.exp(-m_row[h]))
        mu = jnp.mean(hh, axis=1, keepdims=True)
        dv = hh - mu
        var = jnp.mean(dv * dv, axis=1, keepdims=True)
        hn = dv * lax.rsqrt(var + LN_EPS) * ng_ref[h]
        out = hn * jax.nn.sigmoid(ld(o_ref, h))
        h_ref[:, :, hs(h)] = out.reshape(nb, lb, HD_A).astype(h_ref.dtype)

    kw, m_new, dec = [], [], []
    for h in heads:
        wj = per_seq(fl[h]) - fc[h] + igc[h]
        wmax = jnp.max(wj.reshape(nb, lb, 1), axis=1, keepdims=True)
        m_new.append(jnp.maximum(fl[h] + m_prev[h], wmax))
        dec.append(jnp.exp(fl[h] + m_prev[h] - m_new[h]))
        kw.append(k[h] * jnp.exp(wj - per_seq(m_new[h])))
    kv = [[lax.dot_general(kw[h][rows(j)].astype(bf16), vb[h][rows(j)], TN_DIMS,
                           preferred_element_type=f32) for j in seqs] for h in heads]
    c_dst, n_dst, m_dst = (cs, ns, ms) if carry else (c_ref, n_ref, m_ref)
    for h in heads:
        n_new = dec[h] * ns[:, h] + jnp.sum(kw[h].reshape(nb, lb, HD_A), axis=1, keepdims=True)
        for j in seqs:
            c_dst[j, h] = dec[h][j] * cs[j, h] + kv[h][j]
        n_dst[:, h] = n_new
        m_dst[:, h] = jnp.broadcast_to(m_new[h], (nb, 1, LANES))

    if carry:
        @pl.when(c == pl.num_programs(1) - 1)
        def _():
            c_ref[...] = cs[...]
            n_ref[...] = ns[...]
            m_ref[...] = ms[...]


def _mlstm(zq, zr, gates, c0, n0, m0, norm_g, t_real, out_dtype, nb_short=8):
    B, T, _ = zq.shape
    if T <= SUBLANES:
        nb, lb = nb_short, T
    else:
        nb, lb = 1, MLSTM_CHUNK
    nc = T // lb
    carry = nc > 1
    assert B % nb == 0 and T % lb == 0 and (carry or t_real == T)
    m0b = jnp.broadcast_to(m0[:, :, None, None], (B, NH_A, 1, LANES))
    zspec = lambda group: pl.BlockSpec((nb, lb, MIX_A), lambda b, c: (b, c, group))
    st_spec = lambda r, w: pl.BlockSpec((nb, NH_A, r, w), lambda b, c: (b, 0, 0, 0))
    st_shapes = [(HD_A, HD_A), (1, HD_A), (1, LANES)]
    outs = pl.pallas_call(
        functools.partial(_mlstm_kernel, t_real=t_real, carry=carry),
        out_shape=[jax.ShapeDtypeStruct((B, T, MIX_A), out_dtype)]
        + [jax.ShapeDtypeStruct((B, NH_A) + s, f32) for s in st_shapes],
        grid=(B // nb, nc),
        in_specs=[zspec(0), zspec(1), zspec(2), zspec(0),
                  pl.BlockSpec((nb, lb, LANES), lambda b, c: (b, c, 0))]
        + [st_spec(*s) for s in st_shapes]
        + [pl.BlockSpec((NH_A, 1, HD_A), lambda b, c: (0, 0, 0))],
        out_specs=[pl.BlockSpec((nb, lb, MIX_A), lambda b, c: (b, c, 0))]
        + [st_spec(*s) for s in st_shapes],
        scratch_shapes=[pltpu.VMEM((nb, NH_A) + s, f32) for s in st_shapes] if carry else [],
        compiler_params=_cparams("parallel", "arbitrary"),
        name="mlstm",
    )(zq, zq, zq, zr, gates, c0, n0.reshape(B, NH_A, 1, HD_A), m0b, norm_g.reshape(NH_A, 1, HD_A))
    hm, c_new, n_new, m_new = outs
    return hm, c_new, n_new.reshape(B, NH_A, HD_A), m_new[:, :, 0, 0]


def _one_minus_exp(x):
    t = jnp.tanh(0.5 * x)
    return -2.0 * t / (1.0 - t)


def _rglru_kernel(xr_ref, gr_ref, conv0_ref, h0_ref, cw_ref, cb_ref, wa_ref, ba_ref, wx_ref,
                  bx_ref, lam_ref, y_ref, hl_ref, xp, a_s, h_s, hc, *, t_real):
    nb, lr = xr_ref.shape[0], xr_ref.shape[1]
    rows = nb * lr
    ngroup = rows // SUBLANES
    c = pl.program_id(1)
    hist = SUBLANES - (CONV_W - 1)

    @pl.when(c == 0)
    def _():
        xp[:, 0:SUBLANES, :] = jnp.zeros((nb, SUBLANES, RG_W), f32)
        xp[:, hist:SUBLANES, :] = conv0_ref[...]
        hc[...] = h0_ref[...]

    xp[:, SUBLANES:SUBLANES + lr, :] = xr_ref[...]
    xc = cb_ref[...]
    for j in range(CONV_W):
        xc = xc + xp[:, hist + j:hist + j + lr, :] * cw_ref[j:j + 1, :]
    xp[:, hist:SUBLANES, :] = xp[:, hist + lr:SUBLANES + lr, :]
    xc = xc.reshape(rows, RG_W)

    xcb = xc.astype(bf16)
    ra = []
    rx = []
    for n in range(RG_BLOCKS):
        xn = xcb[:, n * RG_BW:(n + 1) * RG_BW]
        ra.append(jnp.dot(xn, wa_ref[n].astype(bf16), preferred_element_type=f32))
        rx.append(jnp.dot(xn, wx_ref[n].astype(bf16), preferred_element_type=f32))
    r = jax.nn.sigmoid(jnp.concatenate(ra, axis=1) + ba_ref[...])
    gi = jax.nn.sigmoid(jnp.concatenate(rx, axis=1) + bx_ref[...])
    log_a = -RG_C * r * jax.nn.softplus(-lam_ref[...])
    t_idx = c * lr + lax.broadcasted_iota(jnp.int32, (nb, lr, 1), 1).reshape(rows, 1)
    valid = t_idx < t_real
    a = jnp.where(valid, jnp.exp(log_a), 1.0)
    u = jnp.where(valid, jnp.sqrt(_one_minus_exp(2.0 * log_a)) * (gi * xc), 0.0)

    a = a.reshape(ngroup, SUBLANES, RG_W)
    u = u.reshape(ngroup, SUBLANES, RG_W)
    r8 = lax.broadcasted_iota(jnp.int32, (1, SUBLANES, 1), 1)
    d = 1
    while d < SUBLANES:
        inside = r8 >= d
        a_sh = jnp.where(inside, pltpu.roll(a, d, axis=1), 1.0)
        u_sh = jnp.where(inside, pltpu.roll(u, d, axis=1), 0.0)
        u = a * u_sh + u
        a = a * a_sh
        d *= 2

    if lr == SUBLANES:
        h = u + a * hc[...]
        h_last = h[:, SUBLANES - 1:SUBLANES, :]
        h = h.reshape(rows, RG_W)
    else:
        assert nb == 1
        a_s[...] = a.reshape(rows, RG_W)
        h_s[...] = u.reshape(rows, RG_W)

        def step(g, h_prev):
            i = pl.multiple_of(g * SUBLANES, SUBLANES)
            hg = h_s[pl.ds(i, SUBLANES), :] + a_s[pl.ds(i, SUBLANES), :] * h_prev
            h_s[pl.ds(i, SUBLANES), :] = hg
            return hg[SUBLANES - 1:SUBLANES, :]

        h_last = lax.fori_loop(0, ngroup, step, hc[0]).reshape(1, 1, RG_W)
        h = h_s[...]
    hc[...] = h_last
    y_ref[...] = (h * jax.nn.gelu(gr_ref[...].reshape(rows, RG_W))).reshape(nb, lr, RG_W).astype(y_ref.dtype)

    @pl.when(c == pl.num_programs(1) - 1)
    def _():
        hl_ref[...] = h_last


def _rglru(zr, conv0, h0, conv_w, conv_b, wa, ba, wx, bx, lam, t_real, out_dtype, lr_cap=544, nb_short=16):
    B, T, _ = zr.shape
    if T == SUBLANES:
        nb, lr = nb_short, T
    else:
        nb, lr = 1, _pick_tile(T, lr_cap)
    assert B % nb == 0
    row = lambda a: a.reshape(1, RG_W)
    full = lambda shp: pl.BlockSpec(shp, lambda b, c: (0,) * len(shp))
    y, h_last = pl.pallas_call(
        functools.partial(_rglru_kernel, t_real=t_real),
        out_shape=[jax.ShapeDtypeStruct((B, T, RG_W), out_dtype),
                   jax.ShapeDtypeStruct((B, 1, RG_W), f32)],
        grid=(B // nb, T // lr),
        in_specs=[pl.BlockSpec((nb, lr, RG_W), lambda b, c: (b, c, 1)),
                  pl.BlockSpec((nb, lr, RG_W), lambda b, c: (b, c, 2)),
                  pl.BlockSpec((nb, CONV_W - 1, RG_W), lambda b, c: (b, 0, 0)),
                  pl.BlockSpec((nb, 1, RG_W), lambda b, c: (b, 0, 0)),
                  full((CONV_W, RG_W)), full((1, RG_W)),
                  full((RG_BLOCKS, RG_BW, RG_BW)), full((1, RG_W)),
                  full((RG_BLOCKS, RG_BW, RG_BW)), full((1, RG_W)), full((1, RG_W))],
        out_specs=[pl.BlockSpec((nb, lr, RG_W), lambda b, c: (b, c, 0)),
                   pl.BlockSpec((nb, 1, RG_W), lambda b, c: (b, 0, 0))],
        scratch_shapes=[pltpu.VMEM((nb, SUBLANES + lr, RG_W), f32), pltpu.VMEM((nb * lr, RG_W), f32),
                        pltpu.VMEM((nb * lr, RG_W), f32), pltpu.VMEM((nb, 1, RG_W), f32)],
        compiler_params=_cparams("parallel", "arbitrary"),
        name="rglru",
    )(zr, zr, conv0, h0.reshape(B, 1, RG_W), conv_w, row(conv_b), wa, row(ba), wx, row(bx), row(lam))
    return y, h_last.reshape(B, RG_W)


def _token_shift(x_ref, st_ref, xp_ref, c):
    tt = x_ref.shape[1]
    last = SUBLANES - 1

    @pl.when(c == 0)
    def _():
        xp_ref[:, last:SUBLANES, :] = st_ref[...]

    x = x_ref[...]
    xp_ref[:, SUBLANES:SUBLANES + tt, :] = x
    xprev = xp_ref[:, last:last + tt, :]
    xp_ref[:, last:SUBLANES, :] = xp_ref[:, last + tt:SUBLANES + tt, :]
    return x, xprev


def _seq_tile(B, T, rows_cap):
    if T <= SUBLANES:
        return _pick_tile(B * T, min(rows_cap, SHORT_SEQ_ROWS_CAP)) // T, T
    return 1, _pick_tile(T, rows_cap)


def _rwkv_rkv_kernel(x_ref, st_ref, mu_ref, w_ref, o_ref, xs_ref, xp_ref):
    c = pl.program_id(1)
    s = pl.program_id(2)

    @pl.when(jnp.logical_and(s == 0, pl.program_id(3) == 0))
    def _():
        x, xprev = _token_shift(x_ref, st_ref, xp_ref, c)
        dx = xprev - x
        for si in range(3):
            xs_ref[si] = (x + dx * mu_ref[si]).reshape(xs_ref.shape[1:]).astype(bf16)

    o_ref[...] = jnp.dot(xs_ref[s], w_ref[...], preferred_element_type=f32)


def _rwkv_rkv(x, shift0, mu3, w3, rows_cap=1088, tn=512):
    B, T, K = x.shape
    nb, tt = _seq_tile(B, T, rows_cap)
    rows = nb * tt
    nc = T // tt
    return pl.pallas_call(
        _rwkv_rkv_kernel,
        out_shape=jax.ShapeDtypeStruct((3, B * T, K), f32),
        grid=(B // nb, nc, 3, K // tn),
        in_specs=[pl.BlockSpec((nb, tt, K), lambda b, c, s, j: (b, c, 0), pipeline_mode=pl.Buffered(1)),
                  pl.BlockSpec((nb, 1, K), lambda b, c, s, j: (b, 0, 0)),
                  pl.BlockSpec((3, 1, K), lambda b, c, s, j: (0, 0, 0)),
                  pl.BlockSpec((None, K, tn), lambda b, c, s, j: (s, 0, j))],
        out_specs=pl.BlockSpec((None, rows, tn), lambda b, c, s, j: (s, b * nc + c, j)),
        scratch_shapes=[pltpu.VMEM((3, rows, K), bf16), pltpu.VMEM((nb, SUBLANES + tt, K), f32)],
        compiler_params=_cparams("parallel", "arbitrary", "arbitrary", "arbitrary"),
        name="rwkv_rkv",
    )(x, shift0, mu3, w3)


def _rwkv_lora_kernel(x_ref, st_ref, mu_ref, w1_ref, w2w_ref, w2a_ref, w2g_ref, w0_ref, a0_ref,
                      lw_ref, a_ref, g_ref, xp_ref):
    x, xprev = _token_shift(x_ref, st_ref, xp_ref, pl.program_id(1))
    dx = xprev - x
    mix = lambda i: (x + dx * mu_ref[i]).reshape(lw_ref.shape).astype(bf16)
    r = LORA_PAD
    hw = jnp.tanh(jnp.dot(mix(0), w1_ref[:, 0:r], preferred_element_type=f32))
    ha = jnp.dot(mix(1), w1_ref[:, r:2 * r], preferred_element_type=f32)
    hg = jax.nn.sigmoid(jnp.dot(mix(2), w1_ref[:, 2 * r:], preferred_element_type=f32))
    w = w0_ref[...] + jnp.dot(hw.astype(bf16), w2w_ref[...], preferred_element_type=f32)
    w = -jax.nn.softplus(-w) - 0.5
    lw_ref[...] = -jnp.exp(w)
    a = a0_ref[...] + jnp.dot(ha.astype(bf16), w2a_ref[...], preferred_element_type=f32)
    a_ref[...] = jax.nn.sigmoid(a)
    g_ref[...] = jnp.dot(hg.astype(bf16), w2g_ref[...], preferred_element_type=f32)


def _rwkv_lora(x, shift0, mu3, w1, w2w, w2a, w2g, w0, a0, rows_cap=272):
    B, T, K = x.shape
    nb, tt = _seq_tile(B, T, rows_cap)
    rows = nb * tt
    nc = T // tt
    full = lambda a: pl.BlockSpec(a.shape, lambda b, c: (0,) * a.ndim)
    out = pl.BlockSpec((rows, K), lambda b, c: (b * nc + c, 0))
    return pl.pallas_call(
        _rwkv_lora_kernel,
        out_shape=[jax.ShapeDtypeStruct((B * T, K), f32)] * 3,
        grid=(B // nb, nc),
        in_specs=[pl.BlockSpec((nb, tt, K), lambda b, c: (b, c, 0)),
                  pl.BlockSpec((nb, 1, K), lambda b, c: (b, 0, 0)),
                  full(mu3), full(w1), full(w2w), full(w2a), full(w2g), full(w0), full(a0)],
        out_specs=[out, out, out],
        scratch_shapes=[pltpu.VMEM((nb, SUBLANES + tt, K), f32)],
        compiler_params=_cparams("parallel", "arbitrary"),
        name="rwkv_lora",
    )(x, shift0, mu3, w1, w2w, w2a, w2g, w0, a0)


def _head_sums(x, ones_bd, npair):
    rows = x.shape[0]
    xs = jnp.concatenate([x[:, g * LANES:(g + 1) * LANES] for g in range(npair)], axis=0)
    s = jnp.dot(xs.astype(bf16), ones_bd, preferred_element_type=f32)
    return jnp.concatenate([s[g * rows:(g + 1) * rows] for g in range(npair)], axis=1)


def _rwkv_scan_kernel(r_ref, k_ref, v_ref, lw_ref, ag_ref, g_ref, s0_ref, kk_ref, ka_ref, rk_ref,
                      lg_ref, lb_ref, y_ref, s_ref, sbd, *, t_real, seg, npair):
    R2 = 2 * RWKV_TILE
    nseg = RWKV_TILE // seg
    c = pl.program_id(2)

    @pl.when(c == 0)
    def _():
        sbd[...] = jnp.zeros(sbd.shape, f32)
        for gi in range(npair):
            for j in range(nseg):
                sbd[gi, j, 0:HD_C, 0:HD_C] = s0_ref[j, 2 * gi]
                sbd[gi, j, HD_C:R2, HD_C:R2] = s0_ref[j, 2 * gi + 1]

    tile_live = c * seg < t_real

    @pl.when(jnp.logical_not(tile_live))
    def _():
        y_ref[...] = jnp.zeros(y_ref.shape, y_ref.dtype)

    @pl.when(tile_live)
    def _():
        _rwkv_tile(c, r_ref, k_ref, v_ref, lw_ref, ag_ref, g_ref, kk_ref, ka_ref, rk_ref, lg_ref, lb_ref,
                   y_ref, sbd, t_real=t_real, seg=seg, npair=npair)

    @pl.when(c == pl.num_programs(2) - 1)
    def _():
        for gi in range(npair):
            for j in range(nseg):
                s_ref[j, 2 * gi] = sbd[gi, j, 0:HD_C, 0:HD_C]
                s_ref[j, 2 * gi + 1] = sbd[gi, j, HD_C:R2, HD_C:R2]


def _rwkv_tile(c, r_ref, k_ref, v_ref, lw_ref, ag_ref, g_ref, kk_ref, ka_ref, rk_ref, lg_ref, lb_ref,
               y_ref, sbd, *, t_real, seg, npair):
    R = RWKV_TILE
    R2 = 2 * R
    nseg = R // seg
    W = npair * LANES
    shift = seg.bit_length() - 1
    mm = lambda a, b: jnp.dot(a.astype(bf16), b.astype(bf16), preferred_element_type=f32)
    mm_nt = lambda a, b: lax.dot_general(a.astype(bf16), b.astype(bf16), NT_DIMS, preferred_element_type=f32)
    mm_tn = lambda a, b: lax.dot_general(a.astype(bf16), b.astype(bf16), TN_DIMS, preferred_element_type=f32)

    row1 = lax.broadcasted_iota(jnp.int32, (R, 1), 0)
    valid = c * seg + (row1 & (seg - 1)) < t_real
    ld = lambda ref: jnp.where(valid, ref[...].reshape(R, W), 0.0)
    r = ld(r_ref)
    k = ld(k_ref)
    v = ld(v_ref)
    lw = ld(lw_ref)
    ag = ld(ag_ref)

    lane = lax.broadcasted_iota(jnp.int32, (1, LANES), 1)
    m0 = (lane < HD_C).astype(f32)
    m1 = 1.0 - m0
    row2 = lax.broadcasted_iota(jnp.int32, (R2, R2), 0)
    col2 = lax.broadcasted_iota(jnp.int32, (R2, R2), 1)
    ones_bd = ((row2 < HD_C) == (col2 < HD_C)).astype(bf16)
    same_blk = (row2 >> shift) == (col2 >> shift)
    strict = jnp.logical_and(same_blk, row2 > col2)
    incl = jnp.logical_and(same_blk, row2 >= col2)
    eye = (row2 == col2).astype(f32)

    kk = k * kk_ref[...]
    kk = kk / jnp.maximum(jnp.sqrt(_head_sums(kk * kk, ones_bd, npair)), 1e-12)
    kp = k * (1.0 + (ag - 1.0) * ka_ref[...])
    bv = kk * ag

    rowR = lax.broadcasted_iota(jnp.int32, (R, R), 0)
    colR = lax.broadcasted_iota(jnp.int32, (R, R), 1)
    tril = jnp.logical_and((rowR >> shift) == (colR >> shift), rowR >= colR).astype(bf16)
    lw_hi, lw_lo = _split_hi_lo(lw)
    cw = (jnp.dot(tril, lw_hi, preferred_element_type=f32)
          + jnp.dot(tril, lw_lo, preferred_element_type=f32))
    cwl = jnp.concatenate(
        [jnp.broadcast_to(cw[j * seg + seg - 1:(j + 1) * seg, :], (seg, W)) for j in range(nseg)], axis=0)
    p_in = jnp.exp(cw)
    p_inv = jnp.exp(-cw)
    p_tail = jnp.exp(cwl - cw)
    at = -kk * jnp.exp(cw - lw)
    rt = r * p_in
    kt = kp * p_inv
    bt = bv * p_inv
    kh = kp * p_tail
    bh = bv * p_tail
    p_last = jnp.exp(cwl)

    def stack(x, gi):
        xg = x[:, gi * LANES:(gi + 1) * LANES]
        return jnp.concatenate([xg * m0, xg * m1], axis=0)

    def pick(x_st, j):
        if nseg == 1:
            return x_st
        return jnp.concatenate([x_st[j * seg:(j + 1) * seg], x_st[R + j * seg:R + (j + 1) * seg]], axis=0)

    def unpick(parts):
        if nseg == 1:
            return parts[0]
        return jnp.concatenate([q[0:seg] for q in parts] + [q[seg:2 * seg] for q in parts], axis=0)

    n_sq = shift - 1
    pairs = range(npair)
    s_prev = [[sbd[gi, j] for j in range(nseg)] for gi in pairs]
    a_st = [stack(at, gi) for gi in pairs]
    r_st = [stack(rt, gi) for gi in pairs]
    v_st = [stack(v, gi) for gi in pairs]
    prod = [mm_nt(jnp.concatenate([a_st[gi], r_st[gi]], axis=0),
                  jnp.concatenate([stack(kt, gi), stack(bt, gi)], axis=0)) for gi in pairs]
    m_ak = [jnp.where(strict, prod[gi][0:R2, 0:R2], 0.0) for gi in pairs]
    a_rk = [jnp.where(incl, prod[gi][R2:2 * R2, 0:R2], 0.0) for gi in pairs]
    a_rb = [jnp.where(incl, prod[gi][R2:2 * R2, R2:2 * R2], 0.0) for gi in pairs]

    x = [jnp.where(strict, prod[gi][0:R2, R2:2 * R2], 0.0) for gi in pairs]
    tinv = [eye + x[gi] for gi in pairs]
    for _ in range(n_sq):
        x = [mm(x[gi], x[gi]) for gi in pairs]
        tinv = [tinv[gi] + mm(tinv[gi], x[gi]) for gi in pairs]

    ar_s = [[mm_nt(jnp.concatenate([pick(a_st[gi], j), pick(r_st[gi], j)], axis=0), s_prev[gi][j])
             for j in range(nseg)] for gi in pairs]
    mv = [mm(m_ak[gi], v_st[gi]) for gi in pairs]
    u_st = [mm(tinv[gi], unpick([q[0:2 * seg] for q in ar_s[gi]]) + mv[gi]) for gi in pairs]
    o_st = [unpick([q[2 * seg:4 * seg] for q in ar_s[gi]])
            + mm(jnp.concatenate([a_rk[gi], a_rb[gi]], axis=1),
                 jnp.concatenate([v_st[gi], u_st[gi]], axis=0)) for gi in pairs]
    for gi in pairs:
        bh_st = stack(bh, gi)
        kh_st = stack(kh, gi)
        for j in range(nseg):
            pl_row = p_last[j * seg:j * seg + 1, gi * LANES:(gi + 1) * LANES]
            upd = mm_tn(jnp.concatenate([pick(u_st[gi], j), pick(v_st[gi], j)], axis=0),
                        jnp.concatenate([pick(bh_st, j), pick(kh_st, j)], axis=0))
            sbd[gi, j] = s_prev[gi][j] * pl_row + upd

    y = jnp.concatenate([o[0:R] + o[R:R2] for o in o_st], axis=1)
    inv_n = 1.0 / HD_C
    sums = _head_sums(jnp.concatenate([y, r * kp * rk_ref[...]], axis=0), ones_bd, npair)
    mu = sums[0:R] * inv_n
    dy = y - mu
    var = _head_sums(dy * dy, ones_bd, npair) * inv_n
    yn = dy * lax.rsqrt(var + GN_EPS) * lg_ref[...] + lb_ref[...]
    out = (yn + sums[R:R2] * v) * g_ref[...].reshape(R, W)
    y_ref[...] = out.reshape(y_ref.shape).astype(y_ref.dtype)


def _rwkv_scan(rkv, lw, ag, g, s0, kk_p, ka_p, rk_p, lnx_g, lnx_b, t_real, out_dtype, npair=16):
    _, B, T, D = rkv.shape
    seg = min(RWKV_TILE, T)
    nseg = RWKV_TILE // seg
    nc = T // seg
    assert nseg == 1 or nc == 1
    assert B % nseg == 0 and T % seg == 0
    W = npair * LANES
    tok = pl.BlockSpec((nseg, seg, W), lambda b, p, c: (b, c, p))
    rkv_spec = lambda s: pl.BlockSpec((None, nseg, seg, W), lambda b, p, c: (s, b, c, p))
    prm = pl.BlockSpec((1, W), lambda b, p, c: (0, p))
    st = pl.BlockSpec((nseg, 2 * npair, HD_C, HD_C), lambda b, p, c: (b, p, 0, 0))
    row = lambda a: a.reshape(1, D)
    return pl.pallas_call(
        functools.partial(_rwkv_scan_kernel, t_real=t_real, seg=seg, npair=npair),
        out_shape=[jax.ShapeDtypeStruct((B, T, D), out_dtype),
                   jax.ShapeDtypeStruct((B, NH_C, HD_C, HD_C), f32)],
        grid=(B // nseg, D // W, nc),
        in_specs=[rkv_spec(0), rkv_spec(1), rkv_spec(2), tok, tok, tok, st, prm, prm, prm, prm, prm],
        out_specs=[tok, st],
        scratch_shapes=[pltpu.VMEM((npair, nseg, 2 * RWKV_TILE, 2 * RWKV_TILE), f32)],
        compiler_params=_cparams("parallel", "parallel", "arbitrary"),
        name="rwkv_scan",
    )(rkv, rkv, rkv, lw, ag, g, s0, row(kk_p), row(ka_p), row(rk_p), row(lnx_g), row(lnx_b))


def _cast_rows_kernel(x_ref, o_ref):
    o_ref[...] = x_ref[...].astype(o_ref.dtype)


def _compact_rows_bf16(w, rows, n_blocks, skip_after, skip):
    K = w.shape[1]
    assert skip % SUBLANES == 0
    return pl.pallas_call(
        _cast_rows_kernel,
        out_shape=jax.ShapeDtypeStruct((n_blocks * rows, K), bf16),
        grid=(n_blocks,),
        in_specs=[pl.BlockSpec(
            (pl.Element(rows), pl.Element(K)),
            lambda i: (pl.multiple_of(i * rows + jnp.where(i >= skip_after, skip, 0), SUBLANES), 0))],
        out_specs=pl.BlockSpec((rows, K), lambda i: (i, 0)),
        compiler_params=_cparams("parallel"),
        name="compact_rows_bf16",
    )(w)


def _stack3_kernel(a_ref, b_ref, c_ref, o_ref):
    s = pl.program_id(0)
    for i, ref in enumerate((a_ref, b_ref, c_ref)):
        @pl.when(s == i)
        def _():
            o_ref[...] = ref[...].astype(o_ref.dtype)


def _stack3_bf16(a, b, c, rows=512):
    R, K = a.shape
    spec = lambda which: pl.BlockSpec((rows, K), lambda s, r: (jnp.where(s == which, r, 0), 0))
    return pl.pallas_call(
        _stack3_kernel,
        out_shape=jax.ShapeDtypeStruct((3, R, K), bf16),
        grid=(3, R // rows),
        in_specs=[spec(0), spec(1), spec(2)],
        out_specs=pl.BlockSpec((None, rows, K), lambda s, r: (s, r, 0)),
        compiler_params=_cparams("arbitrary", "arbitrary"),
        name="stack3_bf16",
    )(a, b, c)


def _pad_lora(w1, w2):
    rank = w1.shape[1]
    return (jnp.pad(w1, ((0, 0), (0, LORA_PAD - rank))), jnp.pad(w2, ((0, LORA_PAD - rank), (0, 0))))


def _prepare_params(p):
    w_in_t = jnp.swapaxes(p['w_in_ab'][0], 0, 1)
    n_qkvo = 4 * MIX_A
    n_g = 2 * NH_A
    q = {}
    q['w_in'] = _compact_rows_bf16(w_in_t, MIX_A, 6, n_qkvo // MIX_A, n_g)
    q['w_gate'] = jnp.stack(_split_hi_lo(jnp.pad(w_in_t[n_qkvo:n_qkvo + n_g], ((0, LANES - n_g), (0, 0)))))
    q['b_gate'] = jnp.pad(p['b_if_ab'][0], (0, LANES - n_g)).reshape(1, LANES)
    q['w_out_ab'] = p['w_out_ab'][0].astype(bf16)
    q['w_up'] = p['w_up'].astype(bf16)
    q['w_down'] = p['w_down'].astype(bf16)
    q['w_rkv'] = _stack3_bf16(p['rw_wr'][0], p['rw_wk'][0], p['rw_wv'][0])
    mu = p['rw_mu'][0]
    q['mu_rkv'] = jnp.stack([mu[0], mu[2], mu[3]])[:, None, :]
    q['mu_lora'] = jnp.stack([mu[1], mu[4], mu[5]])[:, None, :]
    w1, w2w = _pad_lora(p['rw_w1'][0], p['rw_w2'][0])
    a1, w2a = _pad_lora(p['rw_a1'][0], p['rw_a2'][0])
    q['lora_w1'] = jnp.concatenate([w1, a1, p['rw_g1'][0]], axis=1).astype(bf16)
    q['lora_w2w'] = w2w.astype(bf16)
    q['lora_w2a'] = w2a.astype(bf16)
    q['lora_w2g'] = p['rw_g2'][0].astype(bf16)
    q['w_o'] = p['rw_wo'][0].astype(bf16)
    return q


def _trunk(tok, states, p, q, t_real, out_from=0):
    mC, mn, mm, rgh, rgc, rwS, rwx = states
    B, T, D = tok.batch, tok.t_pad, tok.x.shape[-1]
    M = B * T
    row = lambda a: a.reshape(1, -1)

    act_dtype = bf16 if T % (2 * SUBLANES) == 0 else f32
    zq, zr, gates = _inproj(tok, q['w_in'], q['w_gate'], q['b_gate'], act_dtype)
    zq = zq.reshape(B, T, -1)
    zr = zr.reshape(B, T, -1)
    hm, c_new, n_new, m_new = _mlstm(zq, zr, gates.reshape(B, T, LANES), mC[0], mn[0], mm[0],
                                     p['mlstm_norm_g'][0], t_real, act_dtype)
    yr, h_last = _rglru(zr, rgc[0], rgh[0], p['rg_conv_w'][0], p['rg_conv_b'][0], p['rg_wa'][0],
                        p['rg_ba'][0], p['rg_wx'][0], p['rg_bx'][0], p['rg_lambda'][0], t_real, act_dtype)
    assert t_real >= CONV_W - 1
    conv_new = zr[:, t_real - (CONV_W - 1):t_real, MIX_A:MIX_A + RG_W]
    h1 = _proj_ln([(hm.reshape(M, MIX_A), MIX_A, 0), (yr.reshape(M, RG_W), RG_W, 0)],
                  q['w_out_ab'], tok, row(p['ln1_g'][0]), row(p['ln1_b'][0]))
    h2 = _mlp_ln(h1, q['w_up'], q['w_down'], 0, row(p['ln2_g'][0]), row(p['ln2_b'][0]))

    h2_3 = h2.reshape(B, T, D)
    shift0 = rwx[0].astype(f32)[:, None]
    rkv = _rwkv_rkv(h2_3, shift0, q['mu_rkv'], q['w_rkv'])
    lw, ag, g = _rwkv_lora(h2_3, shift0, q['mu_lora'], q['lora_w1'], q['lora_w2w'], q['lora_w2a'],
                           q['lora_w2g'], row(p['rw_w0'][0]), row(p['rw_a0'][0]))
    r3 = lambda a: a.reshape(B, T, D)
    xo, s_new = _rwkv_scan(rkv.reshape(3, B, T, D), r3(lw), r3(ag), r3(g), rwS[0], p['rw_kk'][0],
                           p['rw_ka'][0], p['rw_rk'][0], p['rw_lnx_g'][0], p['rw_lnx_b'][0], t_real,
                           act_dtype)
    xo = xo.reshape(M, D)
    h3 = _proj_ln([(xo, D, 0)], q['w_o'], h2, row(p['ln1_g'][1]), row(p['ln1_b'][1]))
    n_keep = t_real - out_from
    h4 = _mlp_ln(h3, q['w_up'], q['w_down'], 1, row(p['ln2_g'][1]), row(p['ln2_b'][1]),
                 window=None if (out_from == 0 and t_real == T) else (T, out_from, n_keep))
    shift_new = h2_3[:, t_real - 1]

    new_states = (c_new[None], n_new[None], m_new[None], h_last[None], conv_new[None],
                  s_new[None], shift_new[None])
    return h4.reshape(B, n_keep, D), new_states


def kernel(x_prompt, x_sample, state_mlstm_C, state_mlstm_n, state_mlstm_m, state_rglru_h, state_rglru_conv, state_rwkv_S, state_rwkv_shift, meta_tokens, w_in_ab, b_if_ab, mlstm_norm_g, rg_conv_w, rg_conv_b, rg_wa, rg_ba, rg_wx, rg_bx, rg_lambda, w_out_ab, rw_mu, rw_wr, rw_wk, rw_wv, rw_wo, rw_w0, rw_w1, rw_w2, rw_a0, rw_a1, rw_a2, rw_g1, rw_g2, rw_kk, rw_ka, rw_rk, rw_lnx_g, rw_lnx_b, ln1_g, ln1_b, ln2_g, ln2_b, w_up, w_down):
    p = dict(w_in_ab=w_in_ab, b_if_ab=b_if_ab, mlstm_norm_g=mlstm_norm_g, rg_conv_w=rg_conv_w,
             rg_conv_b=rg_conv_b, rg_wa=rg_wa, rg_ba=rg_ba, rg_wx=rg_wx, rg_bx=rg_bx,
             rg_lambda=rg_lambda, w_out_ab=w_out_ab, rw_mu=rw_mu, rw_wr=rw_wr, rw_wk=rw_wk,
             rw_wv=rw_wv, rw_wo=rw_wo, rw_w0=rw_w0, rw_w1=rw_w1, rw_w2=rw_w2, rw_a0=rw_a0,
             rw_a1=rw_a1, rw_a2=rw_a2, rw_g1=rw_g1, rw_g2=rw_g2, rw_kk=rw_kk, rw_ka=rw_ka,
             rw_rk=rw_rk, rw_lnx_g=rw_lnx_g, rw_lnx_b=rw_lnx_b, ln1_g=ln1_g, ln1_b=ln1_b,
             ln2_g=ln2_g, ln2_b=ln2_b, w_up=w_up, w_down=w_down)
    q = _prepare_params(p)
    B, S, D = x_prompt.shape
    dt = x_prompt.dtype
    t_prompt = N_META + S
    t_pad = -(-t_prompt // MLSTM_CHUNK) * MLSTM_CHUNK
    zero_states = (
        jnp.zeros((1, B, NH_A, HD_A, HD_A), f32),
        jnp.zeros((1, B, NH_A, HD_A), f32),
        jnp.zeros((1, B, NH_A), f32),
        jnp.zeros((1, B, RG_W), f32),
        jnp.zeros((1, B, CONV_W - 1, RG_W), dt),
        jnp.zeros((1, B, NH_C, HD_C, HD_C), f32),
        jnp.zeros((1, B, D), dt),
    )
    hp, ps = _trunk(_Tokens(x_prompt, meta_tokens.astype(dt), t_pad), zero_states, p, q, t_prompt,
                    out_from=N_META)
    sample_states = (state_mlstm_C, state_mlstm_n, state_mlstm_m, state_rglru_h, state_rglru_conv,
                     state_rwkv_S, state_rwkv_shift)
    hs, ss = _trunk(_Tokens(x_sample, None, x_sample.shape[1]), sample_states, p, q, x_sample.shape[1])
    return (hp, hs, *ps, *ss)
```

```python
import functools
from typing import NamedTuple, Optional

import jax
import jax.numpy as jnp
from jax import lax
from jax.experimental import pallas as pl
from jax.experimental.pallas import tpu as pltpu

f32 = jnp.float32
bf16 = jnp.bfloat16
HI = lax.Precision.HIGHEST

D_MODEL = 2048
N_META = 16
MIX_A = 1024
NH_A = 4
HD_A = 256
RG_W = 1024
RG_BLOCKS = 8
RG_BW = 128
CONV_W = 4
RG_C = 8.0
HD_C = 64
NH_C = 32
D_FF = 8192
LN_EPS = 1e-5
GN_EPS = 64e-5
DEPTH = 2
ALPHA = (2.0 * DEPTH) ** 0.25

LANES = 128
SUBLANES = 8
VMEM_LIMIT_BYTES = 56 * 1024 * 1024

MLSTM_CHUNK = 128
RWKV_TILE = 64
SHORT_SEQ_ROWS_CAP = 512
LORA_PAD = 128
NEG_BIG = -1e30

NT_DIMS = (((1,), (1,)), ((), ()))
TN_DIMS = (((0,), (0,)), ((), ()))


class _Tokens(NamedTuple):
    x: jax.Array
    lead: Optional[jax.Array]
    t_pad: int

    @property
    def batch(self):
        return self.x.shape[0]


def _cparams(*sem):
    return pltpu.CompilerParams(dimension_semantics=sem, vmem_limit_bytes=VMEM_LIMIT_BYTES)


def _pick_tile(n, cap):
    best = None
    for d in range(SUBLANES, min(n, cap) + 1, SUBLANES):
        if n % d == 0:
            best = d
    assert best is not None, (n, cap)
    return best


def _dot_bf16(a, b):
    return jnp.dot(a.astype(bf16), b.astype(bf16), preferred_element_type=f32)


def _dot_hi(a, b, dims=None):
    if dims is None:
        return jnp.dot(a, b, precision=HI, preferred_element_type=f32)
    return lax.dot_general(a, b, dims, precision=HI, preferred_element_type=f32)


def _split_hi_lo(x):
    hi = x.astype(bf16)
    return hi, (x - hi.astype(f32)).astype(bf16)


def _layer_norm_rows(t, g, b):
    mu = jnp.mean(t, axis=-1, keepdims=True)
    d = t - mu
    var = jnp.mean(d * d, axis=-1, keepdims=True)
    return d * lax.rsqrt(var + LN_EPS) * g + b


def _window_spec(tm, K, seq_rows, t_pad, n_lead, grid_rank, **kw):
    tps = t_pad // tm
    assert tps >= 2 and tps * tm == t_pad and t_pad - n_lead - seq_rows <= tm

    def index(g, *_):
        start = jnp.clip(tm * (g % tps) - n_lead, 0, seq_rows - tm)
        return (pl.multiple_of((g // tps) * seq_rows + start, SUBLANES), 0)

    assert grid_rank in (1, 2)
    return pl.BlockSpec((pl.Element(tm), pl.Element(K)), index, **kw)


def _virtual_rows(blk, lead_ref, i_seq, tps, n_tail):
    tm = blk.shape[0]
    n_lead = lead_ref.shape[0]
    first = jnp.concatenate([lead_ref[...], blk[:tm - n_lead]], axis=0)
    last = jnp.concatenate([blk[n_tail:], jnp.zeros((n_tail, blk.shape[1]), blk.dtype)], axis=0)
    return jnp.where(i_seq == 0, first, jnp.where(i_seq == tps - 1, last, blk))


def _inproj_kernel(x_ref, w_ref, wg_ref, bg_ref, *rest, virtual):
    if virtual is None:
        o1_ref, o2_ref, g_ref, xb_ref = rest
    else:
        lead_ref, o1_ref, o2_ref, g_ref, xb_ref = rest

    @pl.when(pl.program_id(1) == 0)
    def _():
        x = x_ref[...]
        if virtual is not None:
            tps, n_tail = virtual
            x = _virtual_rows(x, lead_ref, pl.program_id(0) % tps, tps, n_tail)
        x_hi, x_lo = _split_hi_lo(x)
        xb_ref[...] = x_hi
        nt = lambda a, b: lax.dot_general(a, b, NT_DIMS, preferred_element_type=f32)
        pre = (nt(x_hi, wg_ref[0]) + (nt(x_hi, wg_ref[1]) + nt(x_lo, wg_ref[0]))) + bg_ref[...]
        lane = lax.broadcasted_iota(jnp.int32, pre.shape, 1)
        g_ref[...] = jnp.where(lane < NH_A, pre,
                               jnp.where(lane < 2 * NH_A, jax.nn.log_sigmoid(pre), 0.0))

    y = lax.dot_general(xb_ref[...], w_ref[...], NT_DIMS, preferred_element_type=f32)
    j = pl.program_id(1)
    n_first = pl.num_programs(1) // 2

    @pl.when(j < n_first)
    def _():
        o1_ref[...] = y.astype(o1_ref.dtype)

    @pl.when(j >= n_first)
    def _():
        o2_ref[...] = y


def _inproj(tok, w, w_g, b_g, qkv_dtype, tm_cap=1088, tn=1024):
    x = tok.x.reshape(-1, tok.x.shape[-1])
    K = x.shape[1]
    M = tok.batch * tok.t_pad
    n_out = w.shape[0]
    n_half = n_out // 2
    nt = n_half // tn
    if tok.lead is None:
        tm = _pick_tile(M, tm_cap)
        x_specs, x_args, virtual = [pl.BlockSpec((tm, K), lambda i, j: (i, 0))], [x], None
    else:
        tm = _pick_tile(tok.t_pad, tm_cap)
        n_lead = tok.lead.shape[0]
        x_specs = [_window_spec(tm, K, tok.x.shape[1], tok.t_pad, n_lead, 2)]
        x_args = [x]
        virtual = (tok.t_pad // tm, tok.t_pad - n_lead - tok.x.shape[1])
    lead_specs = [] if tok.lead is None else [pl.BlockSpec(tok.lead.shape, lambda i, j: (0, 0))]
    lead_args = [] if tok.lead is None else [tok.lead]
    return pl.pallas_call(
        functools.partial(_inproj_kernel, virtual=virtual),
        out_shape=[jax.ShapeDtypeStruct((M, n_half), qkv_dtype), jax.ShapeDtypeStruct((M, n_half), f32),
                   jax.ShapeDtypeStruct((M, LANES), f32)],
        grid=(M // tm, n_out // tn),
        in_specs=x_specs + [pl.BlockSpec((tn, K), lambda i, j: (j, 0)),
                            pl.BlockSpec((2, LANES, K), lambda i, j: (0, 0, 0)),
                            pl.BlockSpec((1, LANES), lambda i, j: (0, 0))] + lead_specs,
        out_specs=[pl.BlockSpec((tm, tn), lambda i, j: (i, jnp.minimum(j, nt - 1))),
                   pl.BlockSpec((tm, tn), lambda i, j: (i, jnp.maximum(j - nt, 0))),
                   pl.BlockSpec((tm, LANES), lambda i, j: (i, 0))],
        scratch_shapes=[pltpu.VMEM((tm, K), bf16)],
        compiler_params=_cparams("parallel", "arbitrary"),
        name="inproj",
    )(*x_args, w, w_g, b_g, *lead_args)


def _proj_ln_kernel(*refs, n_x, virtual):
    x_refs = refs[:n_x]
    if virtual is None:
        w_ref, res_ref, g_ref, b_ref, o_ref = refs[n_x:]
        res = res_ref[...]
    else:
        w_ref, res_ref, g_ref, b_ref, lead_ref, o_ref = refs[n_x:]
        tps, n_tail = virtual
        res = _virtual_rows(res_ref[...], lead_ref, pl.program_id(0) % tps, tps, n_tail)
    y = None
    k0 = 0
    for x_ref in x_refs:
        kc = x_ref.shape[1]
        part = _dot_bf16(x_ref[...], w_ref[k0:k0 + kc, :])
        y = part if y is None else y + part
        k0 += kc
    t = ALPHA * res + y
    o_ref[...] = _layer_norm_rows(t, g_ref[...], b_ref[...])


def _proj_ln(xs, w, res, g, b, tm_cap=272):
    if isinstance(res, _Tokens) and res.lead is not None:
        tok = res
        N = tok.x.shape[-1]
        M = tok.batch * tok.t_pad
        tm = _pick_tile(tok.t_pad, tm_cap)
        n_lead = tok.lead.shape[0]
        res = tok.x.reshape(-1, N)
        res_spec = _window_spec(tm, N, tok.x.shape[1], tok.t_pad, n_lead, 1)
        virtual = (tok.t_pad // tm, tok.t_pad - n_lead - tok.x.shape[1])
        lead_specs, lead_args = [pl.BlockSpec(tok.lead.shape, lambda i: (0, 0))], [tok.lead]
    else:
        if isinstance(res, _Tokens):
            res = res.x.reshape(-1, res.x.shape[-1])
        M, N = res.shape
        tm = _pick_tile(M, tm_cap)
        res_spec = pl.BlockSpec((tm, N), lambda i: (i, 0))
        virtual, lead_specs, lead_args = None, [], []
    in_specs = []
    args = []
    for arr, width, blk in xs:
        in_specs.append(pl.BlockSpec((tm, width), functools.partial(lambda i, blk: (i, blk), blk=blk)))
        args.append(arr)
    K = w.shape[0]
    in_specs += [pl.BlockSpec((K, N), lambda i: (0, 0)),
                 res_spec,
                 pl.BlockSpec((1, N), lambda i: (0, 0)),
                 pl.BlockSpec((1, N), lambda i: (0, 0))] + lead_specs
    return pl.pallas_call(
        functools.partial(_proj_ln_kernel, n_x=len(xs), virtual=virtual),
        out_shape=jax.ShapeDtypeStruct((M, N), f32),
        grid=(M // tm,),
        in_specs=in_specs,
        out_specs=pl.BlockSpec((tm, N), lambda i: (i, 0)),
        compiler_params=_cparams("parallel"),
        name="proj_ln",
    )(*args, w, res, g, b, *lead_args)


def _mlp_ln_kernel(x_ref, wu_ref, wd_ref, g_ref, b_ref, o_ref, xb_ref):
    f = pl.program_id(1)

    @pl.when(f == 0)
    def _():
        xb_ref[...] = x_ref[...].astype(bf16)
        o_ref[...] = jnp.zeros_like(o_ref)

    a = jnp.dot(xb_ref[...], wu_ref[...], preferred_element_type=f32)
    a = jnp.square(jnp.maximum(a, 0.0))
    o_ref[...] += jnp.dot(a.astype(bf16), wd_ref[...], preferred_element_type=f32)

    @pl.when(f == pl.num_programs(1) - 1)
    def _():
        t = ALPHA * x_ref[...] + o_ref[...]
        o_ref[...] = _layer_norm_rows(t, g_ref[...], b_ref[...])


def _mlp_ln(x, w_up, w_down, layer, g, b, tm_cap=1088, tf=1024, window=None):
    M, K = x.shape
    F = w_up.shape[2]
    if window is None:
        tm = _pick_tile(M, tm_cap)
        x_spec = pl.BlockSpec((tm, K), lambda i, j: (i, 0), pipeline_mode=pl.Buffered(1))
    else:
        seq_rows, first, count = window
        assert first % SUBLANES == 0 and seq_rows % SUBLANES == 0
        tm = _pick_tile(count, tm_cap)
        per_seq = count // tm
        M = (M // seq_rows) * count
        x_spec = pl.BlockSpec(
            (pl.Element(tm), pl.Element(K)),
            lambda i, j: (pl.multiple_of((i // per_seq) * seq_rows + first + (i % per_seq) * tm, SUBLANES), 0),
            pipeline_mode=pl.Buffered(1))
    return pl.pallas_call(
        _mlp_ln_kernel,
        out_shape=jax.ShapeDtypeStruct((M, K), f32),
        grid=(M // tm, F // tf),
        in_specs=[x_spec,
                  pl.BlockSpec((None, K, tf), lambda i, j: (layer, 0, j)),
                  pl.BlockSpec((None, tf, K), lambda i, j: (layer, j, 0)),
                  pl.BlockSpec((1, K), lambda i, j: (0, 0)),
                  pl.BlockSpec((1, K), lambda i, j: (0, 0))],
        out_specs=pl.BlockSpec((tm, K), lambda i, j: (i, 0), pipeline_mode=pl.Buffered(1)),
        scratch_shapes=[pltpu.VMEM((tm, K), bf16)],
        compiler_params=_cparams("parallel", "arbitrary"),
        name="mlp_ln",
    )(x, w_up, w_down, g, b)


def _mlstm_kernel(q_ref, k_ref, v_ref, o_ref, g_ref, c0_ref, n0_ref, m0_ref, ng_ref,
                  h_ref, c_ref, n_ref, m_ref, *scratch, t_real, carry):
    nb, lb = q_ref.shape[0], q_ref.shape[1]
    R = nb * lb
    c = pl.program_id(1)
    if carry:
        cs, ns, ms = scratch

        @pl.when(c == 0)
        def _():
            cs[...] = c0_ref[...]
            ns[...] = n0_ref[...]
            ms[...] = m0_ref[...]
    else:
        cs, ns, ms = c0_ref, n0_ref, m0_ref

    heads = range(NH_A)
    seqs = range(nb)
    hs = lambda h: slice(h * HD_A, (h + 1) * HD_A)
    rows = lambda j: slice(j * lb, (j + 1) * lb)
    t_col = c * lb + lax.broadcasted_iota(jnp.int32, (nb, lb, 1), 1).reshape(R, 1)
    vrow = t_col < t_real
    rowi = lax.broadcasted_iota(jnp.int32, (R, R), 0)
    coli = lax.broadcasted_iota(jnp.int32, (R, R), 1)
    if nb == 1:
        causal = rowi >= coli
    else:
        shift = lb.bit_length() - 1
        causal = jnp.logical_and((rowi >> shift) == (coli >> shift), rowi >= coli)

    def per_seq(x):
        return jnp.broadcast_to(x, (nb, lb, x.shape[-1])).reshape(R, x.shape[-1])

    def last_row(x):
        return x.reshape(nb, lb, x.shape[-1])[:, lb - 1:lb, :]

    ld = lambda ref, h: jnp.where(vrow, ref[:, :, hs(h)].astype(f32).reshape(R, HD_A), 0.0)
    q = [ld(q_ref, h) for h in heads]
    k = [ld(k_ref, h) * (HD_A ** -0.5) for h in heads]
    vb = [ld(v_ref, h).astype(bf16) for h in heads]
    qb = [x.astype(bf16) for x in q]
    qk = [lax.dot_general(qb[h], k[h].astype(bf16), NT_DIMS, preferred_element_type=f32) for h in heads]

    gate = jnp.where(vrow, g_ref[...].reshape(R, LANES), 0.0)
    fcum = _dot_hi(causal.astype(f32), gate)
    gate_t = gate.T
    fcum_t = fcum.T
    vcol = (c * lb + (lax.broadcasted_iota(jnp.int32, (1, R), 1) & (lb - 1))) < t_real

    m_prev, fc, igc, fl, m_row, inter, s = [], [], [], [], [], [], []
    for h in heads:
        fc.append(fcum[:, NH_A + h:NH_A + h + 1])
        fr = fcum_t[NH_A + h:NH_A + h + 1, :]
        igc.append(jnp.where(vrow, gate[:, h:h + 1], NEG_BIG))
        igr = jnp.where(vcol, gate_t[h:h + 1, :], NEG_BIG)
        m_prev.append(ms[:, h, :, 0:1])
        logd = jnp.where(causal, fc[h] - fr + igr, NEG_BIG)
        b_in = fc[h] + per_seq(m_prev[h])
        m_row.append(jnp.maximum(b_in, jnp.max(logd, axis=1, keepdims=True)))
        inter.append(jnp.exp(b_in - m_row[h]))
        s.append(qk[h] * jnp.exp(logd - m_row[h]))
        fl.append(last_row(fc[h]))

    sv = [jnp.dot(s[h].astype(bf16), vb[h], preferred_element_type=f32) for h in heads]
    qc = [jnp.concatenate([jnp.dot(qb[h][rows(j)], cs[j, h].astype(bf16), preferred_element_type=f32)
                           for j in seqs], axis=0) for h in heads]
    for h in heads:
        num = sv[h] + inter[h] * qc[h]
        den = jnp.sum(s[h], axis=1, keepdims=True) \
            + inter[h] * jnp.sum(q[h] * per_seq(ns[:, h]), axis=1, keepdims=True)
        hh = num / jnp.maximum(jnp.abs(den), jnp.exp(-m_row[h]))
        mu = jnp.mean(hh, axis=1, keepdims=True)
        dv = hh - mu
        var = jnp.mean(dv * dv, axis=1, keepdims=True)
        hn = dv * lax.rsqrt(var + LN_EPS) * ng_ref[h]
        out = hn * jax.nn.sigmoid(ld(o_ref, h))
        h_ref[:, :, hs(h)] = out.reshape(nb, lb, HD_A).astype(h_ref.dtype)

    kw, m_new, dec = [], [], []
    for h in heads:
        wj = per_seq(fl[h]) - fc[h] + igc[h]
        wmax = jnp.max(wj.reshape(nb, lb, 1), axis=1, keepdims=True)
        m_new.append(jnp.maximum(fl[h] + m_prev[h], wmax))
        dec.append(jnp.exp(fl[h] + m_prev[h] - m_new[h]))
        kw.append(k[h] * jnp.exp(wj - per_seq(m_new[h])))
    kv = [[lax.dot_general(kw[h][rows(j)].astype(bf16), vb[h][rows(j)], TN_DIMS,
                           preferred_element_type=f32) for j in seqs] for h in heads]
    c_dst, n_dst, m_dst = (cs, ns, ms) if carry else (c_ref, n_ref, m_ref)
    for h in heads:
        n_new = dec[h] * ns[:, h] + jnp.sum(kw[h].reshape(nb, lb, HD_A), axis=1, keepdims=True)
        for j in seqs:
            c_dst[j, h] = dec[h][j] * cs[j, h] + kv[h][j]
        n_dst[:, h] = n_new
        m_dst[:, h] = jnp.broadcast_to(m_new[h], (nb, 1, LANES))

    if carry:
        @pl.when(c == pl.num_programs(1) - 1)
        def _():
            c_ref[...] = cs[...]
            n_ref[...] = ns[...]
            m_ref[...] = ms[...]


def _mlstm(zq, zr, gates, c0, n0, m0, norm_g, t_real, out_dtype, nb_short=8):
    B, T, _ = zq.shape
    if T <= SUBLANES:
        nb, lb = nb_short, T
    else:
        nb, lb = 1, MLSTM_CHUNK
    nc = T // lb
    carry = nc > 1
    assert B % nb == 0 and T % lb == 0 and (carry or t_real == T)
    m0b = jnp.broadcast_to(m0[:, :, None, None], (B, NH_A, 1, LANES))
    zspec = lambda group: pl.BlockSpec((nb, lb, MIX_A), lambda b, c: (b, c, group))
    st_spec = lambda r, w: pl.BlockSpec((nb, NH_A, r, w), lambda b, c: (b, 0, 0, 0))
    st_shapes = [(HD_A, HD_A), (1, HD_A), (1, LANES)]
    outs = pl.pallas_call(
        functools.partial(_mlstm_kernel, t_real=t_real, carry=carry),
        out_shape=[jax.ShapeDtypeStruct((B, T, MIX_A), out_dtype)]
        + [jax.ShapeDtypeStruct((B, NH_A) + s, f32) for s in st_shapes],
        grid=(B // nb, nc),
        in_specs=[zspec(0), zspec(1), zspec(2), zspec(0),
                  pl.BlockSpec((nb, lb, LANES), lambda b, c: (b, c, 0))]
        + [st_spec(*s) for s in st_shapes]
        + [pl.BlockSpec((NH_A, 1, HD_A), lambda b, c: (0, 0, 0))],
        out_specs=[pl.BlockSpec((nb, lb, MIX_A), lambda b, c: (b, c, 0))]
        + [st_spec(*s) for s in st_shapes],
        scratch_shapes=[pltpu.VMEM((nb, NH_A) + s, f32) for s in st_shapes] if carry else [],
        compiler_params=_cparams("parallel", "arbitrary"),
        name="mlstm",
    )(zq, zq, zq, zr, gates, c0, n0.reshape(B, NH_A, 1, HD_A), m0b, norm_g.reshape(NH_A, 1, HD_A))
    hm, c_new, n_new, m_new = outs
    return hm, c_new, n_new.reshape(B, NH_A, HD_A), m_new[:, :, 0, 0]


def _one_minus_exp(x):
    t = jnp.tanh(0.5 * x)
    return -2.0 * t / (1.0 - t)


def _rglru_kernel(xr_ref, gr_ref, conv0_ref, h0_ref, cw_ref, cb_ref, wa_ref, ba_ref, wx_ref,
                  bx_ref, lam_ref, y_ref, hl_ref, xp, a_s, h_s, hc, *, t_real):
    nb, lr = xr_ref.shape[0], xr_ref.shape[1]
    rows = nb * lr
    ngroup = rows // SUBLANES
    c = pl.program_id(1)
    hist = SUBLANES - (CONV_W - 1)

    @pl.when(c == 0)
    def _():
        xp[:, 0:SUBLANES, :] = jnp.zeros((nb, SUBLANES, RG_W), f32)
        xp[:, hist:SUBLANES, :] = conv0_ref[...]
        hc[...] = h0_ref[...]

    xp[:, SUBLANES:SUBLANES + lr, :] = xr_ref[...]
    xc = cb_ref[...]
    for j in range(CONV_W):
        xc = xc + xp[:, hist + j:hist + j + lr, :] * cw_ref[j:j + 1, :]
    xp[:, hist:SUBLANES, :] = xp[:, hist + lr:SUBLANES + lr, :]
    xc = xc.reshape(rows, RG_W)

    xcb = xc.astype(bf16)
    ra = []
    rx = []
    for n in range(RG_BLOCKS):
        xn = xcb[:, n * RG_BW:(n + 1) * RG_BW]
        ra.append(jnp.dot(xn, wa_ref[n].astype(bf16), preferred_element_type=f32))
        rx.append(jnp.dot(xn, wx_ref[n].astype(bf16), preferred_element_type=f32))
    r = jax.nn.sigmoid(jnp.concatenate(ra, axis=1) + ba_ref[...])
    gi = jax.nn.sigmoid(jnp.concatenate(rx, axis=1) + bx_ref[...])
    log_a = -RG_C * r * jax.nn.softplus(-lam_ref[...])
    t_idx = c * lr + lax.broadcasted_iota(jnp.int32, (nb, lr, 1), 1).reshape(rows, 1)
    valid = t_idx < t_real
    a = jnp.where(valid, jnp.exp(log_a), 1.0)
    u = jnp.where(valid, jnp.sqrt(_one_minus_exp(2.0 * log_a)) * (gi * xc), 0.0)

    a = a.reshape(ngroup, SUBLANES, RG_W)
    u = u.reshape(ngroup, SUBLANES, RG_W)
    r8 = lax.broadcasted_iota(jnp.int32, (1, SUBLANES, 1), 1)
    d = 1
    while d < SUBLANES:
        inside = r8 >= d
        a_sh = jnp.where(inside, pltpu.roll(a, d, axis=1), 1.0)
        u_sh = jnp.where(inside, pltpu.roll(u, d, axis=1), 0.0)
        u = a * u_sh + u
        a = a * a_sh
        d *= 2

    if lr == SUBLANES:
        h = u + a * hc[...]
        h_last = h[:, SUBLANES - 1:SUBLANES, :]
        h = h.reshape(rows, RG_W)
    else:
        assert nb == 1
        a_s[...] = a.reshape(rows, RG_W)
        h_s[...] = u.reshape(rows, RG_W)

        def step(g, h_prev):
            i = pl.multiple_of(g * SUBLANES, SUBLANES)
            hg = h_s[pl.ds(i, SUBLANES), :] + a_s[pl.ds(i, SUBLANES), :] * h_prev
            h_s[pl.ds(i, SUBLANES), :] = hg
            return hg[SUBLANES - 1:SUBLANES, :]

        h_last = lax.fori_loop(0, ngroup, step, hc[0]).reshape(1, 1, RG_W)
        h = h_s[...]
    hc[...] = h_last
    y_ref[...] = (h * jax.nn.gelu(gr_ref[...].reshape(rows, RG_W))).reshape(nb, lr, RG_W).astype(y_ref.dtype)

    @pl.when(c == pl.num_programs(1) - 1)
    def _():
        hl_ref[...] = h_last


def _rglru(zr, conv0, h0, conv_w, conv_b, wa, ba, wx, bx, lam, t_real, out_dtype, lr_cap=544, nb_short=16):
    B, T, _ = zr.shape
    if T == SUBLANES:
        nb, lr = nb_short, T
    else:
        nb, lr = 1, _pick_tile(T, lr_cap)
    assert B % nb == 0
    row = lambda a: a.reshape(1, RG_W)
    full = lambda shp: pl.BlockSpec(shp, lambda b, c: (0,) * len(shp))
    y, h_last = pl.pallas_call(
        functools.partial(_rglru_kernel, t_real=t_real),
        out_shape=[jax.ShapeDtypeStruct((B, T, RG_W), out_dtype),
                   jax.ShapeDtypeStruct((B, 1, RG_W), f32)],
        grid=(B // nb, T // lr),
        in_specs=[pl.BlockSpec((nb, lr, RG_W), lambda b, c: (b, c, 1)),
                  pl.BlockSpec((nb, lr, RG_W), lambda b, c: (b, c, 2)),
                  pl.BlockSpec((nb, CONV_W - 1, RG_W), lambda b, c: (b, 0, 0)),
                  pl.BlockSpec((nb, 1, RG_W), lambda b, c: (b, 0, 0)),
                  full((CONV_W, RG_W)), full((1, RG_W)),
                  full((RG_BLOCKS, RG_BW, RG_BW)), full((1, RG_W)),
                  full((RG_BLOCKS, RG_BW, RG_BW)), full((1, RG_W)), full((1, RG_W))],
        out_specs=[pl.BlockSpec((nb, lr, RG_W), lambda b, c: (b, c, 0)),
                   pl.BlockSpec((nb, 1, RG_W), lambda b, c: (b, 0, 0))],
        scratch_shapes=[pltpu.VMEM((nb, SUBLANES + lr, RG_W), f32), pltpu.VMEM((nb * lr, RG_W), f32),
                        pltpu.VMEM((nb * lr, RG_W), f32), pltpu.VMEM((nb, 1, RG_W), f32)],
        compiler_params=_cparams("parallel", "arbitrary"),
        name="rglru",
    )(zr, zr, conv0, h0.reshape(B, 1, RG_W), conv_w, row(conv_b), wa, row(ba), wx, row(bx), row(lam))
    return y, h_last.reshape(B, RG_W)


def _token_shift(x_ref, st_ref, xp_ref, c):
    tt = x_ref.shape[1]
    last = SUBLANES - 1

    @pl.when(c == 0)
    def _():
        xp_ref[:, last:SUBLANES, :] = st_ref[...]

    xp_ref[:, SUBLANES:2 * SUBLANES, :] = x_ref[:, 0:SUBLANES, :]
    xprev = xp_ref[:, last:last + SUBLANES, :]
    if tt > SUBLANES:
        xprev = jnp.concatenate([xprev, x_ref[:, last:tt - 1, :]], axis=1)
    xp_ref[:, last:SUBLANES, :] = x_ref[:, tt - 1:tt, :]
    return x_ref[...], xprev


def _seq_tile(B, T, rows_cap):
    if T <= SUBLANES:
        return _pick_tile(B * T, min(rows_cap, SHORT_SEQ_ROWS_CAP)) // T, T
    return 1, _pick_tile(T, rows_cap)


def _rwkv_rkv_kernel(x_ref, st_ref, mu_ref, w_ref, o_ref, xs_ref, xp_ref):
    c = pl.program_id(1)
    s = pl.program_id(2)

    @pl.when(jnp.logical_and(s == 0, pl.program_id(3) == 0))
    def _():
        x, xprev = _token_shift(x_ref, st_ref, xp_ref, c)
        dx = xprev - x
        for si in range(3):
            xs_ref[si] = (x + dx * mu_ref[si]).reshape(xs_ref.shape[1:]).astype(bf16)

    o_ref[...] = jnp.dot(xs_ref[s], w_ref[...], preferred_element_type=f32)


def _rwkv_rkv(x, shift0, mu3, w3, rows_cap=1088):
    B, T, K = x.shape
    nb, tt = _seq_tile(B, T, rows_cap)
    rows = nb * tt
    nc = T // tt
    tn = K if rows <= SHORT_SEQ_ROWS_CAP else K // 2
    return pl.pallas_call(
        _rwkv_rkv_kernel,
        out_shape=jax.ShapeDtypeStruct((3, B * T, K), f32),
        grid=(B // nb, nc, 3, K // tn),
        in_specs=[pl.BlockSpec((nb, tt, K), lambda b, c, s, j: (b, c, 0), pipeline_mode=pl.Buffered(1)),
                  pl.BlockSpec((nb, 1, K), lambda b, c, s, j: (b, 0, 0)),
                  pl.BlockSpec((3, 1, K), lambda b, c, s, j: (0, 0, 0)),
                  pl.BlockSpec((None, K, tn), lambda b, c, s, j: (s, 0, j))],
        out_specs=pl.BlockSpec((None, rows, tn), lambda b, c, s, j: (s, b * nc + c, j)),
        scratch_shapes=[pltpu.VMEM((3, rows, K), bf16), pltpu.VMEM((nb, 2 * SUBLANES, K), f32)],
        compiler_params=_cparams("parallel", "arbitrary", "arbitrary", "arbitrary"),
        name="rwkv_rkv",
    )(x, shift0, mu3, w3)


def _rwkv_lora_kernel(x_ref, st_ref, mu_ref, w1_ref, w2w_ref, w2a_ref, w2g_ref, w0_ref, a0_ref,
                      lw_ref, a_ref, g_ref, xp_ref):
    x, xprev = _token_shift(x_ref, st_ref, xp_ref, pl.program_id(1))
    dx = xprev - x
    mix = lambda i: (x + dx * mu_ref[i]).reshape(lw_ref.shape).astype(bf16)
    r = LORA_PAD
    hw = jnp.tanh(jnp.dot(mix(0), w1_ref[:, 0:r], preferred_element_type=f32))
    ha = jnp.dot(mix(1), w1_ref[:, r:2 * r], preferred_element_type=f32)
    hg = jax.nn.sigmoid(jnp.dot(mix(2), w1_ref[:, 2 * r:], preferred_element_type=f32))
    w = w0_ref[...] + jnp.dot(hw.astype(bf16), w2w_ref[...], preferred_element_type=f32)
    w = -jax.nn.softplus(-w) - 0.5
    lw_ref[...] = -jnp.exp(w)
    a = a0_ref[...] + jnp.dot(ha.astype(bf16), w2a_ref[...], preferred_element_type=f32)
    a_ref[...] = jax.nn.sigmoid(a)
    g_ref[...] = jnp.dot(hg.astype(bf16), w2g_ref[...], preferred_element_type=f32)


def _rwkv_lora(x, shift0, mu3, w1, w2w, w2a, w2g, w0, a0, rows_cap=272):
    B, T, K = x.shape
    nb, tt = _seq_tile(B, T, rows_cap)
    rows = nb * tt
    nc = T // tt
    full = lambda a: pl.BlockSpec(a.shape, lambda b, c: (0,) * a.ndim)
    out = pl.BlockSpec((rows, K), lambda b, c: (b * nc + c, 0))
    return pl.pallas_call(
        _rwkv_lora_kernel,
        out_shape=[jax.ShapeDtypeStruct((B * T, K), f32)] * 3,
        grid=(B // nb, nc),
        in_specs=[pl.BlockSpec((nb, tt, K), lambda b, c: (b, c, 0)),
                  pl.BlockSpec((nb, 1, K), lambda b, c: (b, 0, 0)),
                  full(mu3), full(w1), full(w2w), full(w2a), full(w2g), full(w0), full(a0)],
        out_specs=[out, out, out],
        scratch_shapes=[pltpu.VMEM((nb, 2 * SUBLANES, K), f32)],
        compiler_params=_cparams("parallel", "arbitrary"),
        name="rwkv_lora",
    )(x, shift0, mu3, w1, w2w, w2a, w2g, w0, a0)


def _head_sums(x, ones_bd, npair):
    rows = x.shape[0]
    xs = jnp.concatenate([x[:, g * LANES:(g + 1) * LANES] for g in range(npair)], axis=0)
    s = jnp.dot(xs.astype(bf16), ones_bd, preferred_element_type=f32)
    return jnp.concatenate([s[g * rows:(g + 1) * rows] for g in range(npair)], axis=1)


def _rwkv_scan_kernel(r_ref, k_ref, v_ref, lw_ref, ag_ref, g_ref, s0_ref, kk_ref, ka_ref, rk_ref,
                      lg_ref, lb_ref, y_ref, s_ref, sbd, *, t_real, seg, npair):
    R2 = 2 * RWKV_TILE
    nseg = RWKV_TILE // seg
    c = pl.program_id(2)

    @pl.when(c == 0)
    def _():
        sbd[...] = jnp.zeros(sbd.shape, f32)
        for gi in range(npair):
            for j in range(nseg):
                sbd[gi, j, 0:HD_C, 0:HD_C] = s0_ref[j, 2 * gi]
                sbd[gi, j, HD_C:R2, HD_C:R2] = s0_ref[j, 2 * gi + 1]

    tile_live = c * seg < t_real

    @pl.when(jnp.logical_not(tile_live))
    def _():
        y_ref[...] = jnp.zeros(y_ref.shape, y_ref.dtype)

    @pl.when(tile_live)
    def _():
        _rwkv_tile(c, r_ref, k_ref, v_ref, lw_ref, ag_ref, g_ref, kk_ref, ka_ref, rk_ref, lg_ref, lb_ref,
                   y_ref, sbd, t_real=t_real, seg=seg, npair=npair)

    @pl.when(c == pl.num_programs(2) - 1)
    def _():
        for gi in range(npair):
            for j in range(nseg):
                s_ref[j, 2 * gi] = sbd[gi, j, 0:HD_C, 0:HD_C]
                s_ref[j, 2 * gi + 1] = sbd[gi, j, HD_C:R2, HD_C:R2]


def _rwkv_tile(c, r_ref, k_ref, v_ref, lw_ref, ag_ref, g_ref, kk_ref, ka_ref, rk_ref, lg_ref, lb_ref,
               y_ref, sbd, *, t_real, seg, npair):
    R = RWKV_TILE
    R2 = 2 * R
    nseg = R // seg
    W = npair * LANES
    shift = seg.bit_length() - 1
    mm = lambda a, b: jnp.dot(a.astype(bf16), b.astype(bf16), preferred_element_type=f32)
    mm_nt = lambda a, b: lax.dot_general(a.astype(bf16), b.astype(bf16), NT_DIMS, preferred_element_type=f32)
    mm_tn = lambda a, b: lax.dot_general(a.astype(bf16), b.astype(bf16), TN_DIMS, preferred_element_type=f32)

    row1 = lax.broadcasted_iota(jnp.int32, (R, 1), 0)
    valid = c * seg + (row1 & (seg - 1)) < t_real
    ld = lambda ref: jnp.where(valid, ref[...].reshape(R, W), 0.0)
    r = ld(r_ref)
    k = ld(k_ref)
    v = ld(v_ref)
    lw = ld(lw_ref)
    ag = ld(ag_ref)

    lane = lax.broadcasted_iota(jnp.int32, (1, LANES), 1)
    m0 = (lane < HD_C).astype(f32)
    m1 = 1.0 - m0
    row2 = lax.broadcasted_iota(jnp.int32, (R2, R2), 0)
    col2 = lax.broadcasted_iota(jnp.int32, (R2, R2), 1)
    ones_bd = ((row2 < HD_C) == (col2 < HD_C)).astype(bf16)
    same_blk = (row2 >> shift) == (col2 >> shift)
    strict = jnp.logical_and(same_blk, row2 > col2)
    incl = jnp.logical_and(same_blk, row2 >= col2)
    eye = (row2 == col2).astype(f32)

    kk = k * kk_ref[...]
    kk = kk / jnp.maximum(jnp.sqrt(_head_sums(kk * kk, ones_bd, npair)), 1e-12)
    kp = k * (1.0 + (ag - 1.0) * ka_ref[...])
    bv = kk * ag

    rowR = lax.broadcasted_iota(jnp.int32, (R, R), 0)
    colR = lax.broadcasted_iota(jnp.int32, (R, R), 1)
    tril = jnp.logical_and((rowR >> shift) == (colR >> shift), rowR >= colR).astype(bf16)
    lw_hi, lw_lo = _split_hi_lo(lw)
    cw = (jnp.dot(tril, lw_hi, preferred_element_type=f32)
          + jnp.dot(tril, lw_lo, preferred_element_type=f32))
    cwl = jnp.concatenate(
        [jnp.broadcast_to(cw[j * seg + seg - 1:(j + 1) * seg, :], (seg, W)) for j in range(nseg)], axis=0)
    p_in = jnp.exp(cw)
    p_inv = jnp.exp(-cw)
    p_tail = jnp.exp(cwl - cw)
    at = -kk * jnp.exp(cw - lw)
    rt = r * p_in
    kt = kp * p_inv
    bt = bv * p_inv
    kh = kp * p_tail
    bh = bv * p_tail
    p_last = jnp.exp(cwl)

    def stack(x, gi):
        xg = x[:, gi * LANES:(gi + 1) * LANES]
        return jnp.concatenate([xg * m0, xg * m1], axis=0)

    def pick(x_st, j):
        if nseg == 1:
            return x_st
        return jnp.concatenate([x_st[j * seg:(j + 1) * seg], x_st[R + j * seg:R + (j + 1) * seg]], axis=0)

    def unpick(parts):
        if nseg == 1:
            return parts[0]
        return jnp.concatenate([q[0:seg] for q in parts] + [q[seg:2 * seg] for q in parts], axis=0)

    n_sq = shift - 1
    pairs = range(npair)
    s_prev = [[sbd[gi, j] for j in range(nseg)] for gi in pairs]
    a_st = [stack(at, gi) for gi in pairs]
    r_st = [stack(rt, gi) for gi in pairs]
    v_st = [stack(v, gi) for gi in pairs]
    prod = [mm_nt(jnp.concatenate([a_st[gi], r_st[gi]], axis=0),
                  jnp.concatenate([stack(kt, gi), stack(bt, gi)], axis=0)) for gi in pairs]
    m_ak = [jnp.where(strict, prod[gi][0:R2, 0:R2], 0.0) for gi in pairs]
    a_rk = [jnp.where(incl, prod[gi][R2:2 * R2, 0:R2], 0.0) for gi in pairs]
    a_rb = [jnp.where(incl, prod[gi][R2:2 * R2, R2:2 * R2], 0.0) for gi in pairs]

    x = [jnp.where(strict, prod[gi][0:R2, R2:2 * R2], 0.0) for gi in pairs]
    tinv = [eye + x[gi] for gi in pairs]
    for _ in range(n_sq):
        x = [mm(x[gi], x[gi]) for gi in pairs]
        tinv = [tinv[gi] + mm(tinv[gi], x[gi]) for gi in pairs]

    ar_s = [[mm_nt(jnp.concatenate([pick(a_st[gi], j), pick(r_st[gi], j)], axis=0), s_prev[gi][j])
             for j in range(nseg)] for gi in pairs]
    mv = [mm(m_ak[gi], v_st[gi]) for gi in pairs]
    u_st = [mm(tinv[gi], unpick([q[0:2 * seg] for q in ar_s[gi]]) + mv[gi]) for gi in pairs]
    o_st = [unpick([q[2 * seg:4 * seg] for q in ar_s[gi]])
            + mm(jnp.concatenate([a_rk[gi], a_rb[gi]], axis=1),
                 jnp.concatenate([v_st[gi], u_st[gi]], axis=0)) for gi in pairs]
    for gi in pairs:
        bh_st = stack(bh, gi)
        kh_st = stack(kh, gi)
        for j in range(nseg):
            pl_row = p_last[j * seg:j * seg + 1, gi * LANES:(gi + 1) * LANES]
            upd = mm_tn(jnp.concatenate([pick(u_st[gi], j), pick(v_st[gi], j)], axis=0),
                        jnp.concatenate([pick(bh_st, j), pick(kh_st, j)], axis=0))
            sbd[gi, j] = s_prev[gi][j] * pl_row + upd

    y = jnp.concatenate([o[0:R] + o[R:R2] for o in o_st], axis=1)
    inv_n = 1.0 / HD_C
    sums = _head_sums(jnp.concatenate([y, r * kp * rk_ref[...]], axis=0), ones_bd, npair)
    mu = sums[0:R] * inv_n
    dy = y - mu
    var = _head_sums(dy * dy, ones_bd, npair) * inv_n
    yn = dy * lax.rsqrt(var + GN_EPS) * lg_ref[...] + lb_ref[...]
    out = (yn + sums[R:R2] * v) * g_ref[...].reshape(R, W)
    y_ref[...] = out.reshape(y_ref.shape).astype(y_ref.dtype)


def _rwkv_scan(rkv, lw, ag, g, s0, kk_p, ka_p, rk_p, lnx_g, lnx_b, t_real, out_dtype, npair=16):
    _, B, T, D = rkv.shape
    seg = min(RWKV_TILE, T)
    nseg = RWKV_TILE // seg
    nc = T // seg
    assert nseg == 1 or nc == 1
    assert B % nseg == 0 and T % seg == 0
    W = npair * LANES
    tok = pl.BlockSpec((nseg, seg, W), lambda b, p, c: (b, c, p))
    rkv_spec = lambda s: pl.BlockSpec((None, nseg, seg, W), lambda b, p, c: (s, b, c, p))
    prm = pl.BlockSpec((1, W), lambda b, p, c: (0, p))
    st = pl.BlockSpec((nseg, 2 * npair, HD_C, HD_C), lambda b, p, c: (b, p, 0, 0))
    row = lambda a: a.reshape(1, D)
    return pl.pallas_call(
        functools.partial(_rwkv_scan_kernel, t_real=t_real, seg=seg, npair=npair),
        out_shape=[jax.ShapeDtypeStruct((B, T, D), out_dtype),
                   jax.ShapeDtypeStruct((B, NH_C, HD_C, HD_C), f32)],
        grid=(B // nseg, D // W, nc),
        in_specs=[rkv_spec(0), rkv_spec(1), rkv_spec(2), tok, tok, tok, st, prm, prm, prm, prm, prm],
        out_specs=[tok, st],
        scratch_shapes=[pltpu.VMEM((npair, nseg, 2 * RWKV_TILE, 2 * RWKV_TILE), f32)],
        compiler_params=_cparams("parallel", "parallel", "arbitrary"),
        name="rwkv_scan",
    )(rkv, rkv, rkv, lw, ag, g, s0, row(kk_p), row(ka_p), row(rk_p), row(lnx_g), row(lnx_b))


def _cast_rows_kernel(x_ref, o_ref):
    o_ref[...] = x_ref[...].astype(o_ref.dtype)


def _compact_rows_bf16(w, rows, n_blocks, skip_after, skip):
    K = w.shape[1]
    assert skip % SUBLANES == 0
    return pl.pallas_call(
        _cast_rows_kernel,
        out_shape=jax.ShapeDtypeStruct((n_blocks * rows, K), bf16),
        grid=(n_blocks,),
        in_specs=[pl.BlockSpec(
            (pl.Element(rows), pl.Element(K)),
            lambda i: (pl.multiple_of(i * rows + jnp.where(i >= skip_after, skip, 0), SUBLANES), 0))],
        out_specs=pl.BlockSpec((rows, K), lambda i: (i, 0)),
        compiler_params=_cparams("parallel"),
        name="compact_rows_bf16",
    )(w)


def _stack3_kernel(a_ref, b_ref, c_ref, o_ref):
    s = pl.program_id(0)
    for i, ref in enumerate((a_ref, b_ref, c_ref)):
        @pl.when(s == i)
        def _():
            o_ref[...] = ref[...].astype(o_ref.dtype)


def _stack3_bf16(a, b, c, rows=512):
    R, K = a.shape
    spec = lambda which: pl.BlockSpec((rows, K), lambda s, r: (jnp.where(s == which, r, 0), 0))
    return pl.pallas_call(
        _stack3_kernel,
        out_shape=jax.ShapeDtypeStruct((3, R, K), bf16),
        grid=(3, R // rows),
        in_specs=[spec(0), spec(1), spec(2)],
        out_specs=pl.BlockSpec((None, rows, K), lambda s, r: (s, r, 0)),
        compiler_params=_cparams("arbitrary", "arbitrary"),
        name="stack3_bf16",
    )(a, b, c)


def _pad_lora(w1, w2):
    rank = w1.shape[1]
    return (jnp.pad(w1, ((0, 0), (0, LORA_PAD - rank))), jnp.pad(w2, ((0, LORA_PAD - rank), (0, 0))))


def _prepare_params(p):
    w_in_t = jnp.swapaxes(p['w_in_ab'][0], 0, 1)
    n_qkvo = 4 * MIX_A
    n_g = 2 * NH_A
    q = {}
    q['w_in'] = _compact_rows_bf16(w_in_t, MIX_A, 6, n_qkvo // MIX_A, n_g)
    q['w_gate'] = jnp.stack(_split_hi_lo(jnp.pad(w_in_t[n_qkvo:n_qkvo + n_g], ((0, LANES - n_g), (0, 0)))))
    q['b_gate'] = jnp.pad(p['b_if_ab'][0], (0, LANES - n_g)).reshape(1, LANES)
    q['w_out_ab'] = p['w_out_ab'][0].astype(bf16)
    q['w_up'] = p['w_up'].astype(bf16)
    q['w_down'] = p['w_down'].astype(bf16)
    q['w_rkv'] = _stack3_bf16(p['rw_wr'][0], p['rw_wk'][0], p['rw_wv'][0])
    mu = p['rw_mu'][0]
    q['mu_rkv'] = jnp.stack([mu[0], mu[2], mu[3]])[:, None, :]
    q['mu_lora'] = jnp.stack([mu[1], mu[4], mu[5]])[:, None, :]
    w1, w2w = _pad_lora(p['rw_w1'][0], p['rw_w2'][0])
    a1, w2a = _pad_lora(p['rw_a1'][0], p['rw_a2'][0])
    q['lora_w1'] = jnp.concatenate([w1, a1, p['rw_g1'][0]], axis=1).astype(bf16)
    q['lora_w2w'] = w2w.astype(bf16)
    q['lora_w2a'] = w2a.astype(bf16)
    q['lora_w2g'] = p['rw_g2'][0].astype(bf16)
    q['w_o'] = p['rw_wo'][0].astype(bf16)
    return q


def _trunk(tok, states, p, q, t_real, out_from=0):
    mC, mn, mm, rgh, rgc, rwS, rwx = states
    B, T, D = tok.batch, tok.t_pad, tok.x.shape[-1]
    M = B * T
    row = lambda a: a.reshape(1, -1)

    act_dtype = bf16 if T % (2 * SUBLANES) == 0 else f32
    zq, zr, gates = _inproj(tok, q['w_in'], q['w_gate'], q['b_gate'], act_dtype)
    zq = zq.reshape(B, T, -1)
    zr = zr.reshape(B, T, -1)
    hm, c_new, n_new, m_new = _mlstm(zq, zr, gates.reshape(B, T, LANES), mC[0], mn[0], mm[0],
                                     p['mlstm_norm_g'][0], t_real, act_dtype)
    yr, h_last = _rglru(zr, rgc[0], rgh[0], p['rg_conv_w'][0], p['rg_conv_b'][0], p['rg_wa'][0],
                        p['rg_ba'][0], p['rg_wx'][0], p['rg_bx'][0], p['rg_lambda'][0], t_real, act_dtype)
    assert t_real >= CONV_W - 1
    conv_new = zr[:, t_real - (CONV_W - 1):t_real, MIX_A:MIX_A + RG_W]
    h1 = _proj_ln([(hm.reshape(M, MIX_A), MIX_A, 0), (yr.reshape(M, RG_W), RG_W, 0)],
                  q['w_out_ab'], tok, row(p['ln1_g'][0]), row(p['ln1_b'][0]))
    h2 = _mlp_ln(h1, q['w_up'], q['w_down'], 0, row(p['ln2_g'][0]), row(p['ln2_b'][0]))

    h2_3 = h2.reshape(B, T, D)
    shift0 = rwx[0].astype(f32)[:, None]
    rkv = _rwkv_rkv(h2_3, shift0, q['mu_rkv'], q['w_rkv'])
    lw, ag, g = _rwkv_lora(h2_3, shift0, q['mu_lora'], q['lora_w1'], q['lora_w2w'], q['lora_w2a'],
                           q['lora_w2g'], row(p['rw_w0'][0]), row(p['rw_a0'][0]))
    r3 = lambda a: a.reshape(B, T, D)
    xo, s_new = _rwkv_scan(rkv.reshape(3, B, T, D), r3(lw), r3(ag), r3(g), rwS[0], p['rw_kk'][0],
                           p['rw_ka'][0], p['rw_rk'][0], p['rw_lnx_g'][0], p['rw_lnx_b'][0], t_real,
                           act_dtype)
    xo = xo.reshape(M, D)
    h3 = _proj_ln([(xo, D, 0)], q['w_o'], h2, row(p['ln1_g'][1]), row(p['ln1_b'][1]))
    n_keep = t_real - out_from
    h4 = _mlp_ln(h3, q['w_up'], q['w_down'], 1, row(p['ln2_g'][1]), row(p['ln2_b'][1]),
                 window=None if (out_from == 0 and t_real == T) else (T, out_from, n_keep))
    shift_new = h2_3[:, t_real - 1]

    new_states = (c_new[None], n_new[None], m_new[None], h_last[None], conv_new[None],
                  s_new[None], shift_new[None])
    return h4.reshape(B, n_keep, D), new_states


def kernel(x_prompt, x_sample, state_mlstm_C, state_mlstm_n, state_mlstm_m, state_rglru_h, state_rglru_conv, state_rwkv_S, state_rwkv_shift, meta_tokens, w_in_ab, b_if_ab, mlstm_norm_g, rg_conv_w, rg_conv_b, rg_wa, rg_ba, rg_wx, rg_bx, rg_lambda, w_out_ab, rw_mu, rw_wr, rw_wk, rw_wv, rw_wo, rw_w0, rw_w1, rw_w2, rw_a0, rw_a1, rw_a2, rw_g1, rw_g2, rw_kk, rw_ka, rw_rk, rw_lnx_g, rw_lnx_b, ln1_g, ln1_b, ln2_g, ln2_b, w_up, w_down):
    p = dict(w_in_ab=w_in_ab, b_if_ab=b_if_ab, mlstm_norm_g=mlstm_norm_g, rg_conv_w=rg_conv_w,
             rg_conv_b=rg_conv_b, rg_wa=rg_wa, rg_ba=rg_ba, rg_wx=rg_wx, rg_bx=rg_bx,
             rg_lambda=rg_lambda, w_out_ab=w_out_ab, rw_mu=rw_mu, rw_wr=rw_wr, rw_wk=rw_wk,
             rw_wv=rw_wv, rw_wo=rw_wo, rw_w0=rw_w0, rw_w1=rw_w1, rw_w2=rw_w2, rw_a0=rw_a0,
             rw_a1=rw_a1, rw_a2=rw_a2, rw_g1=rw_g1, rw_g2=rw_g2, rw_kk=rw_kk, rw_ka=rw_ka,
             rw_rk=rw_rk, rw_lnx_g=rw_lnx_g, rw_lnx_b=rw_lnx_b, ln1_g=ln1_g, ln1_b=ln1_b,
             ln2_g=ln2_g, ln2_b=ln2_b, w_up=w_up, w_down=w_down)
    q = _prepare_params(p)
    B, S, D = x_prompt.shape
    dt = x_prompt.dtype
    t_prompt = N_META + S
    t_pad = -(-t_prompt // MLSTM_CHUNK) * MLSTM_CHUNK
    zero_states = (
        jnp.zeros((1, B, NH_A, HD_A, HD_A), f32),
        jnp.zeros((1, B, NH_A, HD_A), f32),
        jnp.zeros((1, B, NH_A), f32),
        jnp.zeros((1, B, RG_W), f32),
        jnp.zeros((1, B, CONV_W - 1, RG_W), dt),
        jnp.zeros((1, B, NH_C, HD_C, HD_C), f32),
        jnp.zeros((1, B, D), dt),
    )
    hp, ps = _trunk(_Tokens(x_prompt, meta_tokens.astype(dt), t_pad), zero_states, p, q, t_prompt,
                    out_from=N_META)
    sample_states = (state_mlstm_C, state_mlstm_n, state_mlstm_m, state_rglru_h, state_rglru_conv,
                     state_rwkv_S, state_rwkv_shift)
    hs, ss = _trunk(_Tokens(x_sample, None, x_sample.shape[1]), sample_states, p, q, x_sample.shape[1])
    return (hp, hs, *ps, *ss)
```

```python
import functools
from typing import NamedTuple, Optional

import jax
import jax.numpy as jnp
from jax import lax
from jax.experimental import pallas as pl
from jax.experimental.pallas import tpu as pltpu

f32 = jnp.float32
bf16 = jnp.bfloat16
HI = lax.Precision.HIGHEST

D_MODEL = 2048
N_META = 16
MIX_A = 1024
NH_A = 4
HD_A = 256
RG_W = 1024
RG_BLOCKS = 8
RG_BW = 128
CONV_W = 4
RG_C = 8.0
HD_C = 64
NH_C = 32
D_FF = 8192
LN_EPS = 1e-5
GN_EPS = 64e-5
DEPTH = 2
ALPHA = (2.0 * DEPTH) ** 0.25

LANES = 128
SUBLANES = 8
VMEM_LIMIT_BYTES = 56 * 1024 * 1024

MLSTM_CHUNK = 128
RWKV_TILE = 64
SHORT_SEQ_ROWS_CAP = 512
LORA_PAD = 128
NEG_BIG = -1e30

NT_DIMS = (((1,), (1,)), ((), ()))
TN_DIMS = (((0,), (0,)), ((), ()))


class _Tokens(NamedTuple):
    x: jax.Array
    lead: Optional[jax.Array]
    t_pad: int

    @property
    def batch(self):
        return self.x.shape[0]


def _cparams(*sem):
    return pltpu.CompilerParams(dimension_semantics=sem, vmem_limit_bytes=VMEM_LIMIT_BYTES)


def _pick_tile(n, cap):
    best = None
    for d in range(SUBLANES, min(n, cap) + 1, SUBLANES):
        if n % d == 0:
            best = d
    assert best is not None, (n, cap)
    return best


def _dot_bf16(a, b):
    return jnp.dot(a.astype(bf16), b.astype(bf16), preferred_element_type=f32)


def _dot_hi(a, b, dims=None):
    if dims is None:
        return jnp.dot(a, b, precision=HI, preferred_element_type=f32)
    return lax.dot_general(a, b, dims, precision=HI, preferred_element_type=f32)


def _split_hi_lo(x):
    hi = x.astype(bf16)
    return hi, (x - hi.astype(f32)).astype(bf16)


def _layer_norm_rows(t, g, b):
    mu = jnp.mean(t, axis=-1, keepdims=True)
    d = t - mu
    var = jnp.mean(d * d, axis=-1, keepdims=True)
    return d * lax.rsqrt(var + LN_EPS) * g + b


def _window_spec(tm, K, seq_rows, t_pad, n_lead, grid_rank, **kw):
    tps = t_pad // tm
    assert tps >= 2 and tps * tm == t_pad and t_pad - n_lead - seq_rows <= tm

    def index(g, *_):
        start = jnp.clip(tm * (g % tps) - n_lead, 0, seq_rows - tm)
        return (pl.multiple_of((g // tps) * seq_rows + start, SUBLANES), 0)

    assert grid_rank in (1, 2)
    return pl.BlockSpec((pl.Element(tm), pl.Element(K)), index, **kw)


def _virtual_rows(blk, lead_ref, i_seq, tps, n_tail):
    tm = blk.shape[0]
    n_lead = lead_ref.shape[0]
    first = jnp.concatenate([lead_ref[...], blk[:tm - n_lead]], axis=0)
    last = jnp.concatenate([blk[n_tail:], jnp.zeros((n_tail, blk.shape[1]), blk.dtype)], axis=0)
    return jnp.where(i_seq == 0, first, jnp.where(i_seq == tps - 1, last, blk))


def _inproj_kernel(x_ref, w_ref, wg_ref, bg_ref, *rest, virtual):
    if virtual is None:
        o1_ref, o2_ref, g_ref, xb_ref = rest
    else:
        lead_ref, o1_ref, o2_ref, g_ref, xb_ref = rest

    @pl.when(pl.program_id(1) == 0)
    def _():
        x = x_ref[...]
        if virtual is not None:
            tps, n_tail = virtual
            x = _virtual_rows(x, lead_ref, pl.program_id(0) % tps, tps, n_tail)
        x_hi, x_lo = _split_hi_lo(x)
        xb_ref[...] = x_hi
        nt = lambda a, b: lax.dot_general(a, b, NT_DIMS, preferred_element_type=f32)
        pre = (nt(x_hi, wg_ref[0]) + (nt(x_hi, wg_ref[1]) + nt(x_lo, wg_ref[0]))) + bg_ref[...]
        lane = lax.broadcasted_iota(jnp.int32, pre.shape, 1)
        g_ref[...] = jnp.where(lane < NH_A, pre,
                               jnp.where(lane < 2 * NH_A, jax.nn.log_sigmoid(pre), 0.0))

    y = lax.dot_general(xb_ref[...], w_ref[...], NT_DIMS, preferred_element_type=f32)
    j = pl.program_id(1)
    n_first = pl.num_programs(1) // 2

    @pl.when(j < n_first)
    def _():
        o1_ref[...] = y.astype(o1_ref.dtype)

    @pl.when(j >= n_first)
    def _():
        o2_ref[...] = y


def _inproj(tok, w, w_g, b_g, qkv_dtype, tm_cap=1088, tn=1024):
    x = tok.x.reshape(-1, tok.x.shape[-1])
    K = x.shape[1]
    M = tok.batch * tok.t_pad
    n_out = w.shape[0]
    n_half = n_out // 2
    nt = n_half // tn
    if tok.lead is None:
        tm = _pick_tile(M, tm_cap)
        x_specs, x_args, virtual = [pl.BlockSpec((tm, K), lambda i, j: (i, 0))], [x], None
    else:
        tm = _pick_tile(tok.t_pad, tm_cap)
        n_lead = tok.lead.shape[0]
        x_specs = [_window_spec(tm, K, tok.x.shape[1], tok.t_pad, n_lead, 2)]
        x_args = [x]
        virtual = (tok.t_pad // tm, tok.t_pad - n_lead - tok.x.shape[1])
    lead_specs = [] if tok.lead is None else [pl.BlockSpec(tok.lead.shape, lambda i, j: (0, 0))]
    lead_args = [] if tok.lead is None else [tok.lead]
    return pl.pallas_call(
        functools.partial(_inproj_kernel, virtual=virtual),
        out_shape=[jax.ShapeDtypeStruct((M, n_half), qkv_dtype), jax.ShapeDtypeStruct((M, n_half), f32),
                   jax.ShapeDtypeStruct((M, LANES), f32)],
        grid=(M // tm, n_out // tn),
        in_specs=x_specs + [pl.BlockSpec((tn, K), lambda i, j: (j, 0)),
                            pl.BlockSpec((2, LANES, K), lambda i, j: (0, 0, 0)),
                            pl.BlockSpec((1, LANES), lambda i, j: (0, 0))] + lead_specs,
        out_specs=[pl.BlockSpec((tm, tn), lambda i, j: (i, jnp.minimum(j, nt - 1))),
                   pl.BlockSpec((tm, tn), lambda i, j: (i, jnp.maximum(j - nt, 0))),
                   pl.BlockSpec((tm, LANES), lambda i, j: (i, 0))],
        scratch_shapes=[pltpu.VMEM((tm, K), bf16)],
        compiler_params=_cparams("parallel", "arbitrary"),
        name="inproj",
    )(*x_args, w, w_g, b_g, *lead_args)


def _proj_ln_kernel(*refs, n_x, virtual):
    x_refs = refs[:n_x]
    if virtual is None:
        w_ref, res_ref, g_ref, b_ref, o_ref = refs[n_x:]
        res = res_ref[...]
    else:
        w_ref, res_ref, g_ref, b_ref, lead_ref, o_ref = refs[n_x:]
        tps, n_tail = virtual
        res = _virtual_rows(res_ref[...], lead_ref, pl.program_id(0) % tps, tps, n_tail)
    y = None
    k0 = 0
    for x_ref in x_refs:
        kc = x_ref.shape[1]
        part = _dot_bf16(x_ref[...], w_ref[k0:k0 + kc, :])
        y = part if y is None else y + part
        k0 += kc
    t = ALPHA * res + y
    o_ref[...] = _layer_norm_rows(t, g_ref[...], b_ref[...])


def _proj_ln(xs, w, res, g, b, tm_cap=272):
    if isinstance(res, _Tokens) and res.lead is not None:
        tok = res
        N = tok.x.shape[-1]
        M = tok.batch * tok.t_pad
        tm = _pick_tile(tok.t_pad, tm_cap)
        n_lead = tok.lead.shape[0]
        res = tok.x.reshape(-1, N)
        res_spec = _window_spec(tm, N, tok.x.shape[1], tok.t_pad, n_lead, 1)
        virtual = (tok.t_pad // tm, tok.t_pad - n_lead - tok.x.shape[1])
        lead_specs, lead_args = [pl.BlockSpec(tok.lead.shape, lambda i: (0, 0))], [tok.lead]
    else:
        if isinstance(res, _Tokens):
            res = res.x.reshape(-1, res.x.shape[-1])
        M, N = res.shape
        tm = _pick_tile(M, tm_cap)
        res_spec = pl.BlockSpec((tm, N), lambda i: (i, 0))
        virtual, lead_specs, lead_args = None, [], []
    in_specs = []
    args = []
    for arr, width, blk in xs:
        in_specs.append(pl.BlockSpec((tm, width), functools.partial(lambda i, blk: (i, blk), blk=blk)))
        args.append(arr)
    K = w.shape[0]
    in_specs += [pl.BlockSpec((K, N), lambda i: (0, 0)),
                 res_spec,
                 pl.BlockSpec((1, N), lambda i: (0, 0)),
                 pl.BlockSpec((1, N), lambda i: (0, 0))] + lead_specs
    return pl.pallas_call(
        functools.partial(_proj_ln_kernel, n_x=len(xs), virtual=virtual),
        out_shape=jax.ShapeDtypeStruct((M, N), f32),
        grid=(M // tm,),
        in_specs=in_specs,
        out_specs=pl.BlockSpec((tm, N), lambda i: (i, 0)),
        compiler_params=_cparams("parallel"),
        name="proj_ln",
    )(*args, w, res, g, b, *lead_args)


def _mlp_ln_kernel(x_ref, wu_ref, wd_ref, g_ref, b_ref, o_ref, xb_ref):
    f = pl.program_id(1)

    @pl.when(f == 0)
    def _():
        xb_ref[...] = x_ref[...].astype(bf16)
        o_ref[...] = jnp.zeros_like(o_ref)

    a = jnp.dot(xb_ref[...], wu_ref[...], preferred_element_type=f32)
    a = jnp.square(jnp.maximum(a, 0.0))
    o_ref[...] += jnp.dot(a.astype(bf16), wd_ref[...], preferred_element_type=f32)

    @pl.when(f == pl.num_programs(1) - 1)
    def _():
        t = ALPHA * x_ref[...] + o_ref[...]
        o_ref[...] = _layer_norm_rows(t, g_ref[...], b_ref[...])


def _mlp_ln(x, w_up, w_down, layer, g, b, tm_cap=1088, tf=1024, window=None):
    M, K = x.shape
    F = w_up.shape[2]
    if window is None:
        tm = _pick_tile(M, tm_cap)
        x_spec = pl.BlockSpec((tm, K), lambda i, j: (i, 0), pipeline_mode=pl.Buffered(1))
    else:
        seq_rows, first, count = window
        assert first % SUBLANES == 0 and seq_rows % SUBLANES == 0
        tm = _pick_tile(count, tm_cap)
        per_seq = count // tm
        M = (M // seq_rows) * count
        x_spec = pl.BlockSpec(
            (pl.Element(tm), pl.Element(K)),
            lambda i, j: (pl.multiple_of((i // per_seq) * seq_rows + first + (i % per_seq) * tm, SUBLANES), 0),
            pipeline_mode=pl.Buffered(1))
    return pl.pallas_call(
        _mlp_ln_kernel,
        out_shape=jax.ShapeDtypeStruct((M, K), f32),
        grid=(M // tm, F // tf),
        in_specs=[x_spec,
                  pl.BlockSpec((None, K, tf), lambda i, j: (layer, 0, j)),
                  pl.BlockSpec((None, tf, K), lambda i, j: (layer, j, 0)),
                  pl.BlockSpec((1, K), lambda i, j: (0, 0)),
                  pl.BlockSpec((1, K), lambda i, j: (0, 0))],
        out_specs=pl.BlockSpec((tm, K), lambda i, j: (i, 0), pipeline_mode=pl.Buffered(1)),
        scratch_shapes=[pltpu.VMEM((tm, K), bf16)],
        compiler_params=_cparams("parallel", "arbitrary"),
        name="mlp_ln",
    )(x, w_up, w_down, g, b)


def _mlstm_kernel(q_ref, k_ref, v_ref, o_ref, g_ref, c0_ref, n0_ref, m0_ref, ng_ref,
                  h_ref, c_ref, n_ref, m_ref, *scratch, t_real, carry):
    nb, lb = q_ref.shape[0], q_ref.shape[1]
    R = nb * lb
    c = pl.program_id(1)
    if carry:
        cs, ns, ms = scratch

        @pl.when(c == 0)
        def _():
            cs[...] = c0_ref[...]
            ns[...] = n0_ref[...]
            ms[...] = m0_ref[...]
    else:
        cs, ns, ms = c0_ref, n0_ref, m0_ref

    heads = range(NH_A)
    seqs = range(nb)
    hs = lambda h: slice(h * HD_A, (h + 1) * HD_A)
    rows = lambda j: slice(j * lb, (j + 1) * lb)
    t_col = c * lb + lax.broadcasted_iota(jnp.int32, (nb, lb, 1), 1).reshape(R, 1)
    vrow = t_col < t_real
    rowi = lax.broadcasted_iota(jnp.int32, (R, R), 0)
    coli = lax.broadcasted_iota(jnp.int32, (R, R), 1)
    if nb == 1:
        causal = rowi >= coli
    else:
        shift = lb.bit_length() - 1
        causal = jnp.logical_and((rowi >> shift) == (coli >> shift), rowi >= coli)

    def per_seq(x):
        return jnp.broadcast_to(x, (nb, lb, x.shape[-1])).reshape(R, x.shape[-1])

    def last_row(x):
        return x.reshape(nb, lb, x.shape[-1])[:, lb - 1:lb, :]

    ld = lambda ref, h: jnp.where(vrow, ref[:, :, hs(h)].astype(f32).reshape(R, HD_A), 0.0)
    q = [ld(q_ref, h) for h in heads]
    k = [ld(k_ref, h) * (HD_A ** -0.5) for h in heads]
    vb = [ld(v_ref, h).astype(bf16) for h in heads]
    qb = [x.astype(bf16) for x in q]
    qk = [lax.dot_general(qb[h], k[h].astype(bf16), NT_DIMS, preferred_element_type=f32) for h in heads]

    gate = jnp.where(vrow, g_ref[...].reshape(R, LANES), 0.0)
    fcum = _dot_hi(causal.astype(f32), gate)
    gate_t = gate.T
    fcum_t = fcum.T
    vcol = (c * lb + (lax.broadcasted_iota(jnp.int32, (1, R), 1) & (lb - 1))) < t_real

    m_prev, fc, igc, fl, m_row, inter, s = [], [], [], [], [], [], []
    for h in heads:
        fc.append(fcum[:, NH_A + h:NH_A + h + 1])
        fr = fcum_t[NH_A + h:NH_A + h + 1, :]
        igc.append(jnp.where(vrow, gate[:, h:h + 1], NEG_BIG))
        igr = jnp.where(vcol, gate_t[h:h + 1, :], NEG_BIG)
        m_prev.append(ms[:, h, :, 0:1])
        logd = jnp.where(causal, fc[h] - fr + igr, NEG_BIG)
        b_in = fc[h] + per_seq(m_prev[h])
        m_row.append(jnp.maximum(b_in, jnp.max(logd, axis=1, keepdims=True)))
        inter.append(jnp.exp(b_in - m_row[h]))
        s.append(qk[h] * jnp.exp(logd - m_row[h]))
        fl.append(last_row(fc[h]))

    sv = [jnp.dot(s[h].astype(bf16), vb[h], preferred_element_type=f32) for h in heads]
    qc = [jnp.concatenate([jnp.dot(qb[h][rows(j)], cs[j, h].astype(bf16), preferred_element_type=f32)
                           for j in seqs], axis=0) for h in heads]
    for h in heads:
        num = sv[h] + inter[h] * qc[h]
        den = jnp.sum(s[h], axis=1, keepdims=True) \
            + inter[h] * jnp.sum(q[h] * per_seq(ns[:, h]), axis=1, keepdims=True)
        hh = num / jnp.maximum(jnp.abs(den), jnp.exp(-m_row[h]))
        mu = jnp.mean(hh, axis=1, keepdims=True)
        dv = hh - mu
        var = jnp.mean(dv * dv, axis=1, keepdims=True)
        hn = dv * lax.rsqrt(var + LN_EPS) * ng_ref[h]
        out = hn * jax.nn.sigmoid(ld(o_ref, h))
        h_ref[:, :, hs(h)] = out.reshape(nb, lb, HD_A).astype(h_ref.dtype)

    kw, m_new, dec = [], [], []
    for h in heads:
        wj = per_seq(fl[h]) - fc[h] + igc[h]
        wmax = jnp.max(wj.reshape(nb, lb, 1), axis=1, keepdims=True)
        m_new.append(jnp.maximum(fl[h] + m_prev[h], wmax))
        dec.append(jnp.exp(fl[h] + m_prev[h] - m_new[h]))
        kw.append(k[h] * jnp.exp(wj - per_seq(m_new[h])))
    kv = [[lax.dot_general(kw[h][rows(j)].astype(bf16), vb[h][rows(j)], TN_DIMS,
                           preferred_element_type=f32) for j in seqs] for h in heads]
    c_dst, n_dst, m_dst = (cs, ns, ms) if carry else (c_ref, n_ref, m_ref)
    for h in heads:
        n_new = dec[h] * ns[:, h] + jnp.sum(kw[h].reshape(nb, lb, HD_A), axis=1, keepdims=True)
        for j in seqs:
            c_dst[j, h] = dec[h][j] * cs[j, h] + kv[h][j]
        n_dst[:, h] = n_new
        m_dst[:, h] = jnp.broadcast_to(m_new[h], (nb, 1, LANES))

    if carry:
        @pl.when(c == pl.num_programs(1) - 1)
        def _():
            c_ref[...] = cs[...]
            n_ref[...] = ns[...]
            m_ref[...] = ms[...]


def _mlstm(zq, zr, gates, c0, n0, m0, norm_g, t_real, out_dtype, nb_short=8):
    B, T, _ = zq.shape
    if T <= SUBLANES:
        nb, lb = nb_short, T
    else:
        nb, lb = 1, MLSTM_CHUNK
    nc = T // lb
    carry = nc > 1
    assert B % nb == 0 and T % lb == 0 and (carry or t_real == T)
    m0b = jnp.broadcast_to(m0[:, :, None, None], (B, NH_A, 1, LANES))
    zspec = lambda group: pl.BlockSpec((nb, lb, MIX_A), lambda b, c: (b, c, group))
    st_spec = lambda r, w: pl.BlockSpec((nb, NH_A, r, w), lambda b, c: (b, 0, 0, 0))
    st_shapes = [(HD_A, HD_A), (1, HD_A), (1, LANES)]
    outs = pl.pallas_call(
        functools.partial(_mlstm_kernel, t_real=t_real, carry=carry),
        out_shape=[jax.ShapeDtypeStruct((B, T, MIX_A), out_dtype)]
        + [jax.ShapeDtypeStruct((B, NH_A) + s, f32) for s in st_shapes],
        grid=(B // nb, nc),
        in_specs=[zspec(0), zspec(1), zspec(2), zspec(0),
                  pl.BlockSpec((nb, lb, LANES), lambda b, c: (b, c, 0))]
        + [st_spec(*s) for s in st_shapes]
        + [pl.BlockSpec((NH_A, 1, HD_A), lambda b, c: (0, 0, 0))],
        out_specs=[pl.BlockSpec((nb, lb, MIX_A), lambda b, c: (b, c, 0))]
        + [st_spec(*s) for s in st_shapes],
        scratch_shapes=[pltpu.VMEM((nb, NH_A) + s, f32) for s in st_shapes] if carry else [],
        compiler_params=_cparams("parallel", "arbitrary"),
        name="mlstm",
    )(zq, zq, zq, zr, gates, c0, n0.reshape(B, NH_A, 1, HD_A), m0b, norm_g.reshape(NH_A, 1, HD_A))
    hm, c_new, n_new, m_new = outs
    return hm, c_new, n_new.reshape(B, NH_A, HD_A), m_new[:, :, 0, 0]


def _one_minus_exp(x):
    t = jnp.tanh(0.5 * x)
    return -2.0 * t / (1.0 - t)


def _rglru_kernel(xr_ref, gr_ref, conv0_ref, h0_ref, cw_ref, cb_ref, wa_ref, ba_ref, wx_ref,
                  bx_ref, lam_ref, y_ref, hl_ref, xp, a_s, h_s, hc, *, t_real):
    nb, lr = xr_ref.shape[0], xr_ref.shape[1]
    rows = nb * lr
    ngroup = rows // SUBLANES
    c = pl.program_id(1)
    hist = SUBLANES - (CONV_W - 1)

    @pl.when(c == 0)
    def _():
        xp[:, 0:SUBLANES, :] = jnp.zeros((nb, SUBLANES, RG_W), f32)
        xp[:, hist:SUBLANES, :] = conv0_ref[...]
        hc[...] = h0_ref[...]

    xp[:, SUBLANES:2 * SUBLANES, :] = xr_ref[:, 0:SUBLANES, :]
    xc = cb_ref[...]
    for j in range(CONV_W):
        tap = xp[:, hist + j:hist + j + SUBLANES, :]
        if lr > SUBLANES:
            tap = jnp.concatenate([tap, xr_ref[:, hist + j:hist + j + lr - SUBLANES, :]], axis=1)
        xc = xc + tap * cw_ref[j:j + 1, :]
    xp[:, hist:SUBLANES, :] = xr_ref[:, lr - (CONV_W - 1):lr, :]
    xc = xc.reshape(rows, RG_W)

    xcb = xc.astype(bf16)
    ra = []
    rx = []
    for n in range(RG_BLOCKS):
        xn = xcb[:, n * RG_BW:(n + 1) * RG_BW]
        ra.append(jnp.dot(xn, wa_ref[n].astype(bf16), preferred_element_type=f32))
        rx.append(jnp.dot(xn, wx_ref[n].astype(bf16), preferred_element_type=f32))
    r = jax.nn.sigmoid(jnp.concatenate(ra, axis=1) + ba_ref[...])
    gi = jax.nn.sigmoid(jnp.concatenate(rx, axis=1) + bx_ref[...])
    log_a = -RG_C * r * jax.nn.softplus(-lam_ref[...])
    t_idx = c * lr + lax.broadcasted_iota(jnp.int32, (nb, lr, 1), 1).reshape(rows, 1)
    valid = t_idx < t_real
    a = jnp.where(valid, jnp.exp(log_a), 1.0)
    u = jnp.where(valid, jnp.sqrt(_one_minus_exp(2.0 * log_a)) * (gi * xc), 0.0)

    a = a.reshape(ngroup, SUBLANES, RG_W)
    u = u.reshape(ngroup, SUBLANES, RG_W)
    r8 = lax.broadcasted_iota(jnp.int32, (1, SUBLANES, 1), 1)
    d = 1
    while d < SUBLANES:
        inside = r8 >= d
        a_sh = jnp.where(inside, pltpu.roll(a, d, axis=1), 1.0)
        u_sh = jnp.where(inside, pltpu.roll(u, d, axis=1), 0.0)
        u = a * u_sh + u
        a = a * a_sh
        d *= 2

    if lr == SUBLANES:
        h = u + a * hc[...]
        h_last = h[:, SUBLANES - 1:SUBLANES, :]
        h = h.reshape(rows, RG_W)
    else:
        assert nb == 1
        a_s[...] = a.reshape(rows, RG_W)
        h_s[...] = u.reshape(rows, RG_W)

        def step(g, h_prev):
            i = pl.multiple_of(g * SUBLANES, SUBLANES)
            hg = h_s[pl.ds(i, SUBLANES), :] + a_s[pl.ds(i, SUBLANES), :] * h_prev
            h_s[pl.ds(i, SUBLANES), :] = hg
            return hg[SUBLANES - 1:SUBLANES, :]

        h_last = lax.fori_loop(0, ngroup, step, hc[0]).reshape(1, 1, RG_W)
        h = h_s[...]
    hc[...] = h_last
    y_ref[...] = (h * jax.nn.gelu(gr_ref[...].reshape(rows, RG_W))).reshape(nb, lr, RG_W).astype(y_ref.dtype)

    @pl.when(c == pl.num_programs(1) - 1)
    def _():
        hl_ref[...] = h_last


def _rglru(zr, conv0, h0, conv_w, conv_b, wa, ba, wx, bx, lam, t_real, out_dtype, lr_cap=544, nb_short=16):
    B, T, _ = zr.shape
    if T == SUBLANES:
        nb, lr = nb_short, T
    else:
        nb, lr = 1, _pick_tile(T, lr_cap)
    assert B % nb == 0
    row = lambda a: a.reshape(1, RG_W)
    full = lambda shp: pl.BlockSpec(shp, lambda b, c: (0,) * len(shp))
    y, h_last = pl.pallas_call(
        functools.partial(_rglru_kernel, t_real=t_real),
        out_shape=[jax.ShapeDtypeStruct((B, T, RG_W), out_dtype),
                   jax.ShapeDtypeStruct((B, 1, RG_W), f32)],
        grid=(B // nb, T // lr),
        in_specs=[pl.BlockSpec((nb, lr, RG_W), lambda b, c: (b, c, 1)),
                  pl.BlockSpec((nb, lr, RG_W), lambda b, c: (b, c, 2)),
                  pl.BlockSpec((nb, CONV_W - 1, RG_W), lambda b, c: (b, 0, 0)),
                  pl.BlockSpec((nb, 1, RG_W), lambda b, c: (b, 0, 0)),
                  full((CONV_W, RG_W)), full((1, RG_W)),
                  full((RG_BLOCKS, RG_BW, RG_BW)), full((1, RG_W)),
                  full((RG_BLOCKS, RG_BW, RG_BW)), full((1, RG_W)), full((1, RG_W))],
        out_specs=[pl.BlockSpec((nb, lr, RG_W), lambda b, c: (b, c, 0)),
                   pl.BlockSpec((nb, 1, RG_W), lambda b, c: (b, 0, 0))],
        scratch_shapes=[pltpu.VMEM((nb, 2 * SUBLANES, RG_W), f32), pltpu.VMEM((nb * lr, RG_W), f32),
                        pltpu.VMEM((nb * lr, RG_W), f32), pltpu.VMEM((nb, 1, RG_W), f32)],
        compiler_params=_cparams("parallel", "arbitrary"),
        name="rglru",
    )(zr, zr, conv0, h0.reshape(B, 1, RG_W), conv_w, row(conv_b), wa, row(ba), wx, row(bx), row(lam))
    return y, h_last.reshape(B, RG_W)


def _token_shift(x_ref, st_ref, xp_ref, c):
    tt = x_ref.shape[1]
    last = SUBLANES - 1

    @pl.when(c == 0)
    def _():
        xp_ref[:, last:SUBLANES, :] = st_ref[...]

    xp_ref[:, SUBLANES:2 * SUBLANES, :] = x_ref[:, 0:SUBLANES, :]
    xprev = xp_ref[:, last:last + SUBLANES, :]
    if tt > SUBLANES:
        xprev = jnp.concatenate([xprev, x_ref[:, last:tt - 1, :]], axis=1)
    xp_ref[:, last:SUBLANES, :] = x_ref[:, tt - 1:tt, :]
    return x_ref[...], xprev


def _seq_tile(B, T, rows_cap):
    if T <= SUBLANES:
        return _pick_tile(B * T, min(rows_cap, SHORT_SEQ_ROWS_CAP)) // T, T
    return 1, _pick_tile(T, rows_cap)


def _rwkv_rkv_kernel(x_ref, st_ref, mu_ref, w_ref, o_ref, xs_ref, xp_ref):
    c = pl.program_id(1)
    s = pl.program_id(2)

    @pl.when(jnp.logical_and(s == 0, pl.program_id(3) == 0))
    def _():
        x, xprev = _token_shift(x_ref, st_ref, xp_ref, c)
        dx = xprev - x
        for si in range(3):
            xs_ref[si] = (x + dx * mu_ref[si]).reshape(xs_ref.shape[1:]).astype(bf16)

    o_ref[...] = jnp.dot(xs_ref[s], w_ref[...], preferred_element_type=f32)


def _rwkv_rkv(x, shift0, mu3, w3, rows_cap=1088):
    B, T, K = x.shape
    nb, tt = _seq_tile(B, T, rows_cap)
    rows = nb * tt
    nc = T // tt
    tn = K if rows <= SHORT_SEQ_ROWS_CAP else K // 2
    return pl.pallas_call(
        _rwkv_rkv_kernel,
        out_shape=jax.ShapeDtypeStruct((3, B * T, K), f32),
        grid=(B // nb, nc, 3, K // tn),
        in_specs=[pl.BlockSpec((nb, tt, K), lambda b, c, s, j: (b, c, 0), pipeline_mode=pl.Buffered(1)),
                  pl.BlockSpec((nb, 1, K), lambda b, c, s, j: (b, 0, 0)),
                  pl.BlockSpec((3, 1, K), lambda b, c, s, j: (0, 0, 0)),
                  pl.BlockSpec((None, K, tn), lambda b, c, s, j: (s, 0, j))],
        out_specs=pl.BlockSpec((None, rows, tn), lambda b, c, s, j: (s, b * nc + c, j)),
        scratch_shapes=[pltpu.VMEM((3, rows, K), bf16), pltpu.VMEM((nb, 2 * SUBLANES, K), f32)],
        compiler_params=_cparams("parallel", "arbitrary", "arbitrary", "arbitrary"),
        name="rwkv_rkv",
    )(x, shift0, mu3, w3)


def _rwkv_lora_kernel(x_ref, st_ref, mu_ref, w1_ref, w2w_ref, w2a_ref, w2g_ref, w0_ref, a0_ref,
                      lw_ref, a_ref, g_ref, xp_ref):
    x, xprev = _token_shift(x_ref, st_ref, xp_ref, pl.program_id(1))
    dx = xprev - x
    mix = lambda i: (x + dx * mu_ref[i]).reshape(lw_ref.shape).astype(bf16)
    r = LORA_PAD
    hw = jnp.tanh(jnp.dot(mix(0), w1_ref[:, 0:r], preferred_element_type=f32))
    ha = jnp.dot(mix(1), w1_ref[:, r:2 * r], preferred_element_type=f32)
    hg = jax.nn.sigmoid(jnp.dot(mix(2), w1_ref[:, 2 * r:], preferred_element_type=f32))
    w = w0_ref[...] + jnp.dot(hw.astype(bf16), w2w_ref[...], preferred_element_type=f32)
    w = -jax.nn.softplus(-w) - 0.5
    lw_ref[...] = -jnp.exp(w)
    a = a0_ref[...] + jnp.dot(ha.astype(bf16), w2a_ref[...], preferred_element_type=f32)
    a_ref[...] = jax.nn.sigmoid(a)
    g_ref[...] = jnp.dot(hg.astype(bf16), w2g_ref[...], preferred_element_type=f32)


def _rwkv_lora(x, shift0, mu3, w1, w2w, w2a, w2g, w0, a0, rows_cap=272):
    B, T, K = x.shape
    nb, tt = _seq_tile(B, T, rows_cap)
    rows = nb * tt
    nc = T // tt
    full = lambda a: pl.BlockSpec(a.shape, lambda b, c: (0,) * a.ndim)
    out = pl.BlockSpec((rows, K), lambda b, c: (b * nc + c, 0))
    return pl.pallas_call(
        _rwkv_lora_kernel,
        out_shape=[jax.ShapeDtypeStruct((B * T, K), f32)] * 3,
        grid=(B // nb, nc),
        in_specs=[pl.BlockSpec((nb, tt, K), lambda b, c: (b, c, 0)),
                  pl.BlockSpec((nb, 1, K), lambda b, c: (b, 0, 0)),
                  full(mu3), full(w1), full(w2w), full(w2a), full(w2g), full(w0), full(a0)],
        out_specs=[out, out, out],
        scratch_shapes=[pltpu.VMEM((nb, 2 * SUBLANES, K), f32)],
        compiler_params=_cparams("parallel", "arbitrary"),
        name="rwkv_lora",
    )(x, shift0, mu3, w1, w2w, w2a, w2g, w0, a0)


def _head_sums(x, ones_bd, npair):
    rows = x.shape[0]
    xs = jnp.concatenate([x[:, g * LANES:(g + 1) * LANES] for g in range(npair)], axis=0)
    s = jnp.dot(xs.astype(bf16), ones_bd, preferred_element_type=f32)
    return jnp.concatenate([s[g * rows:(g + 1) * rows] for g in range(npair)], axis=1)


def _rwkv_scan_kernel(r_ref, k_ref, v_ref, lw_ref, ag_ref, g_ref, s0_ref, kk_ref, ka_ref, rk_ref,
                      lg_ref, lb_ref, y_ref, s_ref, sbd, *, t_real, seg, npair):
    R2 = 2 * RWKV_TILE
    nseg = RWKV_TILE // seg
    c = pl.program_id(2)

    @pl.when(c == 0)
    def _():
        sbd[...] = jnp.zeros(sbd.shape, f32)
        for gi in range(npair):
            for j in range(nseg):
                sbd[gi, j, 0:HD_C, 0:HD_C] = s0_ref[j, 2 * gi]
                sbd[gi, j, HD_C:R2, HD_C:R2] = s0_ref[j, 2 * gi + 1]

    tile_live = c * seg < t_real

    @pl.when(jnp.logical_not(tile_live))
    def _():
        y_ref[...] = jnp.zeros(y_ref.shape, y_ref.dtype)

    @pl.when(tile_live)
    def _():
        _rwkv_tile(c, r_ref, k_ref, v_ref, lw_ref, ag_ref, g_ref, kk_ref, ka_ref, rk_ref, lg_ref, lb_ref,
                   y_ref, sbd, t_real=t_real, seg=seg, npair=npair)

    @pl.when(c == pl.num_programs(2) - 1)
    def _():
        for gi in range(npair):
            for j in range(nseg):
                s_ref[j, 2 * gi] = sbd[gi, j, 0:HD_C, 0:HD_C]
                s_ref[j, 2 * gi + 1] = sbd[gi, j, HD_C:R2, HD_C:R2]


def _rwkv_tile(c, r_ref, k_ref, v_ref, lw_ref, ag_ref, g_ref, kk_ref, ka_ref, rk_ref, lg_ref, lb_ref,
               y_ref, sbd, *, t_real, seg, npair):
    R = RWKV_TILE
    R2 = 2 * R
    nseg = R // seg
    W = npair * LANES
    shift = seg.bit_length() - 1
    mm = lambda a, b: jnp.dot(a.astype(bf16), b.astype(bf16), preferred_element_type=f32)
    mm_nt = lambda a, b: lax.dot_general(a.astype(bf16), b.astype(bf16), NT_DIMS, preferred_element_type=f32)
    mm_tn = lambda a, b: lax.dot_general(a.astype(bf16), b.astype(bf16), TN_DIMS, preferred_element_type=f32)

    row1 = lax.broadcasted_iota(jnp.int32, (R, 1), 0)
    valid = c * seg + (row1 & (seg - 1)) < t_real
    ld = lambda ref: jnp.where(valid, ref[...].reshape(R, W), 0.0)
    r = ld(r_ref)
    k = ld(k_ref)
    v = ld(v_ref)
    lw = ld(lw_ref)
    ag = ld(ag_ref)

    lane = lax.broadcasted_iota(jnp.int32, (1, LANES), 1)
    m0 = (lane < HD_C).astype(f32)
    m1 = 1.0 - m0
    row2 = lax.broadcasted_iota(jnp.int32, (R2, R2), 0)
    col2 = lax.broadcasted_iota(jnp.int32, (R2, R2), 1)
    ones_bd = ((row2 < HD_C) == (col2 < HD_C)).astype(bf16)
    same_blk = (row2 >> shift) == (col2 >> shift)
    strict = jnp.logical_and(same_blk, row2 > col2)
    incl = jnp.logical_and(same_blk, row2 >= col2)
    eye = (row2 == col2).astype(f32)

    kk = k * kk_ref[...]
    kk = kk / jnp.maximum(jnp.sqrt(_head_sums(kk * kk, ones_bd, npair)), 1e-12)
    kp = k * (1.0 + (ag - 1.0) * ka_ref[...])
    bv = kk * ag

    rowR = lax.broadcasted_iota(jnp.int32, (R, R), 0)
    colR = lax.broadcasted_iota(jnp.int32, (R, R), 1)
    tril = jnp.logical_and((rowR >> shift) == (colR >> shift), rowR >= colR).astype(bf16)
    lw_hi, lw_lo = _split_hi_lo(lw)
    cw = (jnp.dot(tril, lw_hi, preferred_element_type=f32)
          + jnp.dot(tril, lw_lo, preferred_element_type=f32))
    cwl = jnp.concatenate(
        [jnp.broadcast_to(cw[j * seg + seg - 1:(j + 1) * seg, :], (seg, W)) for j in range(nseg)], axis=0)
    p_in = jnp.exp(cw)
    p_inv = jnp.exp(-cw)
    p_tail = jnp.exp(cwl - cw)
    at = -kk * jnp.exp(cw - lw)
    rt = r * p_in
    kt = kp * p_inv
    bt = bv * p_inv
    kh = kp * p_tail
    bh = bv * p_tail
    p_last = jnp.exp(cwl)

    def stack(x, gi):
        xg = x[:, gi * LANES:(gi + 1) * LANES]
        return jnp.concatenate([xg * m0, xg * m1], axis=0)

    def pick(x_st, j):
        if nseg == 1:
            return x_st
        return jnp.concatenate([x_st[j * seg:(j + 1) * seg], x_st[R + j * seg:R + (j + 1) * seg]], axis=0)

    def unpick(parts):
        if nseg == 1:
            return parts[0]
        return jnp.concatenate([q[0:seg] for q in parts] + [q[seg:2 * seg] for q in parts], axis=0)

    n_sq = shift - 1
    pairs = range(npair)
    s_prev = [[sbd[gi, j] for j in range(nseg)] for gi in pairs]
    a_st = [stack(at, gi) for gi in pairs]
    r_st = [stack(rt, gi) for gi in pairs]
    v_st = [stack(v, gi) for gi in pairs]
    prod = [mm_nt(jnp.concatenate([a_st[gi], r_st[gi]], axis=0),
                  jnp.concatenate([stack(kt, gi), stack(bt, gi)], axis=0)) for gi in pairs]
    m_ak = [jnp.where(strict, prod[gi][0:R2, 0:R2], 0.0) for gi in pairs]
    a_rk = [jnp.where(incl, prod[gi][R2:2 * R2, 0:R2], 0.0) for gi in pairs]
    a_rb = [jnp.where(incl, prod[gi][R2:2 * R2, R2:2 * R2], 0.0) for gi in pairs]

    x = [jnp.where(strict, prod[gi][0:R2, R2:2 * R2], 0.0) for gi in pairs]
    tinv = [eye + x[gi] for gi in pairs]
    for _ in range(n_sq):
        x = [mm(x[gi], x[gi]) for gi in pairs]
        tinv = [tinv[gi] + mm(tinv[gi], x[gi]) for gi in pairs]

    ar_s = [[mm_nt(jnp.concatenate([pick(a_st[gi], j), pick(r_st[gi], j)], axis=0), s_prev[gi][j])
             for j in range(nseg)] for gi in pairs]
    mv = [mm(m_ak[gi], v_st[gi]) for gi in pairs]
    u_st = [mm(tinv[gi], unpick([q[0:2 * seg] for q in ar_s[gi]]) + mv[gi]) for gi in pairs]
    o_st = [unpick([q[2 * seg:4 * seg] for q in ar_s[gi]])
            + mm(jnp.concatenate([a_rk[gi], a_rb[gi]], axis=1),
                 jnp.concatenate([v_st[gi], u_st[gi]], axis=0)) for gi in pairs]
    for gi in pairs:
        bh_st = stack(bh, gi)
        kh_st = stack(kh, gi)
        for j in range(nseg):
            pl_row = p_last[j * seg:j * seg + 1, gi * LANES:(gi + 1) * LANES]
            upd = mm_tn(jnp.concatenate([pick(u_st[gi], j), pick(v_st[gi], j)], axis=0),
                        jnp.concatenate([pick(bh_st, j), pick(kh_st, j)], axis=0))
            sbd[gi, j] = s_prev[gi][j] * pl_row + upd

    y = jnp.concatenate([o[0:R] + o[R:R2] for o in o_st], axis=1)
    inv_n = 1.0 / HD_C
    sums = _head_sums(jnp.concatenate([y, r * kp * rk_ref[...]], axis=0), ones_bd, npair)
    mu = sums[0:R] * inv_n
    dy = y - mu
    var = _head_sums(dy * dy, ones_bd, npair) * inv_n
    yn = dy * lax.rsqrt(var + GN_EPS) * lg_ref[...] + lb_ref[...]
    out = (yn + sums[R:R2] * v) * g_ref[...].reshape(R, W)
    y_ref[...] = out.reshape(y_ref.shape).astype(y_ref.dtype)


def _rwkv_scan(rkv, lw, ag, g, s0, kk_p, ka_p, rk_p, lnx_g, lnx_b, t_real, out_dtype, npair=16):
    _, B, T, D = rkv.shape
    seg = min(RWKV_TILE, T)
    nseg = RWKV_TILE // seg
    nc = T // seg
    assert nseg == 1 or nc == 1
    assert B % nseg == 0 and T % seg == 0
    W = npair * LANES
    tok = pl.BlockSpec((nseg, seg, W), lambda b, p, c: (b, c, p))
    rkv_spec = lambda s: pl.BlockSpec((None, nseg, seg, W), lambda b, p, c: (s, b, c, p))
    prm = pl.BlockSpec((1, W), lambda b, p, c: (0, p))
    st = pl.BlockSpec((nseg, 2 * npair, HD_C, HD_C), lambda b, p, c: (b, p, 0, 0))
    row = lambda a: a.reshape(1, D)
    return pl.pallas_call(
        functools.partial(_rwkv_scan_kernel, t_real=t_real, seg=seg, npair=npair),
        out_shape=[jax.ShapeDtypeStruct((B, T, D), out_dtype),
                   jax.ShapeDtypeStruct((B, NH_C, HD_C, HD_C), f32)],
        grid=(B // nseg, D // W, nc),
        in_specs=[rkv_spec(0), rkv_spec(1), rkv_spec(2), tok, tok, tok, st, prm, prm, prm, prm, prm],
        out_specs=[tok, st],
        scratch_shapes=[pltpu.VMEM((npair, nseg, 2 * RWKV_TILE, 2 * RWKV_TILE), f32)],
        compiler_params=_cparams("parallel", "parallel", "arbitrary"),
        name="rwkv_scan",
    )(rkv, rkv, rkv, lw, ag, g, s0, row(kk_p), row(ka_p), row(rk_p), row(lnx_g), row(lnx_b))


def _cast_rows_kernel(x_ref, o_ref):
    o_ref[...] = x_ref[...].astype(o_ref.dtype)


def _compact_rows_bf16(w, rows, n_blocks, skip_after, skip):
    K = w.shape[1]
    assert skip % SUBLANES == 0
    return pl.pallas_call(
        _cast_rows_kernel,
        out_shape=jax.ShapeDtypeStruct((n_blocks * rows, K), bf16),
        grid=(n_blocks,),
        in_specs=[pl.BlockSpec(
            (pl.Element(rows), pl.Element(K)),
            lambda i: (pl.multiple_of(i * rows + jnp.where(i >= skip_after, skip, 0), SUBLANES), 0))],
        out_specs=pl.BlockSpec((rows, K), lambda i: (i, 0)),
        compiler_params=_cparams("parallel"),
        name="compact_rows_bf16",
    )(w)


def _stack3_kernel(a_ref, b_ref, c_ref, o_ref):
    s = pl.program_id(0)
    for i, ref in enumerate((a_ref, b_ref, c_ref)):
        @pl.when(s == i)
        def _():
            o_ref[...] = ref[...].astype(o_ref.dtype)


def _stack3_bf16(a, b, c, rows=512):
    R, K = a.shape
    spec = lambda which: pl.BlockSpec((rows, K), lambda s, r: (jnp.where(s == which, r, 0), 0))
    return pl.pallas_call(
        _stack3_kernel,
        out_shape=jax.ShapeDtypeStruct((3, R, K), bf16),
        grid=(3, R // rows),
        in_specs=[spec(0), spec(1), spec(2)],
        out_specs=pl.BlockSpec((None, rows, K), lambda s, r: (s, r, 0)),
        compiler_params=_cparams("arbitrary", "arbitrary"),
        name="stack3_bf16",
    )(a, b, c)


def _pad_lora(w1, w2):
    rank = w1.shape[1]
    return (jnp.pad(w1, ((0, 0), (0, LORA_PAD - rank))), jnp.pad(w2, ((0, LORA_PAD - rank), (0, 0))))


def _prepare_params(p):
    w_in_t = jnp.swapaxes(p['w_in_ab'][0], 0, 1)
    n_qkvo = 4 * MIX_A
    n_g = 2 * NH_A
    q = {}
    q['w_in'] = _compact_rows_bf16(w_in_t, MIX_A, 6, n_qkvo // MIX_A, n_g)
    q['w_gate'] = jnp.stack(_split_hi_lo(jnp.pad(w_in_t[n_qkvo:n_qkvo + n_g], ((0, LANES - n_g), (0, 0)))))
    q['b_gate'] = jnp.pad(p['b_if_ab'][0], (0, LANES - n_g)).reshape(1, LANES)
    q['w_out_ab'] = p['w_out_ab'][0].astype(bf16)
    q['w_up'] = p['w_up'].astype(bf16)
    q['w_down'] = p['w_down'].astype(bf16)
    q['w_rkv'] = _stack3_bf16(p['rw_wr'][0], p['rw_wk'][0], p['rw_wv'][0])
    mu = p['rw_mu'][0]
    q['mu_rkv'] = jnp.stack([mu[0], mu[2], mu[3]])[:, None, :]
    q['mu_lora'] = jnp.stack([mu[1], mu[4], mu[5]])[:, None, :]
    w1, w2w = _pad_lora(p['rw_w1'][0], p['rw_w2'][0])
    a1, w2a = _pad_lora(p['rw_a1'][0], p['rw_a2'][0])
    q['lora_w1'] = jnp.concatenate([w1, a1, p['rw_g1'][0]], axis=1).astype(bf16)
    q['lora_w2w'] = w2w.astype(bf16)
    q['lora_w2a'] = w2a.astype(bf16)
    q['lora_w2g'] = p['rw_g2'][0].astype(bf16)
    q['w_o'] = p['rw_wo'][0].astype(bf16)
    return q


def _trunk(tok, states, p, q, t_real, out_from=0):
    mC, mn, mm, rgh, rgc, rwS, rwx = states
    B, T, D = tok.batch, tok.t_pad, tok.x.shape[-1]
    M = B * T
    row = lambda a: a.reshape(1, -1)

    act_dtype = bf16 if T % (2 * SUBLANES) == 0 else f32
    zq, zr, gates = _inproj(tok, q['w_in'], q['w_gate'], q['b_gate'], act_dtype)
    zq = zq.reshape(B, T, -1)
    zr = zr.reshape(B, T, -1)
    hm, c_new, n_new, m_new = _mlstm(zq, zr, gates.reshape(B, T, LANES), mC[0], mn[0], mm[0],
                                     p['mlstm_norm_g'][0], t_real, act_dtype)
    yr, h_last = _rglru(zr, rgc[0], rgh[0], p['rg_conv_w'][0], p['rg_conv_b'][0], p['rg_wa'][0],
                        p['rg_ba'][0], p['rg_wx'][0], p['rg_bx'][0], p['rg_lambda'][0], t_real, act_dtype)
    assert t_real >= CONV_W - 1
    conv_new = zr[:, t_real - (CONV_W - 1):t_real, MIX_A:MIX_A + RG_W]
    h1 = _proj_ln([(hm.reshape(M, MIX_A), MIX_A, 0), (yr.reshape(M, RG_W), RG_W, 0)],
                  q['w_out_ab'], tok, row(p['ln1_g'][0]), row(p['ln1_b'][0]))
    h2 = _mlp_ln(h1, q['w_up'], q['w_down'], 0, row(p['ln2_g'][0]), row(p['ln2_b'][0]))

    h2_3 = h2.reshape(B, T, D)
    shift0 = rwx[0].astype(f32)[:, None]
    rkv = _rwkv_rkv(h2_3, shift0, q['mu_rkv'], q['w_rkv'])
    lw, ag, g = _rwkv_lora(h2_3, shift0, q['mu_lora'], q['lora_w1'], q['lora_w2w'], q['lora_w2a'],
                           q['lora_w2g'], row(p['rw_w0'][0]), row(p['rw_a0'][0]))
    r3 = lambda a: a.reshape(B, T, D)
    xo, s_new = _rwkv_scan(rkv.reshape(3, B, T, D), r3(lw), r3(ag), r3(g), rwS[0], p['rw_kk'][0],
                           p['rw_ka'][0], p['rw_rk'][0], p['rw_lnx_g'][0], p['rw_lnx_b'][0], t_real,
                           act_dtype)
    xo = xo.reshape(M, D)
    h3 = _proj_ln([(xo, D, 0)], q['w_o'], h2, row(p['ln1_g'][1]), row(p['ln1_b'][1]))
    n_keep = t_real - out_from
    h4 = _mlp_ln(h3, q['w_up'], q['w_down'], 1, row(p['ln2_g'][1]), row(p['ln2_b'][1]),
                 window=None if (out_from == 0 and t_real == T) else (T, out_from, n_keep))
    shift_new = h2_3[:, t_real - 1]

    new_states = (c_new[None], n_new[None], m_new[None], h_last[None], conv_new[None],
                  s_new[None], shift_new[None])
    return h4.reshape(B, n_keep, D), new_states


def kernel(x_prompt, x_sample, state_mlstm_C, state_mlstm_n, state_mlstm_m, state_rglru_h, state_rglru_conv, state_rwkv_S, state_rwkv_shift, meta_tokens, w_in_ab, b_if_ab, mlstm_norm_g, rg_conv_w, rg_conv_b, rg_wa, rg_ba, rg_wx, rg_bx, rg_lambda, w_out_ab, rw_mu, rw_wr, rw_wk, rw_wv, rw_wo, rw_w0, rw_w1, rw_w2, rw_a0, rw_a1, rw_a2, rw_g1, rw_g2, rw_kk, rw_ka, rw_rk, rw_lnx_g, rw_lnx_b, ln1_g, ln1_b, ln2_g, ln2_b, w_up, w_down):
    p = dict(w_in_ab=w_in_ab, b_if_ab=b_if_ab, mlstm_norm_g=mlstm_norm_g, rg_conv_w=rg_conv_w,
             rg_conv_b=rg_conv_b, rg_wa=rg_wa, rg_ba=rg_ba, rg_wx=rg_wx, rg_bx=rg_bx,
             rg_lambda=rg_lambda, w_out_ab=w_out_ab, rw_mu=rw_mu, rw_wr=rw_wr, rw_wk=rw_wk,
             rw_wv=rw_wv, rw_wo=rw_wo, rw_w0=rw_w0, rw_w1=rw_w1, rw_w2=rw_w2, rw_a0=rw_a0,
             rw_a1=rw_a1, rw_a2=rw_a2, rw_g1=rw_g1, rw_g2=rw_g2, rw_kk=rw_kk, rw_ka=rw_ka,
             rw_rk=rw_rk, rw_lnx_g=rw_lnx_g, rw_lnx_b=rw_lnx_b, ln1_g=ln1_g, ln1_b=ln1_b,
             ln2_g=ln2_g, ln2_b=ln2_b, w_up=w_up, w_down=w_down)
    q = _prepare_params(p)
    B, S, D = x_prompt.shape
    dt = x_prompt.dtype
    t_prompt = N_META + S
    t_pad = -(-t_prompt // MLSTM_CHUNK) * MLSTM_CHUNK
    zero_states = (
        jnp.zeros((1, B, NH_A, HD_A, HD_A), f32),
        jnp.zeros((1, B, NH_A, HD_A), f32),
        jnp.zeros((1, B, NH_A), f32),
        jnp.zeros((1, B, RG_W), f32),
        jnp.zeros((1, B, CONV_W - 1, RG_W), dt),
        jnp.zeros((1, B, NH_C, HD_C, HD_C), f32),
        jnp.zeros((1, B, D), dt),
    )
    hp, ps = _trunk(_Tokens(x_prompt, meta_tokens.astype(dt), t_pad), zero_states, p, q, t_prompt,
                    out_from=N_META)
    sample_states = (state_mlstm_C, state_mlstm_n, state_mlstm_m, state_rglru_h, state_rglru_conv,
                     state_rwkv_S, state_rwkv_shift)
    hs, ss = _trunk(_Tokens(x_sample, None, x_sample.shape[1]), sample_states, p, q, x_sample.shape[1])
    return (hp, hs, *ps, *ss)
```
